```python
import math
import jax
import jax.numpy as jnp
from jax import lax
import numpy as np

D_MODEL = 1024
BATCH = 8
SEQ = 2048
DEPTH = 4
DEC_BATCH = 32
DEC_SEQ = 1
PAST_LEN = 8192
PAGE_SIZE = 128

HEAD_DIM = 64
HEADS_PER_GROUP = 4
ATTN_GROUPS = ((128, 1), (512, 4), (2048, 16))
N_ATTN_HEADS = HEADS_PER_GROUP * len(ATTN_GROUPS)
ATTN_QKV_WIDTH = N_ATTN_HEADS * HEAD_DIM
ATTN_OUT_WIDTH = HEADS_PER_GROUP * HEAD_DIM
ROT_DIM = HEAD_DIM // 4
ROPE_THETA = 500000.0
BAND_BLOCK = 128
CHUNK = 128
GMLP_GROUPS = 4
GMLP_GROUP_DIM = 192
GMLP_WIDTH = GMLP_GROUPS * GMLP_GROUP_DIM
EPS = 1e-6
IN_WIDTHS = (ATTN_QKV_WIDTH, ATTN_QKV_WIDTH, ATTN_QKV_WIDTH, ATTN_OUT_WIDTH, GMLP_WIDTH, GMLP_WIDTH, GMLP_WIDTH, 2 * D_MODEL)
N_IN = sum(IN_WIDTHS)
SPLITS = tuple(sum(IN_WIDTHS[:i + 1]) for i in range(len(IN_WIDTHS) - 1))

kernel_name = 'hybrid_dilated_attn_gmlp_decode_step'


def rms_norm(x, g):
    xf = x.astype(jnp.float32)
    y = xf * lax.rsqrt(jnp.mean(xf * xf, axis=-1, keepdims=True) + EPS)
    return (y * g.astype(jnp.float32)).astype(x.dtype)


def layer_norm(x, g, b):
    xf = x.astype(jnp.float32)
    xc = xf - jnp.mean(xf, axis=-1, keepdims=True)
    y = xc * lax.rsqrt(jnp.mean(xc * xc, axis=-1, keepdims=True) + EPS)
    return (y * g.astype(jnp.float32) + b.astype(jnp.float32)).astype(x.dtype)


def rope_partial(x, pos):
    half = ROT_DIM // 2
    inv_freq = jnp.power(ROPE_THETA, -jnp.arange(half, dtype=jnp.float32) / half)
    ang = pos.astype(jnp.float32)[:, None] * inv_freq[None, :]
    cos = jnp.cos(ang)[:, None, :]
    sin = jnp.sin(ang)[:, None, :]
    xr = x[..., :ROT_DIM].astype(jnp.float32)
    x1, x2 = xr[..., :half], xr[..., half:]
    rot = jnp.concatenate([x1 * cos - x2 * sin, x2 * cos + x1 * sin], axis=-1)
    return jnp.concatenate([rot.astype(x.dtype), x[..., ROT_DIM:]], axis=-1)


def dilated_band_prompt(q, k, v, window, dilation):
    B, S, H, hd = q.shape
    n = S // dilation
    span = window // dilation
    qlen = math.gcd(n, BAND_BLOCK)
    nblk = n // qlen

    def to_sub(t):
        return t.reshape(B, n, dilation, H, hd).transpose(0, 2, 1, 3, 4)

    qs = to_sub(q).reshape(B, dilation, nblk, qlen, H, hd)
    pad = ((0, 0), (0, 0), (span, 0), (0, 0), (0, 0))
    kp = jnp.pad(to_sub(k), pad)
    vp = jnp.pad(to_sub(v), pad)
    kidx = jnp.arange(nblk)[:, None] * qlen + jnp.arange(qlen + span)[None, :]
    kb = kp[:, :, kidx]
    vb = vp[:, :, kidx]
    blk = jnp.arange(nblk)[:, None, None]
    qi = jnp.arange(qlen)[None, :, None]
    kj = jnp.arange(qlen + span)[None, None, :]
    dist = qi - kj + span
    valid = (dist >= 0) & (dist <= span) & (blk * qlen + kj >= span)
    scores = jnp.einsum('brnqhd,brnkhd->brnhqk', qs, kb, preferred_element_type=jnp.float32) * (hd ** -0.5)
    scores = jnp.where(valid[:, None], scores, -jnp.inf)
    lse = jax.nn.logsumexp(scores, axis=-1)
    p = jnp.exp(scores - lse[..., None]).astype(v.dtype)
    o = jnp.einsum('brnhqk,brnkhd->brnqhd', p, vb)
    o = o.reshape(B, dilation, n, H, hd).transpose(0, 2, 1, 3, 4).reshape(B, S, H, hd)
    lse = jnp.swapaxes(lse, -1, -2).reshape(B, dilation, n, H).transpose(0, 2, 1, 3).reshape(B, S, H)
    return o, lse


def dilated_band_sample(q, k, v, kv_cache, window, dilation):
    L = kv_cache.shape[1]
    T = q.shape[1]
    hd = q.shape[-1]
    span = window // dilation
    kc = jnp.concatenate([kv_cache[:, :, 0], k], axis=1)
    vc = jnp.concatenate([kv_cache[:, :, 1], v], axis=1)
    rows = L + jnp.arange(T)[:, None] - dilation * jnp.arange(span + 1)[None, :]
    valid = rows >= 0
    rows = jnp.maximum(rows, 0)
    kg = kc[:, rows]
    vg = vc[:, rows]
    scores = jnp.einsum('bthd,btjhd->bhtj', q, kg, preferred_element_type=jnp.float32) * (hd ** -0.5)
    scores = jnp.where(valid[None, None], scores, -jnp.inf)
    lse = jax.nn.logsumexp(scores, axis=-1)
    p = jnp.exp(scores - lse[..., None]).astype(v.dtype)
    o = jnp.einsum('bhtj,btjhd->bthd', p, vg)
    return o, jnp.swapaxes(lse, 1, 2)


def layer_step(x, pos, kv_caches, g_pre, w_in, b_merge, v_norm_g, v_norm_b, w_s, b_s, w_pa, w_pb, w_out, g_post):
    B, T, _ = x.shape
    z = rms_norm(x, g_pre) @ w_in
    q, k, v, gate_a, u, vb, gate_b, merge_logits = jnp.split(z, SPLITS, axis=-1)
    q = rope_partial(q.reshape(B, T, N_ATTN_HEADS, HEAD_DIM), pos)
    k = rope_partial(k.reshape(B, T, N_ATTN_HEADS, HEAD_DIM), pos)
    v = v.reshape(B, T, N_ATTN_HEADS, HEAD_DIM)
    outs, lses, kv_rows = [], [], []
    for g, (window, dilation) in enumerate(ATTN_GROUPS):
        hs = slice(g * HEADS_PER_GROUP, (g + 1) * HEADS_PER_GROUP)
        qg, kg, vg = q[:, :, hs], k[:, :, hs], v[:, :, hs]
        if kv_caches is None:
            o, lse = dilated_band_prompt(qg, kg, vg, window, dilation)
            keep = min(window, T)
            kv_rows.append(jnp.stack([kg[:, T - keep:], vg[:, T - keep:]], axis=2))
        else:
            o, lse = dilated_band_sample(qg, kg, vg, kv_caches[g], window, dilation)
            kv_rows.append(jnp.stack([kg, vg], axis=2))
        outs.append(o)
        lses.append(lse)
    alpha = jax.nn.softmax(jnp.stack(lses, axis=0), axis=0)[..., None]
    attn = jnp.sum(alpha * jnp.stack(outs, axis=0).astype(jnp.float32), axis=0)
    attn = attn.reshape(B, T, ATTN_OUT_WIDTH).astype(x.dtype)
    u = jax.nn.gelu(u)
    vn = layer_norm(jax.nn.gelu(vb), v_norm_g, v_norm_b)
    cl = CHUNK if kv_caches is None else T
    ws = jnp.where(jnp.tril(jnp.ones((CHUNK, CHUNK), dtype=bool)), w_s, 0)[:, :cl, :cl]
    vc = vn.reshape(B, T // cl, cl, GMLP_GROUPS, GMLP_GROUP_DIM)
    mix = jnp.einsum('gij,bnjgc->bnigc', ws, vc) + jnp.swapaxes(b_s[:, :cl], 0, 1)[:, :, None]
    sgu = u * mix.reshape(B, T, GMLP_WIDTH)
    branch_a = (attn * jax.nn.silu(gate_a)) @ w_pa
    branch_b = (sgu * jax.nn.silu(gate_b)) @ w_pb
    gates = jax.nn.sigmoid(merge_logits + b_merge)
    merged = gates[..., :D_MODEL] * branch_a + gates[..., D_MODEL:] * branch_b
    x = x + rms_norm(merged @ w_out, g_post)
    return x, kv_rows, vn


def setup_inputs(seed: int = 0) -> dict:
    key = jax.random.key(seed)
    ks = jax.random.split(key, 16)

    def nrm(k, shape, scale=1.0):
        return scale * jax.random.normal(k, shape, jnp.float32)

    def cache_shape(window):
        return (DEPTH, DEC_BATCH, min(window, PAST_LEN), 2, HEADS_PER_GROUP, HEAD_DIM)

    return {
        'x_prompt': nrm(ks[0], (BATCH, SEQ, D_MODEL)),
        'x_sample': nrm(ks[1], (DEC_BATCH, DEC_SEQ, D_MODEL)),
        'cache_kv_w128': nrm(ks[2], cache_shape(ATTN_GROUPS[0][0])),
        'cache_kv_w512': nrm(ks[3], cache_shape(ATTN_GROUPS[1][0])),
        'cache_kv_w2048': nrm(ks[4], cache_shape(ATTN_GROUPS[2][0])),
        'norm_pre': 1.0 + nrm(ks[5], (DEPTH, D_MODEL), 0.02),
        'w_in': nrm(ks[6], (DEPTH, D_MODEL, N_IN), D_MODEL ** -0.5),
        'b_merge': nrm(ks[7], (DEPTH, 2 * D_MODEL), 0.02),
        'v_norm_g': 1.0 + nrm(ks[8], (DEPTH, GMLP_WIDTH), 0.02),
        'v_norm_b': nrm(ks[9], (DEPTH, GMLP_WIDTH), 0.02),
        'w_spatial': nrm(ks[10], (DEPTH, GMLP_GROUPS, CHUNK, CHUNK), CHUNK ** -0.5),
        'b_spatial': 1.0 + nrm(ks[11], (DEPTH, GMLP_GROUPS, CHUNK), 0.02),
        'w_proj_a': nrm(ks[12], (DEPTH, ATTN_OUT_WIDTH, D_MODEL), ATTN_OUT_WIDTH ** -0.5),
        'w_proj_b': nrm(ks[13], (DEPTH, GMLP_WIDTH, D_MODEL), GMLP_WIDTH ** -0.5),
        'w_out': nrm(ks[14], (DEPTH, D_MODEL, D_MODEL), D_MODEL ** -0.5),
        'norm_post': 1.0 + nrm(ks[15], (DEPTH, D_MODEL), 0.02),
    }


def reference(x_prompt, x_sample, cache_kv_w128, cache_kv_w512, cache_kv_w2048, norm_pre, w_in, b_merge, v_norm_g, v_norm_b, w_spatial, b_spatial, w_proj_a, w_proj_b, w_out, norm_post):
    pos_prompt = jnp.arange(x_prompt.shape[1], dtype=jnp.int32)
    pos_sample = PAST_LEN + jnp.arange(x_sample.shape[1], dtype=jnp.int32)
    caches = (cache_kv_w128, cache_kv_w512, cache_kv_w2048)
    xp, xs = x_prompt, x_sample
    kv_p = [[] for _ in ATTN_GROUPS]
    kv_s = [[] for _ in ATTN_GROUPS]
    v_s = []
    for l in range(DEPTH):
        w = (norm_pre[l], w_in[l], b_merge[l], v_norm_g[l], v_norm_b[l], w_spatial[l], b_spatial[l], w_proj_a[l], w_proj_b[l], w_out[l], norm_post[l])
        xp, rows_p, _ = layer_step(xp, pos_prompt, None, *w)
        xs, rows_s, vn_s = layer_step(xs, pos_sample, [c[l] for c in caches], *w)
        for g in range(len(ATTN_GROUPS)):
            kv_p[g].append(rows_p[g])
            kv_s[g].append(rows_s[g])
        v_s.append(vn_s)
    new_kv_w128_prompt = jnp.stack(kv_p[0], axis=0)
    new_kv_w512_prompt = jnp.stack(kv_p[1], axis=0)
    new_kv_w2048_prompt = jnp.stack(kv_p[2], axis=0)
    new_kv_w128_sample = jnp.stack(kv_s[0], axis=0)
    new_kv_w512_sample = jnp.stack(kv_s[1], axis=0)
    new_kv_w2048_sample = jnp.stack(kv_s[2], axis=0)
    new_gmlp_v_sample = jnp.stack(v_s, axis=0)
    return (xp, xs, new_kv_w128_prompt, new_kv_w512_prompt, new_kv_w2048_prompt, new_kv_w128_sample, new_kv_w512_sample, new_kv_w2048_sample, new_gmlp_v_sample)
```

```python
import functools

import jax
import jax.numpy as jnp
from jax import lax
from jax.experimental import pallas as pl
from jax.experimental.pallas import tpu as pltpu

F32 = jnp.float32
BF16 = jnp.bfloat16

PAST_LEN = 8192
HEAD_DIM = 64
HEADS_PER_GROUP = 4
GROUP_W = HEADS_PER_GROUP * HEAD_DIM
ATTN_GROUPS = ((128, 1), (512, 4), (2048, 16))
N_GROUPS = len(ATTN_GROUPS)
QKV_W = N_GROUPS * GROUP_W
ROT_DIM = HEAD_DIM // 4
ROT_HALF = ROT_DIM // 2
ROPE_THETA = 500000.0
BLK = 128
GMLP_GROUPS = 4
GMLP_GROUP_DIM = 192
GMLP_W = GMLP_GROUPS * GMLP_GROUP_DIM
EPS = 1e-6
LANES = 128
NEG = -1e30

OFF_GA = 0
OFF_U = OFF_GA + GROUP_W
OFF_VB = OFF_U + GMLP_W
OFF_GB = OFF_VB + GMLP_W
OFF_ML = OFF_GB + GMLP_W

QKV_TILE = 512
MLP_TILE = 256
SAMPLE_BT = 8
VMEM_LIMIT = 56 * 1024 * 1024


def _rms(x, g):
    return (x * lax.rsqrt(jnp.mean(x * x, axis=-1, keepdims=True) + EPS)) * g


def _layer_norm(x, g, b):
    xc = x - jnp.mean(x, axis=-1, keepdims=True)
    return xc * lax.rsqrt(jnp.mean(xc * xc, axis=-1, keepdims=True) + EPS) * g + b


def _rope(z, c, sa, sb):
    outs = []
    for s in range(GROUP_W // LANES):
        zs = z[:, s * LANES:(s + 1) * LANES]
        outs.append(zs * c + pltpu.roll(zs, LANES - ROT_HALF, 1) * sa + pltpu.roll(zs, ROT_HALF, 1) * sb)
    return jnp.concatenate(outs, axis=-1)


def _rope_tables(pos, scale):
    inv_freq = jnp.power(ROPE_THETA, -jnp.arange(ROT_HALF, dtype=F32) / ROT_HALF)
    ang = pos.astype(F32)[:, None] * inv_freq[None, :]
    cos, sin = jnp.cos(ang), jnp.sin(ang)
    n = pos.shape[0]
    rest1 = jnp.ones((n, HEAD_DIM - ROT_DIM), F32)
    rest0 = jnp.zeros((n, HEAD_DIM - ROT_DIM), F32)
    z8 = jnp.zeros((n, ROT_HALF), F32)
    c = jnp.concatenate([cos, cos, rest1], axis=-1)
    sa = jnp.concatenate([-sin, z8, rest0], axis=-1)
    sb = jnp.concatenate([z8, sin, rest0], axis=-1)
    reps = LANES // HEAD_DIM
    return tuple(jnp.tile(t, (1, reps)) * scale for t in (c, sa, sb))


def _activations(z, b_merge, vng, vnb):
    ga = jax.nn.silu(z[:, OFF_GA:OFF_U])
    u = jax.nn.gelu(z[:, OFF_U:OFF_VB])
    vn = _layer_norm(jax.nn.gelu(z[:, OFF_VB:OFF_GB]), vng, vnb)
    gb = jax.nn.silu(z[:, OFF_GB:OFF_ML])
    gates = jax.nn.sigmoid(z[:, OFF_ML:] + b_merge)
    return ga, u, vn, gb, gates


def _merge_out(x, a_in, b_in, gates, wpa, wpb, wout, gpost):
    d = x.shape[-1]
    ba = jnp.dot(a_in.astype(BF16), wpa, preferred_element_type=F32)
    bb = jnp.dot(b_in.astype(BF16), wpb, preferred_element_type=F32)
    merged = gates[:, :d] * ba + gates[:, d:] * bb
    y = jnp.dot(merged.astype(BF16), wout, preferred_element_type=F32)
    return x + _rms(y, gpost)


def _const_spec(shape):
    nd = len(shape)
    return pl.BlockSpec(shape, lambda *_: (0,) * nd)


def _qkv_kernel(x_ref, g_ref, w_ref, cq_ref, saq_ref, sbq_ref, ck_ref, sak_ref, sbk_ref,
                q1_ref, q2_ref, q3_ref, kv1_ref, kv2_ref, kv3_ref):
    hb = _rms(x_ref[...], g_ref[...]).astype(BF16)
    q_refs = (q1_ref, q2_ref, q3_ref)
    kv_refs = (kv1_ref, kv2_ref, kv3_ref)
    cq, saq, sbq = cq_ref[...], saq_ref[...], sbq_ref[...]
    ck, sak, sbk = ck_ref[...], sak_ref[...], sbk_ref[...]
    for g in range(N_GROUPS):
        zq = jnp.dot(hb, w_ref[:, g * GROUP_W:(g + 1) * GROUP_W], preferred_element_type=F32)
        q_refs[g][...] = _rope(zq, cq, saq, sbq).astype(BF16)
        zk = jnp.dot(hb, w_ref[:, QKV_W + g * GROUP_W:QKV_W + (g + 1) * GROUP_W], preferred_element_type=F32)
        kv_refs[g][:, 0:GROUP_W] = _rope(zk, ck, sak, sbk)
        zv = jnp.dot(hb, w_ref[:, 2 * QKV_W + g * GROUP_W:2 * QKV_W + (g + 1) * GROUP_W],
                     preferred_element_type=F32)
        kv_refs[g][:, GROUP_W:2 * GROUP_W] = zv


def _qkv_call(x, g_pre, w_qkv, tabs_q, tabs_k, seq):
    n, d = x.shape
    t = QKV_TILE
    tiles_per_seq = seq // t
    row = lambda i: (i, 0)
    tab_spec = pl.BlockSpec((t, LANES), lambda i: (i % tiles_per_seq, 0))
    return pl.pallas_call(
        _qkv_kernel,
        grid=(n // t,),
        in_specs=[pl.BlockSpec((t, d), row), _const_spec((1, d)), _const_spec(w_qkv.shape)] + [tab_spec] * 6,
        out_specs=[pl.BlockSpec((t, GROUP_W), row)] * 3 + [pl.BlockSpec((t, 2 * GROUP_W), row)] * 3,
        out_shape=[jax.ShapeDtypeStruct((n, GROUP_W), BF16)] * 3
        + [jax.ShapeDtypeStruct((n, 2 * GROUP_W), F32)] * 3,
        compiler_params=pltpu.CompilerParams(dimension_semantics=("arbitrary",), vmem_limit_bytes=VMEM_LIMIT),
        name="prompt_qkv",
    )(x, g_pre, w_qkv, *tabs_q, *tabs_k)


def _attn_block(q, k, v, bias):
    nk = k.shape[0]
    hid = lax.shift_right_logical(lax.broadcasted_iota(jnp.int32, (BLK, GROUP_W), 1), HEAD_DIM.bit_length() - 1)
    qf = q.astype(F32)
    qs = jnp.concatenate([jnp.where(hid == h, qf, 0.0) for h in range(HEADS_PER_GROUP)], axis=0).astype(BF16)
    s = lax.dot_general(qs, k.astype(BF16), (((1,), (1,)), ((), ())), preferred_element_type=F32)
    s = s.reshape(HEADS_PER_GROUP, BLK, nk) + bias[None]
    m = jnp.max(s, axis=-1, keepdims=True)
    p = jnp.exp(s - m)
    l = jnp.sum(p, axis=-1, keepdims=True)
    pv = jnp.dot(p.reshape(HEADS_PER_GROUP * BLK, nk).astype(BF16), v.astype(BF16), preferred_element_type=F32)
    pv = pv.reshape(HEADS_PER_GROUP, BLK, GROUP_W)
    inv = 1.0 / l
    lse = m + jnp.log(l)
    o = jnp.zeros((BLK, GROUP_W), F32)
    ls = jnp.zeros((BLK, GROUP_W), F32)
    for h in range(HEADS_PER_GROUP):
        o = jnp.where(hid == h, pv[h] * inv[h], o)
        ls = jnp.where(hid == h, lse[h], ls)
    return o, ls


def _attn_kernel(q1_ref, q2_ref, q3_ref, kv1_ref, kv2_ref, kv3_ref, band_ref, causal_ref, out_ref,
                 o1_s, l1_s, o2_s, l2_s, o3_s, l3_s):
    j = pl.program_id(1)
    nslab = GROUP_W // LANES
    d2 = ATTN_GROUPS[1][1]
    d3 = ATTN_GROUPS[2][1]
    blocks_per_res2 = pl.num_programs(1) // d2

    def store(o_s, l_s, rows, o, ls):
        for s in range(nslab):
            o_s[s, rows, :] = o[:, s * LANES:(s + 1) * LANES]
            l_s[s, rows, :] = ls[:, s * LANES:(s + 1) * LANES]

    start = pl.multiple_of(jnp.maximum(j - 1, 0) * BLK, BLK)
    o, ls = _attn_block(q1_ref[...], kv1_ref[pl.ds(start, 2 * BLK), 0:GROUP_W],
                        kv1_ref[pl.ds(start, 2 * BLK), GROUP_W:2 * GROUP_W], band_ref[jnp.minimum(j, 1)])
    store(o1_s, l1_s, pl.ds(pl.multiple_of(j * BLK, BLK), BLK), o, ls)

    n = j % blocks_per_res2
    start = pl.multiple_of(jnp.maximum(n - 1, 0) * BLK, BLK)
    o, ls = _attn_block(q2_ref[...], kv2_ref[pl.ds(start, 2 * BLK), 0:GROUP_W],
                        kv2_ref[pl.ds(start, 2 * BLK), GROUP_W:2 * GROUP_W], band_ref[jnp.minimum(n, 1)])
    store(o2_s, l2_s, pl.ds(n * (BLK * d2) + j // blocks_per_res2, BLK, stride=d2), o, ls)

    o, ls = _attn_block(q3_ref[...], kv3_ref[:, 0:GROUP_W], kv3_ref[:, GROUP_W:2 * GROUP_W], causal_ref[...])
    store(o3_s, l3_s, pl.ds(j, BLK, stride=d3), o, ls)

    @pl.when(j == pl.num_programs(1) - 1)
    def _():
        seq = out_ref.shape[0]
        rows_per_step = 256
        for s in range(nslab):
            def body(c, carry):
                rows = pl.ds(pl.multiple_of(c * rows_per_step, rows_per_step), rows_per_step)
                l1, l2, l3 = l1_s[s, rows, :], l2_s[s, rows, :], l3_s[s, rows, :]
                m = jnp.maximum(jnp.maximum(l1, l2), l3)
                e1, e2, e3 = jnp.exp(l1 - m), jnp.exp(l2 - m), jnp.exp(l3 - m)
                inv = 1.0 / (e1 + e2 + e3)
                out = (e1 * inv) * o1_s[s, rows, :] + (e2 * inv) * o2_s[s, rows, :] + (e3 * inv) * o3_s[s, rows, :]
                out_ref[rows, s * LANES:(s + 1) * LANES] = out
                return carry
            lax.fori_loop(0, seq // rows_per_step, body, 0)


def _attn_call(q1, q2, q3, kv1, kv2, kv3, band, causal, batch, seq):
    d2 = ATTN_GROUPS[1][1]
    d3 = ATTN_GROUPS[2][1]
    nblk = seq // BLK
    assert seq // d3 == BLK and nblk % d2 == 0 and nblk == d3
    bpr2 = nblk // d2
    q1v = q1.reshape(batch, seq, GROUP_W)
    q2v = q2.reshape(batch, seq // d2, d2 * GROUP_W)
    q3v = q3.reshape(batch, seq // d3, d3 * GROUP_W)
    kv1v = kv1.reshape(batch, seq, 2 * GROUP_W)
    kv2v = kv2.reshape(batch, seq // d2, d2 * 2 * GROUP_W)
    kv3v = kv3.reshape(batch, seq // d3, d3 * 2 * GROUP_W)
    scratch = [pltpu.VMEM((GROUP_W // LANES, seq, LANES), F32)] * 6
    return pl.pallas_call(
        _attn_kernel,
        grid=(batch, nblk),
        in_specs=[
            pl.BlockSpec((None, BLK, GROUP_W), lambda b, j: (b, j, 0)),
            pl.BlockSpec((None, BLK, GROUP_W), lambda b, j: (b, j % bpr2, j // bpr2)),
            pl.BlockSpec((None, BLK, GROUP_W), lambda b, j: (b, 0, j)),
            pl.BlockSpec((None, seq, 2 * GROUP_W), lambda b, j: (b, 0, 0)),
            pl.BlockSpec((None, seq // d2, 2 * GROUP_W), lambda b, j: (b, 0, j // bpr2)),
            pl.BlockSpec((None, BLK, 2 * GROUP_W), lambda b, j: (b, 0, j)),
            _const_spec(band.shape),
            _const_spec(causal.shape),
        ],
        out_specs=pl.BlockSpec((None, seq, GROUP_W), lambda b, j: (b, 0, 0)),
        out_shape=jax.ShapeDtypeStruct((batch, seq, GROUP_W), F32),
        scratch_shapes=scratch,
        compiler_params=pltpu.CompilerParams(dimension_semantics=("arbitrary", "arbitrary"),
                                             vmem_limit_bytes=VMEM_LIMIT),
        name="prompt_attn",
    )(q1v, q2v, q3v, kv1v, kv2v, kv3v, band, causal)


def _attn_bias_tables():
    qi = jnp.arange(BLK)[:, None]
    kj = jnp.arange(2 * BLK)[None, :]
    general = (kj >= qi) & (kj <= qi + BLK)
    first = kj <= qi
    band = jnp.where(jnp.stack([first, general]), 0.0, NEG).astype(F32)
    causal = jnp.where(jnp.arange(BLK)[None, :] <= qi, 0.0, NEG).astype(F32)
    return band, causal


def _mlp_kernel(x_ref, attn_ref, gpre_ref, w_ref, bm_ref, vng_ref, vnb_ref, ws_ref, bsp_ref,
                wpa_ref, wpb_ref, wout_ref, gpost_ref, out_ref):
    x = x_ref[...]
    rows = x.shape[0]
    hb = _rms(x, gpre_ref[...]).astype(BF16)
    z = jnp.dot(hb, w_ref[...], preferred_element_type=F32)
    ga, u, vn, gb, gates = _activations(z, bm_ref[...], vng_ref[...], vnb_ref[...])
    wr = lax.broadcasted_iota(jnp.int32, ws_ref.shape, 0) & (BLK - 1)
    wc = lax.broadcasted_iota(jnp.int32, ws_ref.shape, 1)
    ws = jnp.where(wc <= wr, ws_ref[...], 0.0).astype(BF16)
    lane = lax.broadcasted_iota(jnp.int32, (BLK, GMLP_W), 1)
    vnb16 = vn.astype(BF16)
    mixes = []
    for c in range(rows // BLK):
        r = jnp.dot(ws, vnb16[c * BLK:(c + 1) * BLK], preferred_element_type=F32)
        mix = r[(GMLP_GROUPS - 1) * BLK:]
        for g in range(GMLP_GROUPS - 2, -1, -1):
            mix = jnp.where(lane < (g + 1) * GMLP_GROUP_DIM, r[g * BLK:(g + 1) * BLK], mix)
        mixes.append(mix + bsp_ref[...])
    mix = jnp.concatenate(mixes, axis=0)
    out_ref[...] = _merge_out(x, attn_ref[...] * ga, (u * mix) * gb, gates,
                              wpa_ref[...], wpb_ref[...], wout_ref[...], gpost_ref[...])


def _mlp_call(x, attn, g_pre, w_rest, b_merge, vng, vnb, ws_stack, bs_full, wpa, wpb, wout, g_post):
    n, d = x.shape
    t = MLP_TILE
    row = lambda i: (i, 0)
    consts = (g_pre, w_rest, b_merge, vng, vnb, ws_stack, bs_full, wpa, wpb, wout, g_post)
    return pl.pallas_call(
        _mlp_kernel,
        grid=(n // t,),
        in_specs=[pl.BlockSpec((t, d), row), pl.BlockSpec((t, GROUP_W), row)] + [_const_spec(c.shape) for c in consts],
        out_specs=pl.BlockSpec((t, d), row),
        out_shape=jax.ShapeDtypeStruct((n, d), F32),
        compiler_params=pltpu.CompilerParams(dimension_semantics=("arbitrary",), vmem_limit_bytes=VMEM_LIMIT),
        name="prompt_mlp",
    )(x, attn, *consts)


def _sample_in_kernel(x_ref, g_ref, w_ref, cq_ref, saq_ref, sbq_ref, ck_ref, sak_ref, sbk_ref,
                      bm_ref, vng_ref, vnb_ref,
                      q_ref, kv1_ref, kv2_ref, kv3_ref, ga_ref, u_ref, vn_ref, gb_ref, gates_ref):
    hb = _rms(x_ref[...], g_ref[...]).astype(BF16)
    z = jnp.dot(hb, w_ref[...], preferred_element_type=F32)
    kv_refs = (kv1_ref, kv2_ref, kv3_ref)
    for g in range(N_GROUPS):
        lo = g * GROUP_W
        q_ref[:, lo:lo + GROUP_W] = _rope(z[:, lo:lo + GROUP_W], cq_ref[...], saq_ref[...], sbq_ref[...])
        kv_refs[g][:, 0:GROUP_W] = _rope(z[:, QKV_W + lo:QKV_W + lo + GROUP_W], ck_ref[...], sak_ref[...], sbk_ref[...])
        kv_refs[g][:, GROUP_W:2 * GROUP_W] = z[:, 2 * QKV_W + lo:2 * QKV_W + lo + GROUP_W]
    ga, u, vn, gb, gates = _activations(z[:, 3 * QKV_W:], bm_ref[...], vng_ref[...], vnb_ref[...])
    ga_ref[...] = ga
    u_ref[...] = u
    vn_ref[...] = vn
    gb_ref[...] = gb
    gates_ref[...] = gates


def _sample_in_call(x, g_pre, w_in, tabs_q, tabs_k, b_merge, vng, vnb):
    n, d = x.shape
    sds = lambda w: jax.ShapeDtypeStruct((n, w), F32)
    return pl.pallas_call(
        _sample_in_kernel,
        out_shape=[sds(QKV_W)] + [sds(2 * GROUP_W)] * 3 + [sds(GROUP_W), sds(GMLP_W), sds(GMLP_W), sds(GMLP_W), sds(2 * d)],
        compiler_params=pltpu.CompilerParams(vmem_limit_bytes=VMEM_LIMIT),
        name="sample_in",
    )(x, g_pre, w_in, *tabs_q, *tabs_k, b_merge, vng, vnb)


def _sample_attn_kernel(q_ref, kvn1_ref, kvn2_ref, kvn3_ref, c1_ref, c2_ref, c3_ref, e_ref, et_ref, out_ref):
    bt = q_ref.shape[0]
    e = e_ref[...]
    et = et_ref[...]
    hdot = functools.partial(jnp.dot, precision=lax.Precision.HIGHEST, preferred_element_type=F32)
    rows = []
    for b in range(bt):
        outs, lses = [], []
        for g, (kvn_ref, c_ref) in enumerate(((kvn1_ref, c1_ref), (kvn2_ref, c2_ref), (kvn3_ref, c3_ref))):
            q = q_ref[b:b + 1, g * GROUP_W:(g + 1) * GROUP_W]
            kc, vc = c_ref[b, :, 0:GROUP_W], c_ref[b, :, GROUP_W:2 * GROUP_W]
            kn, vn = kvn_ref[b:b + 1, 0:GROUP_W], kvn_ref[b:b + 1, GROUP_W:2 * GROUP_W]
            sc = hdot(kc * q, e)
            ss = hdot(kn * q, e)
            m = jnp.maximum(jnp.max(sc, axis=0, keepdims=True), ss)
            pc = jnp.exp(sc - m)
            ps = jnp.exp(ss - m)
            l = jnp.sum(pc, axis=0, keepdims=True) + ps
            o = jnp.sum(hdot(pc, et) * vc, axis=0, keepdims=True) + hdot(ps, et) * vn
            le = hdot(l, et)
            outs.append(o / le)
            lses.append(hdot(m, et) + jnp.log(le))
        m = jnp.maximum(jnp.maximum(lses[0], lses[1]), lses[2])
        es = [jnp.exp(l - m) for l in lses]
        inv = 1.0 / (es[0] + es[1] + es[2])
        rows.append((es[0] * inv) * outs[0] + (es[1] * inv) * outs[1] + (es[2] * inv) * outs[2])
    out_ref[...] = jnp.concatenate(rows, axis=0)


def _sample_attn_call(layer, q, kvn, caches, e, et):
    n = q.shape[0]
    bt = SAMPLE_BT
    row = lambda i: (i, 0)
    cache_specs = []
    cache_views = []
    for (window, dil), c in zip(ATTN_GROUPS, caches):
        depth, nb, rows = c.shape[:3]
        assert rows == window, "cache must hold exactly the window"
        cache_views.append(c.reshape(depth, nb, rows // dil, dil * 2 * GROUP_W))
        cache_specs.append(pl.BlockSpec((None, bt, rows // dil, 2 * GROUP_W), lambda i: (layer, i, 0, 0)))
    return pl.pallas_call(
        _sample_attn_kernel,
        grid=(n // bt,),
        in_specs=[pl.BlockSpec((bt, QKV_W), row)] + [pl.BlockSpec((bt, 2 * GROUP_W), row)] * 3 + cache_specs
        + [_const_spec(e.shape), _const_spec(et.shape)],
        out_specs=pl.BlockSpec((bt, GROUP_W), row),
        out_shape=jax.ShapeDtypeStruct((n, GROUP_W), F32),
        compiler_params=pltpu.CompilerParams(dimension_semantics=("arbitrary",), vmem_limit_bytes=VMEM_LIMIT),
        name="sample_attn",
    )(q, *kvn, *cache_views, e, et)


def _sample_out_kernel(x_ref, attn_ref, ga_ref, u_ref, vn_ref, gb_ref, gates_ref, ws0_ref, bs0_ref,
                       wpa_ref, wpb_ref, wout_ref, gpost_ref, out_ref):
    mix = vn_ref[...] * ws0_ref[...] + bs0_ref[...]
    out_ref[...] = _merge_out(x_ref[...], attn_ref[...] * ga_ref[...], (u_ref[...] * mix) * gb_ref[...],
                              gates_ref[...], wpa_ref[...], wpb_ref[...], wout_ref[...], gpost_ref[...])


def _sample_out_call(x, attn, ga, u, vn, gb, gates, ws0, bs0, wpa, wpb, wout, g_post):
    return pl.pallas_call(
        _sample_out_kernel,
        out_shape=jax.ShapeDtypeStruct(x.shape, F32),
        compiler_params=pltpu.CompilerParams(vmem_limit_bytes=VMEM_LIMIT),
        name="sample_out",
    )(x, attn, ga, u, vn, gb, gates, ws0, bs0, wpa, wpb, wout, g_post)


def kernel(x_prompt, x_sample, cache_kv_w128, cache_kv_w512, cache_kv_w2048, norm_pre, w_in, b_merge, v_norm_g, v_norm_b, w_spatial, b_spatial, w_proj_a, w_proj_b, w_out, norm_post):
    batch, seq, d = x_prompt.shape
    nb, dec_seq, _ = x_sample.shape
    depth = w_in.shape[0]
    assert dec_seq == 1, "the sample group decodes one position per step"
    assert seq % QKV_TILE == 0 and (batch * seq) % MLP_TILE == 0 and nb % SAMPLE_BT == 0
    caches = (cache_kv_w128, cache_kv_w512, cache_kv_w2048)

    w_in_b = w_in.astype(BF16)
    wpa_b, wpb_b, wout_b = w_proj_a.astype(BF16), w_proj_b.astype(BF16), w_out.astype(BF16)
    ws_stack = w_spatial.reshape(depth, GMLP_GROUPS * BLK, BLK)
    chan_group = jnp.arange(GMLP_W) // GMLP_GROUP_DIM
    bs_full = jnp.swapaxes(b_spatial, 1, 2)[:, :, chan_group]
    ws0 = w_spatial[:, :, 0, 0][:, chan_group][:, None, :]
    bs0 = b_spatial[:, :, 0][:, chan_group][:, None, :]
    row2 = lambda a: a[:, None, :]
    g_pre, g_post, bm, vng, vnb = row2(norm_pre), row2(norm_post), row2(b_merge), row2(v_norm_g), row2(v_norm_b)

    scale = HEAD_DIM ** -0.5
    pos_p = jnp.arange(seq, dtype=jnp.int32)
    pos_s = PAST_LEN + jnp.arange(dec_seq, dtype=jnp.int32)
    tq_p, tk_p = _rope_tables(pos_p, scale), _rope_tables(pos_p, 1.0)
    tq_s, tk_s = _rope_tables(pos_s, scale), _rope_tables(pos_s, 1.0)
    band, causal = _attn_bias_tables()
    head_of_lane = jnp.arange(GROUP_W) // HEAD_DIM
    e = (head_of_lane[:, None] == jnp.arange(LANES)[None, :]).astype(F32)
    et = e.T

    xp = x_prompt.reshape(batch * seq, d)
    xs = x_sample.reshape(nb, d)
    kv_p = [[] for _ in ATTN_GROUPS]
    kv_s = [[] for _ in ATTN_GROUPS]
    v_s = []
    for l in range(depth):
        q1, q2, q3, kv1, kv2, kv3 = _qkv_call(xp, g_pre[l], w_in_b[l, :, :3 * QKV_W], tq_p, tk_p, seq)
        attn = _attn_call(q1, q2, q3, kv1, kv2, kv3, band, causal, batch, seq)
        xp = _mlp_call(xp, attn.reshape(batch * seq, GROUP_W), g_pre[l], w_in_b[l, :, 3 * QKV_W:], bm[l], vng[l], vnb[l],
                       ws_stack[l], bs_full[l], wpa_b[l], wpb_b[l], wout_b[l], g_post[l])
        for g, ((window, _), kv) in enumerate(zip(ATTN_GROUPS, (kv1, kv2, kv3))):
            keep = min(window, seq)
            kv_p[g].append(kv.reshape(batch, seq, 2, HEADS_PER_GROUP, HEAD_DIM)[:, seq - keep:])
        qs, kn1, kn2, kn3, ga, u, vn, gb, gates = _sample_in_call(xs, g_pre[l], w_in_b[l], tq_s, tk_s, bm[l], vng[l], vnb[l])
        attn_s = _sample_attn_call(l, qs, (kn1, kn2, kn3), caches, e, et)
        xs = _sample_out_call(xs, attn_s, ga, u, vn, gb, gates, ws0[l], bs0[l], wpa_b[l], wpb_b[l], wout_b[l], g_post[l])
        for g, kn in enumerate((kn1, kn2, kn3)):
            kv_s[g].append(kn.reshape(nb, dec_seq, 2, HEADS_PER_GROUP, HEAD_DIM))
        v_s.append(vn.reshape(nb, dec_seq, GMLP_W))

    return (xp.reshape(batch, seq, d), xs.reshape(nb, dec_seq, d),
            jnp.stack(kv_p[0]), jnp.stack(kv_p[1]), jnp.stack(kv_p[2]),
            jnp.stack(kv_s[0]), jnp.stack(kv_s[1]), jnp.stack(kv_s[2]),
            jnp.stack(v_s))
```

```python
import functools

import jax
import jax.numpy as jnp
from jax import lax
from jax.experimental import pallas as pl
from jax.experimental.pallas import tpu as pltpu

F32 = jnp.float32
BF16 = jnp.bfloat16

PAST_LEN = 8192
HEAD_DIM = 64
HEADS_PER_GROUP = 4
GROUP_W = HEADS_PER_GROUP * HEAD_DIM
ATTN_GROUPS = ((128, 1), (512, 4), (2048, 16))
N_GROUPS = len(ATTN_GROUPS)
QKV_W = N_GROUPS * GROUP_W
ROT_DIM = HEAD_DIM // 4
ROT_HALF = ROT_DIM // 2
ROPE_THETA = 500000.0
BLK = 128
GMLP_GROUPS = 4
GMLP_GROUP_DIM = 192
GMLP_W = GMLP_GROUPS * GMLP_GROUP_DIM
EPS = 1e-6
LANES = 128
NEG = -1e30

OFF_GA = 0
OFF_U = OFF_GA + GROUP_W
OFF_VB = OFF_U + GMLP_W
OFF_GB = OFF_VB + GMLP_W
OFF_ML = OFF_GB + GMLP_W

QKV_TILE = 512
MLP_TILE = 256
SAMPLE_BT = 4
VMEM_LIMIT = 56 * 1024 * 1024


def _rms(x, g):
    return (x * lax.rsqrt(jnp.mean(x * x, axis=-1, keepdims=True) + EPS)) * g


def _layer_norm(x, g, b):
    xc = x - jnp.mean(x, axis=-1, keepdims=True)
    return xc * lax.rsqrt(jnp.mean(xc * xc, axis=-1, keepdims=True) + EPS) * g + b


def _rope(z, c, sa, sb):
    outs = []
    for s in range(GROUP_W // LANES):
        zs = z[:, s * LANES:(s + 1) * LANES]
        outs.append(zs * c + pltpu.roll(zs, LANES - ROT_HALF, 1) * sa + pltpu.roll(zs, ROT_HALF, 1) * sb)
    return jnp.concatenate(outs, axis=-1)


def _rope_tables(pos, scale):
    inv_freq = jnp.power(ROPE_THETA, -jnp.arange(ROT_HALF, dtype=F32) / ROT_HALF)
    ang = pos.astype(F32)[:, None] * inv_freq[None, :]
    cos, sin = jnp.cos(ang), jnp.sin(ang)
    n = pos.shape[0]
    rest1 = jnp.ones((n, HEAD_DIM - ROT_DIM), F32)
    rest0 = jnp.zeros((n, HEAD_DIM - ROT_DIM), F32)
    z8 = jnp.zeros((n, ROT_HALF), F32)
    c = jnp.concatenate([cos, cos, rest1], axis=-1)
    sa = jnp.concatenate([-sin, z8, rest0], axis=-1)
    sb = jnp.concatenate([z8, sin, rest0], axis=-1)
    reps = LANES // HEAD_DIM
    return tuple(jnp.tile(t, (1, reps)) * scale for t in (c, sa, sb))


def _activations(z, b_merge, vng, vnb):
    ga = jax.nn.silu(z[:, OFF_GA:OFF_U])
    u = jax.nn.gelu(z[:, OFF_U:OFF_VB])
    vn = _layer_norm(jax.nn.gelu(z[:, OFF_VB:OFF_GB]), vng, vnb)
    gb = jax.nn.silu(z[:, OFF_GB:OFF_ML])
    gates = jax.nn.sigmoid(z[:, OFF_ML:] + b_merge)
    return ga, u, vn, gb, gates


def _merge_out(x, a_in, b_in, gates, wpa, wpb, wout, gpost):
    d = x.shape[-1]
    ba = jnp.dot(a_in.astype(BF16), wpa, preferred_element_type=F32)
    bb = jnp.dot(b_in.astype(BF16), wpb, preferred_element_type=F32)
    merged = gates[:, :d] * ba + gates[:, d:] * bb
    y = jnp.dot(merged.astype(BF16), wout, preferred_element_type=F32)
    return x + _rms(y, gpost)


def _const_spec(shape):
    nd = len(shape)
    return pl.BlockSpec(shape, lambda *_: (0,) * nd)


def _qkv_kernel(seq, x_ref, g_ref, w_ref, cq_ref, saq_ref, sbq_ref, ck_ref, sak_ref, sbk_ref,
                q1_ref, kv1_ref, q2_ref, kv2_ref, q3_ref, kv3_ref, p1_ref, p2_ref, p3_ref, dei_s):
    last_tile = pl.program_id(1) == pl.num_programs(1) - 1
    t = x_ref.shape[0]
    hb = _rms(x_ref[...], g_ref[...]).astype(BF16)
    q_refs = (q1_ref, q2_ref, q3_ref)
    kv_refs = (kv1_ref, kv2_ref, kv3_ref)
    p_refs = (p1_ref, p2_ref, p3_ref)
    cq, saq, sbq = cq_ref[...], saq_ref[...], sbq_ref[...]
    ck, sak, sbk = ck_ref[...], sak_ref[...], sbk_ref[...]
    nslab = GROUP_W // LANES

    def split_by_residue(z, dst_ref, lane0, dil):
        for s in range(nslab):
            dei_s[s] = z[:, s * LANES:(s + 1) * LANES]
        for r in range(dil):
            rows = [dei_s[s, pl.ds(r, t // dil, stride=dil), :] for s in range(nslab)]
            dst_ref[r, :, lane0:lane0 + GROUP_W] = jnp.concatenate(rows, axis=-1).astype(BF16)

    def store_transposed(p_ref, idx, z):
        p_ref[idx] = z.T.reshape(HEADS_PER_GROUP, HEAD_DIM, z.shape[0])

    for g, (window, dil) in enumerate(ATTN_GROUPS):
        zq = jnp.dot(hb, w_ref[:, g * GROUP_W:(g + 1) * GROUP_W], preferred_element_type=F32)
        zq = _rope(zq, cq, saq, sbq)
        zk = jnp.dot(hb, w_ref[:, QKV_W + g * GROUP_W:QKV_W + (g + 1) * GROUP_W], preferred_element_type=F32)
        zk = _rope(zk, ck, sak, sbk)
        zv = jnp.dot(hb, w_ref[:, 2 * QKV_W + g * GROUP_W:2 * QKV_W + (g + 1) * GROUP_W],
                     preferred_element_type=F32)
        if dil == 1:
            q_refs[g][...] = zq.astype(BF16)
            kv_refs[g][:, 0:GROUP_W] = zk.astype(BF16)
            kv_refs[g][:, GROUP_W:2 * GROUP_W] = zv.astype(BF16)
        else:
            split_by_residue(zq, q_refs[g], 0, dil)
            split_by_residue(zk, kv_refs[g], 0, dil)
            split_by_residue(zv, kv_refs[g], GROUP_W, dil)
        keep = p_refs[g].shape[-1]
        if window >= seq:
            store_transposed(p_refs[g], 0, zk)
            store_transposed(p_refs[g], 1, zv)
        else:
            @pl.when(last_tile)
            def _(g=g, zk=zk, zv=zv, keep=keep):
                store_transposed(p_refs[g], 0, zk[t - keep:])
                store_transposed(p_refs[g], 1, zv[t - keep:])


def _qkv_call(x, g_pre, w_qkv, tabs_q, tabs_k):
    batch, seq, d = x.shape
    t = QKV_TILE
    nt = seq // t
    tab_spec = pl.BlockSpec((t, LANES), lambda b, c: (c, 0))
    in_specs = [pl.BlockSpec((None, t, d), lambda b, c: (b, c, 0)), _const_spec((1, d)), _const_spec(w_qkv.shape)]
    out_specs, out_shape = [], []
    for window, dil in ATTN_GROUPS:
        assert t % (dil * 16) == 0
        if dil == 1:
            q_blk, kv_blk = (None, t, GROUP_W), (None, t, 2 * GROUP_W)
            q_shape, kv_shape = (batch, seq, GROUP_W), (batch, seq, 2 * GROUP_W)
            idx = lambda b, c: (b, c, 0)
        else:
            q_blk, kv_blk = (None, dil, t // dil, GROUP_W), (None, dil, t // dil, 2 * GROUP_W)
            q_shape, kv_shape = (batch, dil, seq // dil, GROUP_W), (batch, dil, seq // dil, 2 * GROUP_W)
            idx = lambda b, c: (b, 0, c, 0)
        out_specs += [pl.BlockSpec(q_blk, idx), pl.BlockSpec(kv_blk, idx)]
        out_shape += [jax.ShapeDtypeStruct(q_shape, BF16), jax.ShapeDtypeStruct(kv_shape, BF16)]
    for window, dil in ATTN_GROUPS:
        keep = min(window, seq)
        assert keep == seq or keep <= t, "a partial window must fit in the last tile"
        blk = min(keep, t)
        p_idx = (lambda b, c: (b, 0, 0, 0, c)) if keep == seq else (lambda b, c: (b, 0, 0, 0, 0))
        out_specs.append(pl.BlockSpec((None, 2, HEADS_PER_GROUP, HEAD_DIM, blk), p_idx))
        out_shape.append(jax.ShapeDtypeStruct((batch, 2, HEADS_PER_GROUP, HEAD_DIM, keep), F32))
    return pl.pallas_call(
        functools.partial(_qkv_kernel, seq),
        grid=(batch, nt),
        in_specs=in_specs + [tab_spec] * 6,
        out_specs=out_specs,
        out_shape=out_shape,
        scratch_shapes=[pltpu.VMEM((GROUP_W // LANES, t, LANES), F32)],
        compiler_params=pltpu.CompilerParams(dimension_semantics=("arbitrary", "arbitrary"),
                                             vmem_limit_bytes=VMEM_LIMIT),
        name="prompt_qkv",
    )(x, g_pre, w_qkv, *tabs_q, *tabs_k)


def _attn_block(q, k, v, bias):
    nk = k.shape[0]
    hid = lax.shift_right_logical(lax.broadcasted_iota(jnp.int32, (BLK, GROUP_W), 1), HEAD_DIM.bit_length() - 1)
    qf = q.astype(F32)
    qs = jnp.concatenate([jnp.where(hid == h, qf, 0.0) for h in range(HEADS_PER_GROUP)], axis=0).astype(BF16)
    s = lax.dot_general(qs, k, (((1,), (1,)), ((), ())), preferred_element_type=F32)
    s = s.reshape(HEADS_PER_GROUP, BLK, nk) + bias[None]
    m = jnp.max(s, axis=-1, keepdims=True)
    p = jnp.exp(s - m)
    l = jnp.sum(p, axis=-1, keepdims=True)
    pv = jnp.dot(p.reshape(HEADS_PER_GROUP * BLK, nk).astype(BF16), v, preferred_element_type=F32)
    pv = pv.reshape(HEADS_PER_GROUP, BLK, GROUP_W)
    inv = 1.0 / l
    lse = m + jnp.log(l)
    o = jnp.zeros((BLK, GROUP_W), F32)
    ls = jnp.zeros((BLK, GROUP_W), F32)
    for h in range(HEADS_PER_GROUP):
        o = jnp.where(hid == h, pv[h] * inv[h], o)
        ls = jnp.where(hid == h, lse[h], ls)
    return o, ls


def _attn_kernel(q1_ref, q2_ref, q3_ref, kv1_ref, kv2_ref, kv3_ref, band_ref, causal_ref, out_ref,
                 o1_s, l1_s, o2_s, l2_s, o3_s, l3_s):
    j = pl.program_id(1)
    nslab = GROUP_W // LANES
    d2 = ATTN_GROUPS[1][1]
    d3 = ATTN_GROUPS[2][1]
    blocks_per_res2 = pl.num_programs(1) // d2

    def store(o_s, l_s, rows, o, ls):
        for s in range(nslab):
            o_s[s, rows, :] = o[:, s * LANES:(s + 1) * LANES]
            l_s[s, rows, :] = ls[:, s * LANES:(s + 1) * LANES]

    start = pl.multiple_of(jnp.maximum(j - 1, 0) * BLK, BLK)
    o, ls = _attn_block(q1_ref[...], kv1_ref[pl.ds(start, 2 * BLK), 0:GROUP_W],
                        kv1_ref[pl.ds(start, 2 * BLK), GROUP_W:2 * GROUP_W], band_ref[jnp.minimum(j, 1)])
    store(o1_s, l1_s, pl.ds(pl.multiple_of(j * BLK, BLK), BLK), o, ls)

    n = j % blocks_per_res2
    start = pl.multiple_of(jnp.maximum(n - 1, 0) * BLK, BLK)
    o, ls = _attn_block(q2_ref[...], kv2_ref[pl.ds(start, 2 * BLK), 0:GROUP_W],
                        kv2_ref[pl.ds(start, 2 * BLK), GROUP_W:2 * GROUP_W], band_ref[jnp.minimum(n, 1)])
    store(o2_s, l2_s, pl.ds(n * (BLK * d2) + j // blocks_per_res2, BLK, stride=d2), o, ls)

    o, ls = _attn_block(q3_ref[...], kv3_ref[:, 0:GROUP_W], kv3_ref[:, GROUP_W:2 * GROUP_W], causal_ref[...])
    store(o3_s, l3_s, pl.ds(j, BLK, stride=d3), o, ls)

    @pl.when(j == pl.num_programs(1) - 1)
    def _():
        seq = out_ref.shape[0]
        rows_per_step = 256
        for s in range(nslab):
            def body(c, carry):
                rows = pl.ds(pl.multiple_of(c * rows_per_step, rows_per_step), rows_per_step)
                l1, l2, l3 = l1_s[s, rows, :], l2_s[s, rows, :], l3_s[s, rows, :]
                m = jnp.maximum(jnp.maximum(l1, l2), l3)
                e1, e2, e3 = jnp.exp(l1 - m), jnp.exp(l2 - m), jnp.exp(l3 - m)
                inv = 1.0 / (e1 + e2 + e3)
                out = (e1 * inv) * o1_s[s, rows, :] + (e2 * inv) * o2_s[s, rows, :] + (e3 * inv) * o3_s[s, rows, :]
                out_ref[rows, s * LANES:(s + 1) * LANES] = out
                return carry
            lax.fori_loop(0, seq // rows_per_step, body, 0)


def _attn_call(q1, q2, q3, kv1, kv2, kv3, band, causal, batch, seq):
    d2 = ATTN_GROUPS[1][1]
    d3 = ATTN_GROUPS[2][1]
    nblk = seq // BLK
    assert seq // d3 == BLK and nblk % d2 == 0 and nblk == d3
    bpr2 = nblk // d2
    scratch = [pltpu.VMEM((GROUP_W // LANES, seq, LANES), F32)] * 6
    return pl.pallas_call(
        _attn_kernel,
        grid=(batch, nblk),
        in_specs=[
            pl.BlockSpec((None, BLK, GROUP_W), lambda b, j: (b, j, 0)),
            pl.BlockSpec((None, None, BLK, GROUP_W), lambda b, j: (b, j // bpr2, j % bpr2, 0)),
            pl.BlockSpec((None, None, BLK, GROUP_W), lambda b, j: (b, j, 0, 0)),
            pl.BlockSpec((None, seq, 2 * GROUP_W), lambda b, j: (b, 0, 0)),
            pl.BlockSpec((None, None, seq // d2, 2 * GROUP_W), lambda b, j: (b, j // bpr2, 0, 0)),
            pl.BlockSpec((None, None, BLK, 2 * GROUP_W), lambda b, j: (b, j, 0, 0)),
            _const_spec(band.shape),
            _const_spec(causal.shape),
        ],
        out_specs=pl.BlockSpec((None, seq, GROUP_W), lambda b, j: (b, 0, 0)),
        out_shape=jax.ShapeDtypeStruct((batch, seq, GROUP_W), F32),
        scratch_shapes=scratch,
        compiler_params=pltpu.CompilerParams(dimension_semantics=("arbitrary", "arbitrary"),
                                             vmem_limit_bytes=VMEM_LIMIT),
        name="prompt_attn",
    )(q1, q2, q3, kv1, kv2, kv3, band, causal)


def _attn_bias_tables():
    qi = jnp.arange(BLK)[:, None]
    kj = jnp.arange(2 * BLK)[None, :]
    general = (kj >= qi) & (kj <= qi + BLK)
    first = kj <= qi
    band = jnp.where(jnp.stack([first, general]), 0.0, NEG).astype(F32)
    causal = jnp.where(jnp.arange(BLK)[None, :] <= qi, 0.0, NEG).astype(F32)
    return band, causal


def _mlp_kernel(x_ref, attn_ref, gpre_ref, w_ref, bm_ref, vng_ref, vnb_ref, ws_ref, bsp_ref,
                wpa_ref, wpb_ref, wout_ref, gpost_ref, out_ref):
    x = x_ref[...]
    rows = x.shape[0]
    hb = _rms(x, gpre_ref[...]).astype(BF16)
    z = jnp.dot(hb, w_ref[...], preferred_element_type=F32)
    ga, u, vn, gb, gates = _activations(z, bm_ref[...], vng_ref[...], vnb_ref[...])
    wr = lax.broadcasted_iota(jnp.int32, ws_ref.shape, 0) & (BLK - 1)
    wc = lax.broadcasted_iota(jnp.int32, ws_ref.shape, 1)
    ws = jnp.where(wc <= wr, ws_ref[...], 0.0).astype(BF16)
    lane = lax.broadcasted_iota(jnp.int32, (BLK, GMLP_W), 1)
    vnb16 = vn.astype(BF16)
    mixes = []
    for c in range(rows // BLK):
        r = jnp.dot(ws, vnb16[c * BLK:(c + 1) * BLK], preferred_element_type=F32)
        mix = r[(GMLP_GROUPS - 1) * BLK:]
        for g in range(GMLP_GROUPS - 2, -1, -1):
            mix = jnp.where(lane < (g + 1) * GMLP_GROUP_DIM, r[g * BLK:(g + 1) * BLK], mix)
        mixes.append(mix + bsp_ref[...])
    mix = jnp.concatenate(mixes, axis=0)
    out_ref[...] = _merge_out(x, attn_ref[...] * ga, (u * mix) * gb, gates,
                              wpa_ref[...], wpb_ref[...], wout_ref[...], gpost_ref[...])


def _mlp_call(x, attn, g_pre, w_rest, b_merge, vng, vnb, ws_stack, bs_full, wpa, wpb, wout, g_post):
    n, d = x.shape
    t = MLP_TILE
    row = lambda i: (i, 0)
    consts = (g_pre, w_rest, b_merge, vng, vnb, ws_stack, bs_full, wpa, wpb, wout, g_post)
    return pl.pallas_call(
        _mlp_kernel,
        grid=(n // t,),
        in_specs=[pl.BlockSpec((t, d), row), pl.BlockSpec((t, GROUP_W), row)] + [_const_spec(c.shape) for c in consts],
        out_specs=pl.BlockSpec((t, d), row),
        out_shape=jax.ShapeDtypeStruct((n, d), F32),
        compiler_params=pltpu.CompilerParams(dimension_semantics=("arbitrary",), vmem_limit_bytes=VMEM_LIMIT),
        name="prompt_mlp",
    )(x, attn, *consts)


def _sample_in_kernel(x_ref, g_ref, w_ref, cq_ref, saq_ref, sbq_ref, ck_ref, sak_ref, sbk_ref,
                      bm_ref, vng_ref, vnb_ref,
                      q_ref, kv1_ref, kv2_ref, kv3_ref, ga_ref, u_ref, vn_ref, gb_ref, gates_ref):
    hb = _rms(x_ref[...], g_ref[...]).astype(BF16)
    z = jnp.dot(hb, w_ref[...], preferred_element_type=F32)
    kv_refs = (kv1_ref, kv2_ref, kv3_ref)
    for g in range(N_GROUPS):
        lo = g * GROUP_W
        q_ref[:, lo:lo + GROUP_W] = _rope(z[:, lo:lo + GROUP_W], cq_ref[...], saq_ref[...], sbq_ref[...])
        kv_refs[g][:, 0:GROUP_W] = _rope(z[:, QKV_W + lo:QKV_W + lo + GROUP_W], ck_ref[...], sak_ref[...], sbk_ref[...])
        kv_refs[g][:, GROUP_W:2 * GROUP_W] = z[:, 2 * QKV_W + lo:2 * QKV_W + lo + GROUP_W]
    ga, u, vn, gb, gates = _activations(z[:, 3 * QKV_W:], bm_ref[...], vng_ref[...], vnb_ref[...])
    ga_ref[...] = ga
    u_ref[...] = u
    vn_ref[...] = vn
    gb_ref[...] = gb
    gates_ref[...] = gates


def _sample_in_call(x, g_pre, w_in, tabs_q, tabs_k, b_merge, vng, vnb):
    n, d = x.shape
    sds = lambda w: jax.ShapeDtypeStruct((n, w), F32)
    return pl.pallas_call(
        _sample_in_kernel,
        out_shape=[sds(QKV_W)] + [sds(2 * GROUP_W)] * 3 + [sds(GROUP_W), sds(GMLP_W), sds(GMLP_W), sds(GMLP_W), sds(2 * d)],
        compiler_params=pltpu.CompilerParams(vmem_limit_bytes=VMEM_LIMIT),
        name="sample_in",
    )(x, g_pre, w_in, *tabs_q, *tabs_k, b_merge, vng, vnb)


def _sample_attn_kernel(q_ref, kvn1_ref, kvn2_ref, kvn3_ref, c1_ref, c2_ref, c3_ref, out_ref):
    for b in range(q_ref.shape[0]):
        outs, lses = [], []
        for g, (kvn_ref, c_ref) in enumerate(((kvn1_ref, c1_ref), (kvn2_ref, c2_ref), (kvn3_ref, c3_ref))):
            q = q_ref[b, g]
            kc, vc = c_ref[b, :, 0], c_ref[b, :, 1]
            kn, vn = kvn_ref[b, 0], kvn_ref[b, 1]
            sc = jnp.sum(kc * q[None], axis=-1, keepdims=True)
            ss = jnp.sum(kn * q, axis=-1, keepdims=True)
            m = jnp.maximum(jnp.max(sc, axis=0), ss)
            pc = jnp.exp(sc - m[None])
            ps = jnp.exp(ss - m)
            l = jnp.sum(pc, axis=0) + ps
            o = jnp.sum(pc * vc, axis=0) + ps * vn
            outs.append(o / l)
            lses.append(m + jnp.log(l))
        m = jnp.maximum(jnp.maximum(lses[0], lses[1]), lses[2])
        es = [jnp.exp(l - m) for l in lses]
        inv = 1.0 / (es[0] + es[1] + es[2])
        out_ref[b] = (es[0] * inv) * outs[0] + (es[1] * inv) * outs[1] + (es[2] * inv) * outs[2]


def _sample_attn_call(layer, q, kvn, caches):
    n = q.shape[0]
    bt = SAMPLE_BT
    lead = lambda i: (i, 0, 0, 0)
    cache_specs = []
    cache_views = []
    for (window, dil), c in zip(ATTN_GROUPS, caches):
        depth, nb, rows = c.shape[:3]
        assert rows == window, "cache must hold exactly the window"
        cache_views.append(c.reshape(depth, nb, rows // dil, dil, 2, HEADS_PER_GROUP, HEAD_DIM))
        cache_specs.append(pl.BlockSpec((None, bt, rows // dil, None, 2, HEADS_PER_GROUP, HEAD_DIM),
                                        lambda i: (layer, i, 0, 0, 0, 0, 0)))
    return pl.pallas_call(
        _sample_attn_kernel,
        grid=(n // bt,),
        in_specs=[pl.BlockSpec((bt, N_GROUPS, HEADS_PER_GROUP, HEAD_DIM), lead)]
        + [pl.BlockSpec((bt, 2, HEADS_PER_GROUP, HEAD_DIM), lead)] * 3 + cache_specs,
        out_specs=pl.BlockSpec((bt, HEADS_PER_GROUP, HEAD_DIM), lambda i: (i, 0, 0)),
        out_shape=jax.ShapeDtypeStruct((n, HEADS_PER_GROUP, HEAD_DIM), F32),
        compiler_params=pltpu.CompilerParams(dimension_semantics=("arbitrary",), vmem_limit_bytes=VMEM_LIMIT),
        name="sample_attn",
    )(q, *kvn, *cache_views)


def _sample_out_kernel(x_ref, attn_ref, ga_ref, u_ref, vn_ref, gb_ref, gates_ref, ws0_ref, bs0_ref,
                       wpa_ref, wpb_ref, wout_ref, gpost_ref, out_ref):
    mix = vn_ref[...] * ws0_ref[...] + bs0_ref[...]
    out_ref[...] = _merge_out(x_ref[...], attn_ref[...] * ga_ref[...], (u_ref[...] * mix) * gb_ref[...],
                              gates_ref[...], wpa_ref[...], wpb_ref[...], wout_ref[...], gpost_ref[...])


def _sample_out_call(x, attn, ga, u, vn, gb, gates, ws0, bs0, wpa, wpb, wout, g_post):
    return pl.pallas_call(
        _sample_out_kernel,
        out_shape=jax.ShapeDtypeStruct(x.shape, F32),
        compiler_params=pltpu.CompilerParams(vmem_limit_bytes=VMEM_LIMIT),
        name="sample_out",
    )(x, attn, ga, u, vn, gb, gates, ws0, bs0, wpa, wpb, wout, g_post)


def kernel(x_prompt, x_sample, cache_kv_w128, cache_kv_w512, cache_kv_w2048, norm_pre, w_in, b_merge, v_norm_g, v_norm_b, w_spatial, b_spatial, w_proj_a, w_proj_b, w_out, norm_post):
    batch, seq, d = x_prompt.shape
    nb, dec_seq, _ = x_sample.shape
    depth = w_in.shape[0]
    assert dec_seq == 1, "the sample group decodes one position per step"
    assert seq % QKV_TILE == 0 and (batch * seq) % MLP_TILE == 0 and nb % SAMPLE_BT == 0
    caches = (cache_kv_w128, cache_kv_w512, cache_kv_w2048)

    w_in_b = w_in.astype(BF16)
    wpa_b, wpb_b, wout_b = w_proj_a.astype(BF16), w_proj_b.astype(BF16), w_out.astype(BF16)
    ws_stack = w_spatial.reshape(depth, GMLP_GROUPS * BLK, BLK)
    chan_group = jnp.arange(GMLP_W) // GMLP_GROUP_DIM
    bs_full = jnp.swapaxes(b_spatial, 1, 2)[:, :, chan_group]
    ws0 = w_spatial[:, :, 0, 0][:, chan_group][:, None, :]
    bs0 = b_spatial[:, :, 0][:, chan_group][:, None, :]
    row2 = lambda a: a[:, None, :]
    g_pre, g_post, bm, vng, vnb = row2(norm_pre), row2(norm_post), row2(b_merge), row2(v_norm_g), row2(v_norm_b)

    scale = HEAD_DIM ** -0.5
    pos_p = jnp.arange(seq, dtype=jnp.int32)
    pos_s = PAST_LEN + jnp.arange(dec_seq, dtype=jnp.int32)
    tq_p, tk_p = _rope_tables(pos_p, scale), _rope_tables(pos_p, 1.0)
    tq_s, tk_s = _rope_tables(pos_s, scale), _rope_tables(pos_s, 1.0)
    band, causal = _attn_bias_tables()

    xp = x_prompt
    xs = x_sample.reshape(nb, d)
    kv_p = [[] for _ in ATTN_GROUPS]
    kv_s = [[] for _ in ATTN_GROUPS]
    v_s = []
    for l in range(depth):
        q1, kv1, q2, kv2, q3, kv3, p1, p2, p3 = _qkv_call(xp, g_pre[l], w_in_b[l, :, :3 * QKV_W], tq_p, tk_p)
        attn = _attn_call(q1, q2, q3, kv1, kv2, kv3, band, causal, batch, seq)
        xp = _mlp_call(xp.reshape(batch * seq, d), attn.reshape(batch * seq, GROUP_W), g_pre[l], w_in_b[l, :, 3 * QKV_W:],
                       bm[l], vng[l], vnb[l], ws_stack[l], bs_full[l], wpa_b[l], wpb_b[l], wout_b[l], g_post[l])
        xp = xp.reshape(batch, seq, d)
        for g, p in enumerate((p1, p2, p3)):
            kv_p[g].append(p)
        qs, kn1, kn2, kn3, ga, u, vn, gb, gates = _sample_in_call(xs, g_pre[l], w_in_b[l], tq_s, tk_s, bm[l], vng[l], vnb[l])
        kns = [kn.reshape(nb, 2, HEADS_PER_GROUP, HEAD_DIM) for kn in (kn1, kn2, kn3)]
        attn_s = _sample_attn_call(l, qs.reshape(nb, N_GROUPS, HEADS_PER_GROUP, HEAD_DIM), kns, caches)
        xs = _sample_out_call(xs, attn_s.reshape(nb, GROUP_W), ga, u, vn, gb, gates, ws0[l], bs0[l],
                              wpa_b[l], wpb_b[l], wout_b[l], g_post[l])
        for g, kn in enumerate(kns):
            kv_s[g].append(kn.reshape(nb, dec_seq, 2, HEADS_PER_GROUP, HEAD_DIM))
        v_s.append(vn.reshape(nb, dec_seq, GMLP_W))

    new_kv_p = [jnp.transpose(jnp.stack(ps), (0, 1, 5, 2, 3, 4)) for ps in kv_p]
    return (xp, xs.reshape(nb, dec_seq, d), new_kv_p[0], new_kv_p[1], new_kv_p[2],
            jnp.stack(kv_s[0]), jnp.stack(kv_s[1]), jnp.stack(kv_s[2]), jnp.stack(v_s))
```

```python
import functools

import jax
import jax.numpy as jnp
from jax import lax
from jax.experimental import pallas as pl
from jax.experimental.pallas import tpu as pltpu

F32 = jnp.float32
BF16 = jnp.bfloat16

PAST_LEN = 8192
HEAD_DIM = 64
HEADS_PER_GROUP = 4
GROUP_W = HEADS_PER_GROUP * HEAD_DIM
ATTN_GROUPS = ((128, 1), (512, 4), (2048, 16))
N_GROUPS = len(ATTN_GROUPS)
QKV_W = N_GROUPS * GROUP_W
ROT_DIM = HEAD_DIM // 4
ROT_HALF = ROT_DIM // 2
ROPE_THETA = 500000.0
BLK = 128
GMLP_GROUPS = 4
GMLP_GROUP_DIM = 192
GMLP_W = GMLP_GROUPS * GMLP_GROUP_DIM
EPS = 1e-6
LANES = 128
NEG = -1e30

OFF_GA = 0
OFF_U = OFF_GA + GROUP_W
OFF_VB = OFF_U + GMLP_W
OFF_GB = OFF_VB + GMLP_W
OFF_ML = OFF_GB + GMLP_W

QKV_TILE = 512
MLP_TILE = 256
SAMPLE_BT = 2
VMEM_LIMIT = 56 * 1024 * 1024


def _rms(x, g):
    return (x * lax.rsqrt(jnp.mean(x * x, axis=-1, keepdims=True) + EPS)) * g


def _layer_norm(x, g, b):
    xc = x - jnp.mean(x, axis=-1, keepdims=True)
    return xc * lax.rsqrt(jnp.mean(xc * xc, axis=-1, keepdims=True) + EPS) * g + b


def _rope(z, c, sa, sb):
    outs = []
    for s in range(GROUP_W // LANES):
        zs = z[:, s * LANES:(s + 1) * LANES]
        outs.append(zs * c + pltpu.roll(zs, LANES - ROT_HALF, 1) * sa + pltpu.roll(zs, ROT_HALF, 1) * sb)
    return jnp.concatenate(outs, axis=-1)


def _rope_tables(pos, scale):
    inv_freq = jnp.power(ROPE_THETA, -jnp.arange(ROT_HALF, dtype=F32) / ROT_HALF)
    ang = pos.astype(F32)[:, None] * inv_freq[None, :]
    cos, sin = jnp.cos(ang), jnp.sin(ang)
    n = pos.shape[0]
    rest1 = jnp.ones((n, HEAD_DIM - ROT_DIM), F32)
    rest0 = jnp.zeros((n, HEAD_DIM - ROT_DIM), F32)
    z8 = jnp.zeros((n, ROT_HALF), F32)
    c = jnp.concatenate([cos, cos, rest1], axis=-1)
    sa = jnp.concatenate([-sin, z8, rest0], axis=-1)
    sb = jnp.concatenate([z8, sin, rest0], axis=-1)
    reps = LANES // HEAD_DIM
    return tuple(jnp.tile(t, (1, reps)) * scale for t in (c, sa, sb))


def _activations(z, b_merge, vng, vnb):
    ga = jax.nn.silu(z[:, OFF_GA:OFF_U])
    u = jax.nn.gelu(z[:, OFF_U:OFF_VB])
    vn = _layer_norm(jax.nn.gelu(z[:, OFF_VB:OFF_GB]), vng, vnb)
    gb = jax.nn.silu(z[:, OFF_GB:OFF_ML])
    gates = jax.nn.sigmoid(z[:, OFF_ML:] + b_merge)
    return ga, u, vn, gb, gates


def _merge_out(x, a_in, b_in, gates, wpa, wpb, wout, gpost):
    d = x.shape[-1]
    ba = jnp.dot(a_in.astype(BF16), wpa, preferred_element_type=F32)
    bb = jnp.dot(b_in.astype(BF16), wpb, preferred_element_type=F32)
    merged = gates[:, :d] * ba + gates[:, d:] * bb
    y = jnp.dot(merged.astype(BF16), wout, preferred_element_type=F32)
    return x + _rms(y, gpost)


def _const_spec(shape):
    nd = len(shape)
    return pl.BlockSpec(shape, lambda *_: (0,) * nd)


def _layer_spec(arr, layer, block_cols=None, col_block=0):
    rows, cols = arr.shape[1:]
    return pl.BlockSpec((None, rows, cols if block_cols is None else block_cols), lambda *_: (layer, 0, col_block))


def _qkv_kernel(seq, x_ref, g_ref, w_ref, cq_ref, saq_ref, sbq_ref, ck_ref, sak_ref, sbk_ref,
                q1_ref, kv1_ref, q2_ref, kv2_ref, q3_ref, kv3_ref, p1_ref, p2_ref, p3_ref, dei_s):
    last_tile = pl.program_id(1) == pl.num_programs(1) - 1
    t = x_ref.shape[0]
    hb = _rms(x_ref[...], g_ref[...]).astype(BF16)
    q_refs = (q1_ref, q2_ref, q3_ref)
    kv_refs = (kv1_ref, kv2_ref, kv3_ref)
    p_refs = (p1_ref, p2_ref, p3_ref)
    cq, saq, sbq = cq_ref[...], saq_ref[...], sbq_ref[...]
    ck, sak, sbk = ck_ref[...], sak_ref[...], sbk_ref[...]
    nslab = GROUP_W // LANES

    def split_by_residue(z, dst_ref, lane0, dil):
        for s in range(nslab):
            dei_s[s] = z[:, s * LANES:(s + 1) * LANES]
        for r in range(dil):
            rows = [dei_s[s, pl.ds(r, t // dil, stride=dil), :] for s in range(nslab)]
            dst_ref[r, :, lane0:lane0 + GROUP_W] = jnp.concatenate(rows, axis=-1).astype(BF16)

    def store_transposed(p_ref, idx, z):
        p_ref[idx] = z.T.reshape(HEADS_PER_GROUP, HEAD_DIM, z.shape[0])

    for g, (window, dil) in enumerate(ATTN_GROUPS):
        zq = jnp.dot(hb, w_ref[:, g * GROUP_W:(g + 1) * GROUP_W], preferred_element_type=F32)
        zq = _rope(zq, cq, saq, sbq)
        zk = jnp.dot(hb, w_ref[:, QKV_W + g * GROUP_W:QKV_W + (g + 1) * GROUP_W], preferred_element_type=F32)
        zk = _rope(zk, ck, sak, sbk)
        zv = jnp.dot(hb, w_ref[:, 2 * QKV_W + g * GROUP_W:2 * QKV_W + (g + 1) * GROUP_W],
                     preferred_element_type=F32)
        if dil == 1:
            q_refs[g][...] = zq.astype(BF16)
            kv_refs[g][:, 0:GROUP_W] = zk.astype(BF16)
            kv_refs[g][:, GROUP_W:2 * GROUP_W] = zv.astype(BF16)
        else:
            split_by_residue(zq, q_refs[g], 0, dil)
            split_by_residue(zk, kv_refs[g], 0, dil)
            split_by_residue(zv, kv_refs[g], GROUP_W, dil)
        keep = p_refs[g].shape[-1]
        if window >= seq:
            store_transposed(p_refs[g], 0, zk)
            store_transposed(p_refs[g], 1, zv)
        else:
            @pl.when(last_tile)
            def _(g=g, zk=zk, zv=zv, keep=keep):
                store_transposed(p_refs[g], 0, zk[t - keep:])
                store_transposed(p_refs[g], 1, zv[t - keep:])


def _qkv_call(layer, x, g_pre, w_in, tabs_q, tabs_k):
    batch, seq, d = x.shape
    t = QKV_TILE
    nt = seq // t
    tab_spec = pl.BlockSpec((t, LANES), lambda b, c: (c, 0))
    in_specs = [pl.BlockSpec((None, t, d), lambda b, c: (b, c, 0)), _layer_spec(g_pre, layer),
                _layer_spec(w_in, layer, 3 * QKV_W, 0)]
    out_specs, out_shape = [], []
    for window, dil in ATTN_GROUPS:
        assert t % (dil * 16) == 0
        if dil == 1:
            q_blk, kv_blk = (None, t, GROUP_W), (None, t, 2 * GROUP_W)
            q_shape, kv_shape = (batch, seq, GROUP_W), (batch, seq, 2 * GROUP_W)
            idx = lambda b, c: (b, c, 0)
        else:
            q_blk, kv_blk = (None, dil, t // dil, GROUP_W), (None, dil, t // dil, 2 * GROUP_W)
            q_shape, kv_shape = (batch, dil, seq // dil, GROUP_W), (batch, dil, seq // dil, 2 * GROUP_W)
            idx = lambda b, c: (b, 0, c, 0)
        out_specs += [pl.BlockSpec(q_blk, idx), pl.BlockSpec(kv_blk, idx)]
        out_shape += [jax.ShapeDtypeStruct(q_shape, BF16), jax.ShapeDtypeStruct(kv_shape, BF16)]
    for window, dil in ATTN_GROUPS:
        keep = min(window, seq)
        assert keep == seq or keep <= t, "a partial window must fit in the last tile"
        blk = min(keep, t)
        p_idx = (lambda b, c: (b, 0, 0, 0, c)) if keep == seq else (lambda b, c: (b, 0, 0, 0, 0))
        out_specs.append(pl.BlockSpec((None, 2, HEADS_PER_GROUP, HEAD_DIM, blk), p_idx))
        out_shape.append(jax.ShapeDtypeStruct((batch, 2, HEADS_PER_GROUP, HEAD_DIM, keep), F32))
    return pl.pallas_call(
        functools.partial(_qkv_kernel, seq),
        grid=(batch, nt),
        in_specs=in_specs + [tab_spec] * 6,
        out_specs=out_specs,
        out_shape=out_shape,
        scratch_shapes=[pltpu.VMEM((GROUP_W // LANES, t, LANES), F32)],
        compiler_params=pltpu.CompilerParams(dimension_semantics=("arbitrary", "arbitrary"),
                                             vmem_limit_bytes=VMEM_LIMIT),
        name="prompt_qkv",
    )(x, g_pre, w_in, *tabs_q, *tabs_k)


def _attn_block(q, k, v, bias):
    nk = k.shape[0]
    hid = lax.shift_right_logical(lax.broadcasted_iota(jnp.int32, (BLK, GROUP_W), 1), HEAD_DIM.bit_length() - 1)
    qf = q.astype(F32)
    qs = jnp.concatenate([jnp.where(hid == h, qf, 0.0) for h in range(HEADS_PER_GROUP)], axis=0).astype(BF16)
    s = lax.dot_general(qs, k, (((1,), (1,)), ((), ())), preferred_element_type=F32)
    s = s.reshape(HEADS_PER_GROUP, BLK, nk) + bias[None]
    m = jnp.max(s, axis=-1, keepdims=True)
    p = jnp.exp(s - m)
    l = jnp.sum(p, axis=-1, keepdims=True)
    pv = jnp.dot(p.reshape(HEADS_PER_GROUP * BLK, nk).astype(BF16), v, preferred_element_type=F32)
    pv = pv.reshape(HEADS_PER_GROUP, BLK, GROUP_W)
    inv = 1.0 / l
    lse = m + jnp.log(l)
    o = jnp.zeros((BLK, GROUP_W), F32)
    ls = jnp.zeros((BLK, GROUP_W), F32)
    for h in range(HEADS_PER_GROUP):
        o = jnp.where(hid == h, pv[h] * inv[h], o)
        ls = jnp.where(hid == h, lse[h], ls)
    return o, ls


def _attn_kernel(q1_ref, q2_ref, q3_ref, kv1_ref, kv2_ref, kv3_ref, band_ref, causal_ref, out_ref,
                 o1_s, l1_s, o2_s, l2_s, o3_s, l3_s):
    j = pl.program_id(1)
    nslab = GROUP_W // LANES
    d2 = ATTN_GROUPS[1][1]
    d3 = ATTN_GROUPS[2][1]
    blocks_per_res2 = pl.num_programs(1) // d2

    def store(o_s, l_s, rows, o, ls):
        for s in range(nslab):
            o_s[s, rows, :] = o[:, s * LANES:(s + 1) * LANES]
            l_s[s, rows, :] = ls[:, s * LANES:(s + 1) * LANES]

    start = pl.multiple_of(jnp.maximum(j - 1, 0) * BLK, BLK)
    o, ls = _attn_block(q1_ref[...], kv1_ref[pl.ds(start, 2 * BLK), 0:GROUP_W],
                        kv1_ref[pl.ds(start, 2 * BLK), GROUP_W:2 * GROUP_W], band_ref[jnp.minimum(j, 1)])
    store(o1_s, l1_s, pl.ds(pl.multiple_of(j * BLK, BLK), BLK), o, ls)

    n = j % blocks_per_res2
    start = pl.multiple_of(jnp.maximum(n - 1, 0) * BLK, BLK)
    o, ls = _attn_block(q2_ref[...], kv2_ref[pl.ds(start, 2 * BLK), 0:GROUP_W],
                        kv2_ref[pl.ds(start, 2 * BLK), GROUP_W:2 * GROUP_W], band_ref[jnp.minimum(n, 1)])
    store(o2_s, l2_s, pl.ds(n * (BLK * d2) + j // blocks_per_res2, BLK, stride=d2), o, ls)

    o, ls = _attn_block(q3_ref[...], kv3_ref[:, 0:GROUP_W], kv3_ref[:, GROUP_W:2 * GROUP_W], causal_ref[...])
    store(o3_s, l3_s, pl.ds(j, BLK, stride=d3), o, ls)

    @pl.when(j == pl.num_programs(1) - 1)
    def _():
        seq = out_ref.shape[0]
        rows_per_step = 256
        for s in range(nslab):
            def body(c, carry):
                rows = pl.ds(pl.multiple_of(c * rows_per_step, rows_per_step), rows_per_step)
                l1, l2, l3 = l1_s[s, rows, :], l2_s[s, rows, :], l3_s[s, rows, :]
                m = jnp.maximum(jnp.maximum(l1, l2), l3)
                e1, e2, e3 = jnp.exp(l1 - m), jnp.exp(l2 - m), jnp.exp(l3 - m)
                inv = 1.0 / (e1 + e2 + e3)
                out = (e1 * inv) * o1_s[s, rows, :] + (e2 * inv) * o2_s[s, rows, :] + (e3 * inv) * o3_s[s, rows, :]
                out_ref[rows, s * LANES:(s + 1) * LANES] = out
                return carry
            lax.fori_loop(0, seq // rows_per_step, body, 0)


def _attn_call(q1, q2, q3, kv1, kv2, kv3, band, causal, batch, seq):
    d2 = ATTN_GROUPS[1][1]
    d3 = ATTN_GROUPS[2][1]
    nblk = seq // BLK
    assert seq // d3 == BLK and nblk % d2 == 0 and nblk == d3
    bpr2 = nblk // d2
    scratch = [pltpu.VMEM((GROUP_W // LANES, seq, LANES), F32)] * 6
    return pl.pallas_call(
        _attn_kernel,
        grid=(batch, nblk),
        in_specs=[
            pl.BlockSpec((None, BLK, GROUP_W), lambda b, j: (b, j, 0)),
            pl.BlockSpec((None, None, BLK, GROUP_W), lambda b, j: (b, j // bpr2, j % bpr2, 0)),
            pl.BlockSpec((None, None, BLK, GROUP_W), lambda b, j: (b, j, 0, 0)),
            pl.BlockSpec((None, seq, 2 * GROUP_W), lambda b, j: (b, 0, 0)),
            pl.BlockSpec((None, None, seq // d2, 2 * GROUP_W), lambda b, j: (b, j // bpr2, 0, 0)),
            pl.BlockSpec((None, None, BLK, 2 * GROUP_W), lambda b, j: (b, j, 0, 0)),
            _const_spec(band.shape),
            _const_spec(causal.shape),
        ],
        out_specs=pl.BlockSpec((None, seq, GROUP_W), lambda b, j: (b, 0, 0)),
        out_shape=jax.ShapeDtypeStruct((batch, seq, GROUP_W), F32),
        scratch_shapes=scratch,
        compiler_params=pltpu.CompilerParams(dimension_semantics=("arbitrary", "arbitrary"),
                                             vmem_limit_bytes=VMEM_LIMIT),
        name="prompt_attn",
    )(q1, q2, q3, kv1, kv2, kv3, band, causal)


def _attn_bias_tables():
    qi = jnp.arange(BLK)[:, None]
    kj = jnp.arange(2 * BLK)[None, :]
    general = (kj >= qi) & (kj <= qi + BLK)
    first = kj <= qi
    band = jnp.where(jnp.stack([first, general]), 0.0, NEG).astype(F32)
    causal = jnp.where(jnp.arange(BLK)[None, :] <= qi, 0.0, NEG).astype(F32)
    return band, causal


def _mlp_kernel(x_ref, attn_ref, gpre_ref, wlo_ref, whi_ref, bm_ref, vng_ref, vnb_ref, ws_ref, bsp_ref,
                wpa_ref, wpb_ref, wout_ref, gpost_ref, out_ref):
    x = x_ref[...]
    rows = x.shape[0]
    hb = _rms(x, gpre_ref[...]).astype(BF16)
    z = jnp.concatenate([jnp.dot(hb, wlo_ref[...], preferred_element_type=F32),
                         jnp.dot(hb, whi_ref[...], preferred_element_type=F32)], axis=-1)
    ga, u, vn, gb, gates = _activations(z, bm_ref[...], vng_ref[...], vnb_ref[...])
    wr = lax.broadcasted_iota(jnp.int32, ws_ref.shape, 0) & (BLK - 1)
    wc = lax.broadcasted_iota(jnp.int32, ws_ref.shape, 1)
    ws = jnp.where(wc <= wr, ws_ref[...], 0.0).astype(BF16)
    lane = lax.broadcasted_iota(jnp.int32, (BLK, GMLP_W), 1)
    vnb16 = vn.astype(BF16)
    mixes = []
    for c in range(rows // BLK):
        r = jnp.dot(ws, vnb16[c * BLK:(c + 1) * BLK], preferred_element_type=F32)
        mix = r[(GMLP_GROUPS - 1) * BLK:]
        for g in range(GMLP_GROUPS - 2, -1, -1):
            mix = jnp.where(lane < (g + 1) * GMLP_GROUP_DIM, r[g * BLK:(g + 1) * BLK], mix)
        mixes.append(mix + bsp_ref[...])
    mix = jnp.concatenate(mixes, axis=0)
    out_ref[...] = _merge_out(x, attn_ref[...] * ga, (u * mix) * gb, gates,
                              wpa_ref[...], wpb_ref[...], wout_ref[...], gpost_ref[...])


def _mlp_call(layer, x, attn, g_pre, w_in, b_merge, vng, vnb, ws_stack, bs_full, wpa, wpb, wout, g_post):
    n, d = x.shape
    t = MLP_TILE
    row = lambda i: (i, 0)
    rest_w = w_in.shape[-1] - 3 * QKV_W
    assert rest_w == 2 * 3 * QKV_W, "rest columns are addressed as column blocks 1 and 2 of width 3 * QKV_W"
    w_specs = [_layer_spec(w_in, layer, 3 * QKV_W, 1), _layer_spec(w_in, layer, 3 * QKV_W, 2)]
    stacks = (b_merge, vng, vnb, ws_stack, bs_full, wpa, wpb, wout, g_post)
    return pl.pallas_call(
        _mlp_kernel,
        grid=(n // t,),
        in_specs=[pl.BlockSpec((t, d), row), pl.BlockSpec((t, GROUP_W), row), _layer_spec(g_pre, layer)] + w_specs
        + [_layer_spec(s, layer) for s in stacks],
        out_specs=pl.BlockSpec((t, d), row),
        out_shape=jax.ShapeDtypeStruct((n, d), F32),
        compiler_params=pltpu.CompilerParams(dimension_semantics=("arbitrary",), vmem_limit_bytes=VMEM_LIMIT),
        name="prompt_mlp",
    )(x, attn, g_pre, w_in, w_in, *stacks)


def _sample_in_kernel(x_ref, g_ref, w_ref, cq_ref, saq_ref, sbq_ref, ck_ref, sak_ref, sbk_ref,
                      bm_ref, vng_ref, vnb_ref,
                      q_ref, kv1_ref, kv2_ref, kv3_ref, ga_ref, u_ref, vn_ref, gb_ref, gates_ref):
    hb = _rms(x_ref[...], g_ref[...]).astype(BF16)
    z = jnp.dot(hb, w_ref[...], preferred_element_type=F32)
    kv_refs = (kv1_ref, kv2_ref, kv3_ref)
    for g in range(N_GROUPS):
        lo = g * GROUP_W
        q_ref[:, lo:lo + GROUP_W] = _rope(z[:, lo:lo + GROUP_W], cq_ref[...], saq_ref[...], sbq_ref[...])
        kv_refs[g][:, 0:GROUP_W] = _rope(z[:, QKV_W + lo:QKV_W + lo + GROUP_W], ck_ref[...], sak_ref[...], sbk_ref[...])
        kv_refs[g][:, GROUP_W:2 * GROUP_W] = z[:, 2 * QKV_W + lo:2 * QKV_W + lo + GROUP_W]
    ga, u, vn, gb, gates = _activations(z[:, 3 * QKV_W:], bm_ref[...], vng_ref[...], vnb_ref[...])
    ga_ref[...] = ga
    u_ref[...] = u
    vn_ref[...] = vn
    gb_ref[...] = gb
    gates_ref[...] = gates


def _sample_in_call(layer, x, g_pre, w_in, tabs_q, tabs_k, b_merge, vng, vnb):
    n, d = x.shape
    widths = [QKV_W] + [2 * GROUP_W] * 3 + [GROUP_W, GMLP_W, GMLP_W, GMLP_W, 2 * d]
    tabs = (*tabs_q, *tabs_k)
    return pl.pallas_call(
        _sample_in_kernel,
        grid=(1,),
        in_specs=[_const_spec(x.shape), _layer_spec(g_pre, layer), _layer_spec(w_in, layer)]
        + [_const_spec(t.shape) for t in tabs] + [_layer_spec(s, layer) for s in (b_merge, vng, vnb)],
        out_specs=[_const_spec((n, w)) for w in widths],
        out_shape=[jax.ShapeDtypeStruct((n, w), F32) for w in widths],
        compiler_params=pltpu.CompilerParams(dimension_semantics=("arbitrary",), vmem_limit_bytes=VMEM_LIMIT),
        name="sample_in",
    )(x, g_pre, w_in, *tabs, b_merge, vng, vnb)


def _sample_attn_kernel(q_ref, kvn1_ref, kvn2_ref, kvn3_ref, c1_ref, c2_ref, c3_ref, out_ref):
    for b in range(q_ref.shape[0]):
        outs, lses = [], []
        for g, (kvn_ref, c_ref) in enumerate(((kvn1_ref, c1_ref), (kvn2_ref, c2_ref), (kvn3_ref, c3_ref))):
            dil = ATTN_GROUPS[g][1]
            q = q_ref[b, g]
            kc, vc = c_ref[b, 0], c_ref[b, 1]
            kn, vn = kvn_ref[b, 0], kvn_ref[b, 1]
            rows = kc.shape[-1]
            sc = jnp.sum(kc * q, axis=1, keepdims=True)
            r = lax.broadcasted_iota(jnp.int32, (1, 1, rows), 2)
            sc = jnp.where((r & (dil - 1)) == 0, sc, NEG)
            ss = jnp.sum(kn * q, axis=1, keepdims=True)
            m = jnp.maximum(jnp.max(sc, axis=-1, keepdims=True), ss)
            pc = jnp.exp(sc - m)
            ps = jnp.exp(ss - m)
            l = jnp.sum(pc, axis=-1, keepdims=True) + ps
            o = jnp.sum(vc * pc, axis=-1, keepdims=True) + ps * vn
            outs.append(o / l)
            lses.append(m + jnp.log(l))
        m = jnp.maximum(jnp.maximum(lses[0], lses[1]), lses[2])
        es = [jnp.exp(l - m) for l in lses]
        inv = 1.0 / (es[0] + es[1] + es[2])
        out_ref[b] = (es[0] * inv) * outs[0] + (es[1] * inv) * outs[1] + (es[2] * inv) * outs[2]


def _sample_attn_call(layer, q, kvn, caches):
    n = q.shape[0]
    bt = SAMPLE_BT
    lead = lambda i: (i, 0, 0, 0, 0)
    cache_specs = []
    cache_views = []
    for (window, dil), c in zip(ATTN_GROUPS, caches):
        depth, nb, rows = c.shape[:3]
        assert rows == window and dil & (dil - 1) == 0, "cache must hold exactly the window"
        cache_views.append(jnp.transpose(c, (0, 1, 3, 4, 5, 2)))
        cache_specs.append(pl.BlockSpec((None, bt, 2, HEADS_PER_GROUP, HEAD_DIM, rows),
                                        lambda i: (layer, i, 0, 0, 0, 0)))
    return pl.pallas_call(
        _sample_attn_kernel,
        grid=(n // bt,),
        in_specs=[pl.BlockSpec((bt, N_GROUPS, HEADS_PER_GROUP, HEAD_DIM, 1), lead)]
        + [pl.BlockSpec((bt, 2, HEADS_PER_GROUP, HEAD_DIM, 1), lead)] * 3 + cache_specs,
        out_specs=pl.BlockSpec((bt, HEADS_PER_GROUP, HEAD_DIM, 1), lambda i: (i, 0, 0, 0)),
        out_shape=jax.ShapeDtypeStruct((n, HEADS_PER_GROUP, HEAD_DIM, 1), F32),
        compiler_params=pltpu.CompilerParams(dimension_semantics=("arbitrary",), vmem_limit_bytes=VMEM_LIMIT),
        name="sample_attn",
    )(q, *kvn, *cache_views)


def _sample_out_kernel(x_ref, attn_ref, ga_ref, u_ref, vn_ref, gb_ref, gates_ref, ws0_ref, bs0_ref,
                       wpa_ref, wpb_ref, wout_ref, gpost_ref, out_ref):
    mix = vn_ref[...] * ws0_ref[...] + bs0_ref[...]
    out_ref[...] = _merge_out(x_ref[...], attn_ref[...] * ga_ref[...], (u_ref[...] * mix) * gb_ref[...],
                              gates_ref[...], wpa_ref[...], wpb_ref[...], wout_ref[...], gpost_ref[...])


def _sample_out_call(layer, x, attn, ga, u, vn, gb, gates, ws0, bs0, wpa, wpb, wout, g_post):
    acts = (x, attn, ga, u, vn, gb, gates)
    stacks = (ws0, bs0, wpa, wpb, wout, g_post)
    return pl.pallas_call(
        _sample_out_kernel,
        grid=(1,),
        in_specs=[_const_spec(a.shape) for a in acts] + [_layer_spec(s, layer) for s in stacks],
        out_specs=_const_spec(x.shape),
        out_shape=jax.ShapeDtypeStruct(x.shape, F32),
        compiler_params=pltpu.CompilerParams(dimension_semantics=("arbitrary",), vmem_limit_bytes=VMEM_LIMIT),
        name="sample_out",
    )(*acts, *stacks)


def kernel(x_prompt, x_sample, cache_kv_w128, cache_kv_w512, cache_kv_w2048, norm_pre, w_in, b_merge, v_norm_g, v_norm_b, w_spatial, b_spatial, w_proj_a, w_proj_b, w_out, norm_post):
    batch, seq, d = x_prompt.shape
    nb, dec_seq, _ = x_sample.shape
    depth = w_in.shape[0]
    assert dec_seq == 1, "the sample group decodes one position per step"
    assert seq % QKV_TILE == 0 and (batch * seq) % MLP_TILE == 0 and nb % SAMPLE_BT == 0
    caches = (cache_kv_w128, cache_kv_w512, cache_kv_w2048)

    w_in_b = w_in.astype(BF16)
    wpa_b, wpb_b, wout_b = w_proj_a.astype(BF16), w_proj_b.astype(BF16), w_out.astype(BF16)
    ws_stack = w_spatial.reshape(depth, GMLP_GROUPS * BLK, BLK)
    chan_group = jnp.arange(GMLP_W) // GMLP_GROUP_DIM
    bs_full = jnp.swapaxes(b_spatial, 1, 2)[:, :, chan_group]
    ws0 = w_spatial[:, :, 0, 0][:, chan_group][:, None, :]
    bs0 = b_spatial[:, :, 0][:, chan_group][:, None, :]
    row2 = lambda a: a[:, None, :]
    g_pre, g_post, bm, vng, vnb = row2(norm_pre), row2(norm_post), row2(b_merge), row2(v_norm_g), row2(v_norm_b)

    scale = HEAD_DIM ** -0.5
    pos_p = jnp.arange(seq, dtype=jnp.int32)
    pos_s = PAST_LEN + jnp.arange(dec_seq, dtype=jnp.int32)
    tq_p, tk_p = _rope_tables(pos_p, scale), _rope_tables(pos_p, 1.0)
    tq_s, tk_s = _rope_tables(pos_s, scale), _rope_tables(pos_s, 1.0)
    band, causal = _attn_bias_tables()

    xp = x_prompt
    xs = x_sample.reshape(nb, d)
    kv_p = [[] for _ in ATTN_GROUPS]
    kv_s = [[] for _ in ATTN_GROUPS]
    v_s = []
    for l in range(depth):
        q1, kv1, q2, kv2, q3, kv3, p1, p2, p3 = _qkv_call(l, xp, g_pre, w_in_b, tq_p, tk_p)
        attn = _attn_call(q1, q2, q3, kv1, kv2, kv3, band, causal, batch, seq)
        xp = _mlp_call(l, xp.reshape(batch * seq, d), attn.reshape(batch * seq, GROUP_W), g_pre, w_in_b,
                       bm, vng, vnb, ws_stack, bs_full, wpa_b, wpb_b, wout_b, g_post)
        xp = xp.reshape(batch, seq, d)
        for g, p in enumerate((p1, p2, p3)):
            kv_p[g].append(p)
        qs, kn1, kn2, kn3, ga, u, vn, gb, gates = _sample_in_call(l, xs, g_pre, w_in_b, tq_s, tk_s, bm, vng, vnb)
        kns = [kn.reshape(nb, 2, HEADS_PER_GROUP, HEAD_DIM, 1) for kn in (kn1, kn2, kn3)]
        attn_s = _sample_attn_call(l, qs.reshape(nb, N_GROUPS, HEADS_PER_GROUP, HEAD_DIM, 1), kns, caches)
        xs = _sample_out_call(l, xs, attn_s.reshape(nb, GROUP_W), ga, u, vn, gb, gates, ws0, bs0,
                              wpa_b, wpb_b, wout_b, g_post)
        for g, kn in enumerate(kns):
            kv_s[g].append(kn.reshape(nb, dec_seq, 2, HEADS_PER_GROUP, HEAD_DIM))
        v_s.append(vn.reshape(nb, dec_seq, GMLP_W))

    new_kv_p = [jnp.transpose(jnp.stack(ps), (0, 1, 5, 2, 3, 4)) for ps in kv_p]
    return (xp, xs.reshape(nb, dec_seq, d), new_kv_p[0], new_kv_p[1], new_kv_p[2],
            jnp.stack(kv_s[0]), jnp.stack(kv_s[1]), jnp.stack(kv_s[2]), jnp.stack(v_s))
```

```python
import functools

import jax
import jax.numpy as jnp
from jax import lax
from jax.experimental import pallas as pl
from jax.experimental.pallas import tpu as pltpu

F32 = jnp.float32
BF16 = jnp.bfloat16

PAST_LEN = 8192
HEAD_DIM = 64
HEADS_PER_GROUP = 4
GROUP_W = HEADS_PER_GROUP * HEAD_DIM
ATTN_GROUPS = ((128, 1), (512, 4), (2048, 16))
N_GROUPS = len(ATTN_GROUPS)
QKV_W = N_GROUPS * GROUP_W
ROT_DIM = HEAD_DIM // 4
ROT_HALF = ROT_DIM // 2
ROPE_THETA = 500000.0
BLK = 128
GMLP_GROUPS = 4
GMLP_GROUP_DIM = 192
GMLP_W = GMLP_GROUPS * GMLP_GROUP_DIM
EPS = 1e-6
LANES = 128
NEG = -1e30

OFF_GA = 0
OFF_U = OFF_GA + GROUP_W
OFF_VB = OFF_U + GMLP_W
OFF_GB = OFF_VB + GMLP_W
OFF_ML = OFF_GB + GMLP_W

QKV_TILE = 512
MLP_TILE = 256
SAMPLE_BT = 2
VMEM_LIMIT = 56 * 1024 * 1024


def _rms(x, g):
    return (x * lax.rsqrt(jnp.mean(x * x, axis=-1, keepdims=True) + EPS)) * g


def _layer_norm(x, g, b):
    xc = x - jnp.mean(x, axis=-1, keepdims=True)
    return xc * lax.rsqrt(jnp.mean(xc * xc, axis=-1, keepdims=True) + EPS) * g + b


def _rope(z, c, sa, sb):
    outs = []
    for s in range(GROUP_W // LANES):
        zs = z[:, s * LANES:(s + 1) * LANES]
        outs.append(zs * c + pltpu.roll(zs, LANES - ROT_HALF, 1) * sa + pltpu.roll(zs, ROT_HALF, 1) * sb)
    return jnp.concatenate(outs, axis=-1)


def _rope_tables(pos, scale):
    inv_freq = jnp.power(ROPE_THETA, -jnp.arange(ROT_HALF, dtype=F32) / ROT_HALF)
    ang = pos.astype(F32)[:, None] * inv_freq[None, :]
    cos, sin = jnp.cos(ang), jnp.sin(ang)
    n = pos.shape[0]
    rest1 = jnp.ones((n, HEAD_DIM - ROT_DIM), F32)
    rest0 = jnp.zeros((n, HEAD_DIM - ROT_DIM), F32)
    z8 = jnp.zeros((n, ROT_HALF), F32)
    c = jnp.concatenate([cos, cos, rest1], axis=-1)
    sa = jnp.concatenate([-sin, z8, rest0], axis=-1)
    sb = jnp.concatenate([z8, sin, rest0], axis=-1)
    reps = LANES // HEAD_DIM
    return tuple(jnp.tile(t, (1, reps)) * scale for t in (c, sa, sb))


def _activations(z, b_merge, vng, vnb):
    ga = jax.nn.silu(z[:, OFF_GA:OFF_U])
    u = jax.nn.gelu(z[:, OFF_U:OFF_VB])
    vn = _layer_norm(jax.nn.gelu(z[:, OFF_VB:OFF_GB]), vng, vnb)
    gb = jax.nn.silu(z[:, OFF_GB:OFF_ML])
    gates = jax.nn.sigmoid(z[:, OFF_ML:] + b_merge)
    return ga, u, vn, gb, gates


def _merge_out(x, a_in, b_in, gates, wpa, wpb, wout, gpost):
    d = x.shape[-1]
    ba = jnp.dot(a_in.astype(BF16), wpa, preferred_element_type=F32)
    bb = jnp.dot(b_in.astype(BF16), wpb, preferred_element_type=F32)
    merged = gates[:, :d] * ba + gates[:, d:] * bb
    y = jnp.dot(merged.astype(BF16), wout, preferred_element_type=F32)
    return x + _rms(y, gpost)


def _const_spec(shape):
    nd = len(shape)
    return pl.BlockSpec(shape, lambda *_: (0,) * nd)


def _layer_spec(arr, layer, block_cols=None, col_block=0):
    rows, cols = arr.shape[1:]
    return pl.BlockSpec((None, rows, cols if block_cols is None else block_cols), lambda *_: (layer, 0, col_block))


def _qkv_kernel(seq, x_ref, g_ref, w_ref, cq_ref, saq_ref, sbq_ref, ck_ref, sak_ref, sbk_ref,
                q1_ref, kv1_ref, q2_ref, kv2_ref, q3_ref, kv3_ref, p1_ref, p2_ref, p3_ref, dei_s):
    last_tile = pl.program_id(1) == pl.num_programs(1) - 1
    t = x_ref.shape[0]
    hb = _rms(x_ref[...], g_ref[...]).astype(BF16)
    q_refs = (q1_ref, q2_ref, q3_ref)
    kv_refs = (kv1_ref, kv2_ref, kv3_ref)
    p_refs = (p1_ref, p2_ref, p3_ref)
    cq, saq, sbq = cq_ref[...], saq_ref[...], sbq_ref[...]
    ck, sak, sbk = ck_ref[...], sak_ref[...], sbk_ref[...]
    nslab = GROUP_W // LANES

    slots = iter(range(dei_s.shape[0]))

    def split_by_residue(z, dst_ref, lane0, dil):
        slot = next(slots)
        for s in range(nslab):
            dei_s[slot, s] = z[:, s * LANES:(s + 1) * LANES]
        for r in range(dil):
            rows = [dei_s[slot, s, pl.ds(r, t // dil, stride=dil), :] for s in range(nslab)]
            dst_ref[r, :, lane0:lane0 + GROUP_W] = jnp.concatenate(rows, axis=-1).astype(BF16)

    def store_transposed(p_ref, idx, z):
        p_ref[idx] = z.T.reshape(HEADS_PER_GROUP, HEAD_DIM, z.shape[0])

    for g, (window, dil) in enumerate(ATTN_GROUPS):
        zq = jnp.dot(hb, w_ref[:, g * GROUP_W:(g + 1) * GROUP_W], preferred_element_type=F32)
        zq = _rope(zq, cq, saq, sbq)
        zk = jnp.dot(hb, w_ref[:, QKV_W + g * GROUP_W:QKV_W + (g + 1) * GROUP_W], preferred_element_type=F32)
        zk = _rope(zk, ck, sak, sbk)
        zv = jnp.dot(hb, w_ref[:, 2 * QKV_W + g * GROUP_W:2 * QKV_W + (g + 1) * GROUP_W],
                     preferred_element_type=F32)
        if dil == 1:
            q_refs[g][...] = zq.astype(BF16)
            kv_refs[g][:, 0:GROUP_W] = zk.astype(BF16)
            kv_refs[g][:, GROUP_W:2 * GROUP_W] = zv.astype(BF16)
        else:
            split_by_residue(zq, q_refs[g], 0, dil)
            split_by_residue(zk, kv_refs[g], 0, dil)
            split_by_residue(zv, kv_refs[g], GROUP_W, dil)
        keep = p_refs[g].shape[-1]
        if window >= seq:
            store_transposed(p_refs[g], 0, zk)
            store_transposed(p_refs[g], 1, zv)
        else:
            @pl.when(last_tile)
            def _(g=g, zk=zk, zv=zv, keep=keep):
                store_transposed(p_refs[g], 0, zk[t - keep:])
                store_transposed(p_refs[g], 1, zv[t - keep:])


def _qkv_call(layer, x, g_pre, w_in, tabs_q, tabs_k):
    batch, seq, d = x.shape
    t = QKV_TILE
    nt = seq // t
    tab_spec = pl.BlockSpec((t, LANES), lambda b, c: (c, 0))
    in_specs = [pl.BlockSpec((None, t, d), lambda b, c: (b, c, 0)), _layer_spec(g_pre, layer),
                _layer_spec(w_in, layer, 3 * QKV_W, 0)]
    out_specs, out_shape = [], []
    for window, dil in ATTN_GROUPS:
        assert t % (dil * 16) == 0
        if dil == 1:
            q_blk, kv_blk = (None, t, GROUP_W), (None, t, 2 * GROUP_W)
            q_shape, kv_shape = (batch, seq, GROUP_W), (batch, seq, 2 * GROUP_W)
            idx = lambda b, c: (b, c, 0)
        else:
            q_blk, kv_blk = (None, dil, t // dil, GROUP_W), (None, dil, t // dil, 2 * GROUP_W)
            q_shape, kv_shape = (batch, dil, seq // dil, GROUP_W), (batch, dil, seq // dil, 2 * GROUP_W)
            idx = lambda b, c: (b, 0, c, 0)
        out_specs += [pl.BlockSpec(q_blk, idx), pl.BlockSpec(kv_blk, idx)]
        out_shape += [jax.ShapeDtypeStruct(q_shape, BF16), jax.ShapeDtypeStruct(kv_shape, BF16)]
    for window, dil in ATTN_GROUPS:
        keep = min(window, seq)
        assert keep == seq or keep <= t, "a partial window must fit in the last tile"
        blk = min(keep, t)
        p_idx = (lambda b, c: (b, 0, 0, 0, c)) if keep == seq else (lambda b, c: (b, 0, 0, 0, 0))
        out_specs.append(pl.BlockSpec((None, 2, HEADS_PER_GROUP, HEAD_DIM, blk), p_idx))
        out_shape.append(jax.ShapeDtypeStruct((batch, 2, HEADS_PER_GROUP, HEAD_DIM, keep), F32))
    return pl.pallas_call(
        functools.partial(_qkv_kernel, seq),
        grid=(batch, nt),
        in_specs=in_specs + [tab_spec] * 6,
        out_specs=out_specs,
        out_shape=out_shape,
        scratch_shapes=[pltpu.VMEM((3 * sum(dil > 1 for _, dil in ATTN_GROUPS), GROUP_W // LANES, t, LANES), F32)],
        compiler_params=pltpu.CompilerParams(dimension_semantics=("arbitrary", "arbitrary"),
                                             vmem_limit_bytes=VMEM_LIMIT),
        name="prompt_qkv",
    )(x, g_pre, w_in, *tabs_q, *tabs_k)


def _attn_block(q, k, v, bias):
    nk = k.shape[0]
    hid = lax.shift_right_logical(lax.broadcasted_iota(jnp.int32, (BLK, GROUP_W), 1), HEAD_DIM.bit_length() - 1)
    qf = q.astype(F32)
    qs = jnp.concatenate([jnp.where(hid == h, qf, 0.0) for h in range(HEADS_PER_GROUP)], axis=0).astype(BF16)
    s = lax.dot_general(qs, k, (((1,), (1,)), ((), ())), preferred_element_type=F32)
    s = s.reshape(HEADS_PER_GROUP, BLK, nk) + bias[None]
    m = jnp.max(s, axis=-1, keepdims=True)
    p = jnp.exp(s - m)
    l = jnp.sum(p, axis=-1, keepdims=True)
    pv = jnp.dot(p.reshape(HEADS_PER_GROUP * BLK, nk).astype(BF16), v, preferred_element_type=F32)
    pv = pv.reshape(HEADS_PER_GROUP, BLK, GROUP_W)
    inv = 1.0 / l
    lse = m + jnp.log(l)
    o = jnp.zeros((BLK, GROUP_W), F32)
    ls = jnp.zeros((BLK, GROUP_W), F32)
    for h in range(HEADS_PER_GROUP):
        o = jnp.where(hid == h, pv[h] * inv[h], o)
        ls = jnp.where(hid == h, lse[h], ls)
    return o, ls


def _attn_kernel(q1_ref, q2_ref, q3_ref, kv1_ref, kv2_ref, kv3_ref, band_ref, causal_ref, out_ref,
                 o1_s, l1_s, o2_s, l2_s, o3_s, l3_s):
    j = pl.program_id(1)
    nslab = GROUP_W // LANES
    d2 = ATTN_GROUPS[1][1]
    d3 = ATTN_GROUPS[2][1]
    blocks_per_res2 = pl.num_programs(1) // d2

    def store(o_s, l_s, rows, o, ls):
        for s in range(nslab):
            o_s[s, rows, :] = o[:, s * LANES:(s + 1) * LANES]
            l_s[s, rows, :] = ls[:, s * LANES:(s + 1) * LANES]

    start = pl.multiple_of(jnp.maximum(j - 1, 0) * BLK, BLK)
    o, ls = _attn_block(q1_ref[...], kv1_ref[pl.ds(start, 2 * BLK), 0:GROUP_W],
                        kv1_ref[pl.ds(start, 2 * BLK), GROUP_W:2 * GROUP_W], band_ref[jnp.minimum(j, 1)])
    store(o1_s, l1_s, pl.ds(pl.multiple_of(j * BLK, BLK), BLK), o, ls)

    n = j % blocks_per_res2
    start = pl.multiple_of(jnp.maximum(n - 1, 0) * BLK, BLK)
    o, ls = _attn_block(q2_ref[...], kv2_ref[pl.ds(start, 2 * BLK), 0:GROUP_W],
                        kv2_ref[pl.ds(start, 2 * BLK), GROUP_W:2 * GROUP_W], band_ref[jnp.minimum(n, 1)])
    store(o2_s, l2_s, pl.ds(n * (BLK * d2) + j // blocks_per_res2, BLK, stride=d2), o, ls)

    o, ls = _attn_block(q3_ref[...], kv3_ref[:, 0:GROUP_W], kv3_ref[:, GROUP_W:2 * GROUP_W], causal_ref[...])
    store(o3_s, l3_s, pl.ds(j, BLK, stride=d3), o, ls)

    @pl.when(j == pl.num_programs(1) - 1)
    def _():
        seq = out_ref.shape[0]
        rows_per_step = 256
        for s in range(nslab):
            def body(c, carry):
                rows = pl.ds(pl.multiple_of(c * rows_per_step, rows_per_step), rows_per_step)
                l1, l2, l3 = l1_s[s, rows, :], l2_s[s, rows, :], l3_s[s, rows, :]
                m = jnp.maximum(jnp.maximum(l1, l2), l3)
                e1, e2, e3 = jnp.exp(l1 - m), jnp.exp(l2 - m), jnp.exp(l3 - m)
                inv = 1.0 / (e1 + e2 + e3)
                out = (e1 * inv) * o1_s[s, rows, :] + (e2 * inv) * o2_s[s, rows, :] + (e3 * inv) * o3_s[s, rows, :]
                out_ref[rows, s * LANES:(s + 1) * LANES] = out
                return carry
            lax.fori_loop(0, seq // rows_per_step, body, 0)


def _attn_call(q1, q2, q3, kv1, kv2, kv3, band, causal, batch, seq):
    d2 = ATTN_GROUPS[1][1]
    d3 = ATTN_GROUPS[2][1]
    nblk = seq // BLK
    assert seq // d3 == BLK and nblk % d2 == 0 and nblk == d3
    bpr2 = nblk // d2
    scratch = [pltpu.VMEM((GROUP_W // LANES, seq, LANES), F32)] * 6
    return pl.pallas_call(
        _attn_kernel,
        grid=(batch, nblk),
        in_specs=[
            pl.BlockSpec((None, BLK, GROUP_W), lambda b, j: (b, j, 0)),
            pl.BlockSpec((None, None, BLK, GROUP_W), lambda b, j: (b, j // bpr2, j % bpr2, 0)),
            pl.BlockSpec((None, None, BLK, GROUP_W), lambda b, j: (b, j, 0, 0)),
            pl.BlockSpec((None, seq, 2 * GROUP_W), lambda b, j: (b, 0, 0)),
            pl.BlockSpec((None, None, seq // d2, 2 * GROUP_W), lambda b, j: (b, j // bpr2, 0, 0)),
            pl.BlockSpec((None, None, BLK, 2 * GROUP_W), lambda b, j: (b, j, 0, 0)),
            _const_spec(band.shape),
            _const_spec(causal.shape),
        ],
        out_specs=pl.BlockSpec((None, seq, GROUP_W), lambda b, j: (b, 0, 0)),
        out_shape=jax.ShapeDtypeStruct((batch, seq, GROUP_W), F32),
        scratch_shapes=scratch,
        compiler_params=pltpu.CompilerParams(dimension_semantics=("arbitrary", "arbitrary"),
                                             vmem_limit_bytes=VMEM_LIMIT),
        name="prompt_attn",
    )(q1, q2, q3, kv1, kv2, kv3, band, causal)


def _attn_bias_tables():
    qi = jnp.arange(BLK)[:, None]
    kj = jnp.arange(2 * BLK)[None, :]
    general = (kj >= qi) & (kj <= qi + BLK)
    first = kj <= qi
    band = jnp.where(jnp.stack([first, general]), 0.0, NEG).astype(F32)
    causal = jnp.where(jnp.arange(BLK)[None, :] <= qi, 0.0, NEG).astype(F32)
    return band, causal


def _mlp_kernel(x_ref, attn_ref, gpre_ref, wlo_ref, whi_ref, bm_ref, vng_ref, vnb_ref, ws_ref, bsp_ref,
                wpa_ref, wpb_ref, wout_ref, gpost_ref, out_ref):
    x = x_ref[...]
    rows = x.shape[0]
    hb = _rms(x, gpre_ref[...]).astype(BF16)
    z = jnp.concatenate([jnp.dot(hb, wlo_ref[...], preferred_element_type=F32),
                         jnp.dot(hb, whi_ref[...], preferred_element_type=F32)], axis=-1)
    ga, u, vn, gb, gates = _activations(z, bm_ref[...], vng_ref[...], vnb_ref[...])
    wr = lax.broadcasted_iota(jnp.int32, ws_ref.shape, 0) & (BLK - 1)
    wc = lax.broadcasted_iota(jnp.int32, ws_ref.shape, 1)
    ws = jnp.where(wc <= wr, ws_ref[...], 0.0).astype(BF16)
    lane = lax.broadcasted_iota(jnp.int32, (BLK, GMLP_W), 1)
    vnb16 = vn.astype(BF16)
    mixes = []
    for c in range(rows // BLK):
        r = jnp.dot(ws, vnb16[c * BLK:(c + 1) * BLK], preferred_element_type=F32)
        mix = r[(GMLP_GROUPS - 1) * BLK:]
        for g in range(GMLP_GROUPS - 2, -1, -1):
            mix = jnp.where(lane < (g + 1) * GMLP_GROUP_DIM, r[g * BLK:(g + 1) * BLK], mix)
        mixes.append(mix + bsp_ref[...])
    mix = jnp.concatenate(mixes, axis=0)
    out_ref[...] = _merge_out(x, attn_ref[...] * ga, (u * mix) * gb, gates,
                              wpa_ref[...], wpb_ref[...], wout_ref[...], gpost_ref[...])


def _mlp_call(layer, x, attn, g_pre, w_in, b_merge, vng, vnb, ws_stack, bs_full, wpa, wpb, wout, g_post):
    n, d = x.shape
    t = MLP_TILE
    row = lambda i: (i, 0)
    rest_w = w_in.shape[-1] - 3 * QKV_W
    assert rest_w == 2 * 3 * QKV_W, "rest columns are addressed as column blocks 1 and 2 of width 3 * QKV_W"
    w_specs = [_layer_spec(w_in, layer, 3 * QKV_W, 1), _layer_spec(w_in, layer, 3 * QKV_W, 2)]
    stacks = (b_merge, vng, vnb, ws_stack, bs_full, wpa, wpb, wout, g_post)
    return pl.pallas_call(
        _mlp_kernel,
        grid=(n // t,),
        in_specs=[pl.BlockSpec((t, d), row), pl.BlockSpec((t, GROUP_W), row), _layer_spec(g_pre, layer)] + w_specs
        + [_layer_spec(s, layer) for s in stacks],
        out_specs=pl.BlockSpec((t, d), row),
        out_shape=jax.ShapeDtypeStruct((n, d), F32),
        compiler_params=pltpu.CompilerParams(dimension_semantics=("arbitrary",), vmem_limit_bytes=VMEM_LIMIT),
        name="prompt_mlp",
    )(x, attn, g_pre, w_in, w_in, *stacks)


def _sample_in_kernel(x_ref, g_ref, w_ref, cq_ref, saq_ref, sbq_ref, ck_ref, sak_ref, sbk_ref,
                      bm_ref, vng_ref, vnb_ref,
                      kv1_ref, kv2_ref, kv3_ref, qt_ref, kvt1_ref, kvt2_ref, kvt3_ref,
                      ga_ref, u_ref, vn_ref, gb_ref, gates_ref):
    hb = _rms(x_ref[...], g_ref[...]).astype(BF16)
    z = jnp.dot(hb, w_ref[...], preferred_element_type=F32)
    bt = qt_ref.shape[-1]

    def store_tiles(dst_ref, rows):
        cols = rows.T
        for i in range(dst_ref.shape[0]):
            dst_ref[i] = cols[:, i * bt:(i + 1) * bt]

    qs = []
    for g, (kv_ref, kvt_ref) in enumerate(((kv1_ref, kvt1_ref), (kv2_ref, kvt2_ref), (kv3_ref, kvt3_ref))):
        lo = g * GROUP_W
        qs.append(_rope(z[:, lo:lo + GROUP_W], cq_ref[...], saq_ref[...], sbq_ref[...]))
        k = _rope(z[:, QKV_W + lo:QKV_W + lo + GROUP_W], ck_ref[...], sak_ref[...], sbk_ref[...])
        kv = jnp.concatenate([k, z[:, 2 * QKV_W + lo:2 * QKV_W + lo + GROUP_W]], axis=-1)
        kv_ref[...] = kv
        store_tiles(kvt_ref, kv)
    store_tiles(qt_ref, jnp.concatenate(qs, axis=-1))
    ga, u, vn, gb, gates = _activations(z[:, 3 * QKV_W:], bm_ref[...], vng_ref[...], vnb_ref[...])
    ga_ref[...] = ga
    u_ref[...] = u
    vn_ref[...] = vn
    gb_ref[...] = gb
    gates_ref[...] = gates


def _sample_in_call(layer, x, g_pre, w_in, tabs_q, tabs_k, b_merge, vng, vnb):
    n, d = x.shape
    bt = SAMPLE_BT
    shapes = ([(n, 2 * GROUP_W)] * 3 + [(n // bt, QKV_W, bt)] + [(n // bt, 2 * GROUP_W, bt)] * 3
              + [(n, w) for w in (GROUP_W, GMLP_W, GMLP_W, GMLP_W, 2 * d)])
    tabs = (*tabs_q, *tabs_k)
    return pl.pallas_call(
        _sample_in_kernel,
        grid=(1,),
        in_specs=[_const_spec(x.shape), _layer_spec(g_pre, layer), _layer_spec(w_in, layer)]
        + [_const_spec(t.shape) for t in tabs] + [_layer_spec(s, layer) for s in (b_merge, vng, vnb)],
        out_specs=[_const_spec(s) for s in shapes],
        out_shape=[jax.ShapeDtypeStruct(s, F32) for s in shapes],
        compiler_params=pltpu.CompilerParams(dimension_semantics=("arbitrary",), vmem_limit_bytes=VMEM_LIMIT),
        name="sample_in",
    )(x, g_pre, w_in, *tabs, b_merge, vng, vnb)


def _sample_attn_kernel(q_ref, kvn1_ref, kvn2_ref, kvn3_ref, c1_ref, c2_ref, c3_ref, out_ref):
    def column(ref, lo, b):
        return ref[lo:lo + GROUP_W, b:b + 1].reshape(HEADS_PER_GROUP, HEAD_DIM, 1)

    for b in range(q_ref.shape[-1]):
        outs, lses = [], []
        for g, (kvn_ref, c_ref) in enumerate(((kvn1_ref, c1_ref), (kvn2_ref, c2_ref), (kvn3_ref, c3_ref))):
            dil = ATTN_GROUPS[g][1]
            q = column(q_ref, g * GROUP_W, b)
            kc, vc = c_ref[b, 0], c_ref[b, 1]
            kn, vn = column(kvn_ref, 0, b), column(kvn_ref, GROUP_W, b)
            rows = kc.shape[-1]
            sc = jnp.sum(kc * q, axis=1, keepdims=True)
            r = lax.broadcasted_iota(jnp.int32, (1, 1, rows), 2)
            sc = jnp.where((r & (dil - 1)) == 0, sc, NEG)
            ss = jnp.sum(kn * q, axis=1, keepdims=True)
            m = jnp.maximum(jnp.max(sc, axis=-1, keepdims=True), ss)
            pc = jnp.exp(sc - m)
            ps = jnp.exp(ss - m)
            l = jnp.sum(pc, axis=-1, keepdims=True) + ps
            o = jnp.sum(vc * pc, axis=-1, keepdims=True) + ps * vn
            outs.append(o / l)
            lses.append(m + jnp.log(l))
        m = jnp.maximum(jnp.maximum(lses[0], lses[1]), lses[2])
        es = [jnp.exp(l - m) for l in lses]
        inv = 1.0 / (es[0] + es[1] + es[2])
        merged = (es[0] * inv) * outs[0] + (es[1] * inv) * outs[1] + (es[2] * inv) * outs[2]
        out_ref[:, b:b + 1] = merged.reshape(GROUP_W, 1)


def _sample_attn_call(layer, q, kvn, caches):
    tiles, _, bt = q.shape
    n = tiles * bt
    tile = lambda i: (i, 0, 0)
    cache_specs = []
    cache_views = []
    for (window, dil), c in zip(ATTN_GROUPS, caches):
        depth, nb, rows = c.shape[:3]
        assert rows == window and dil & (dil - 1) == 0, "cache must hold exactly the window"
        cache_views.append(jnp.transpose(c, (0, 1, 3, 4, 5, 2)))
        cache_specs.append(pl.BlockSpec((None, bt, 2, HEADS_PER_GROUP, HEAD_DIM, rows),
                                        lambda i: (layer, i, 0, 0, 0, 0)))
    return pl.pallas_call(
        _sample_attn_kernel,
        grid=(n // bt,),
        in_specs=[pl.BlockSpec((None, QKV_W, bt), tile)] + [pl.BlockSpec((None, 2 * GROUP_W, bt), tile)] * 3 + cache_specs,
        out_specs=pl.BlockSpec((None, GROUP_W, bt), tile),
        out_shape=jax.ShapeDtypeStruct((tiles, GROUP_W, bt), F32),
        compiler_params=pltpu.CompilerParams(dimension_semantics=("arbitrary",), vmem_limit_bytes=VMEM_LIMIT),
        name="sample_attn",
    )(q, *kvn, *cache_views)


def _sample_out_kernel(x_ref, attn_ref, ga_ref, u_ref, vn_ref, gb_ref, gates_ref, ws0_ref, bs0_ref,
                       wpa_ref, wpb_ref, wout_ref, gpost_ref, out_ref):
    mix = vn_ref[...] * ws0_ref[...] + bs0_ref[...]
    attn = jnp.concatenate([attn_ref[i] for i in range(attn_ref.shape[0])], axis=-1).T
    out_ref[...] = _merge_out(x_ref[...], attn * ga_ref[...], (u_ref[...] * mix) * gb_ref[...],
                              gates_ref[...], wpa_ref[...], wpb_ref[...], wout_ref[...], gpost_ref[...])


def _sample_out_call(layer, x, attn, ga, u, vn, gb, gates, ws0, bs0, wpa, wpb, wout, g_post):
    acts = (x, attn, ga, u, vn, gb, gates)
    stacks = (ws0, bs0, wpa, wpb, wout, g_post)
    return pl.pallas_call(
        _sample_out_kernel,
        grid=(1,),
        in_specs=[_const_spec(a.shape) for a in acts] + [_layer_spec(s, layer) for s in stacks],
        out_specs=_const_spec(x.shape),
        out_shape=jax.ShapeDtypeStruct(x.shape, F32),
        compiler_params=pltpu.CompilerParams(dimension_semantics=("arbitrary",), vmem_limit_bytes=VMEM_LIMIT),
        name="sample_out",
    )(*acts, *stacks)


def kernel(x_prompt, x_sample, cache_kv_w128, cache_kv_w512, cache_kv_w2048, norm_pre, w_in, b_merge, v_norm_g, v_norm_b, w_spatial, b_spatial, w_proj_a, w_proj_b, w_out, norm_post):
    batch, seq, d = x_prompt.shape
    nb, dec_seq, _ = x_sample.shape
    depth = w_in.shape[0]
    assert dec_seq == 1, "the sample group decodes one position per step"
    assert seq % QKV_TILE == 0 and (batch * seq) % MLP_TILE == 0 and nb % SAMPLE_BT == 0
    caches = (cache_kv_w128, cache_kv_w512, cache_kv_w2048)

    w_in_b = w_in.astype(BF16)
    wpa_b, wpb_b, wout_b = w_proj_a.astype(BF16), w_proj_b.astype(BF16), w_out.astype(BF16)
    ws_stack = w_spatial.reshape(depth, GMLP_GROUPS * BLK, BLK)
    chan_group = jnp.arange(GMLP_W) // GMLP_GROUP_DIM
    bs_full = jnp.swapaxes(b_spatial, 1, 2)[:, :, chan_group]
    ws0 = w_spatial[:, :, 0, 0][:, chan_group][:, None, :]
    bs0 = b_spatial[:, :, 0][:, chan_group][:, None, :]
    row2 = lambda a: a[:, None, :]
    g_pre, g_post, bm, vng, vnb = row2(norm_pre), row2(norm_post), row2(b_merge), row2(v_norm_g), row2(v_norm_b)

    scale = HEAD_DIM ** -0.5
    pos_p = jnp.arange(seq, dtype=jnp.int32)
    pos_s = PAST_LEN + jnp.arange(dec_seq, dtype=jnp.int32)
    tq_p, tk_p = _rope_tables(pos_p, scale), _rope_tables(pos_p, 1.0)
    tq_s, tk_s = _rope_tables(pos_s, scale), _rope_tables(pos_s, 1.0)
    band, causal = _attn_bias_tables()

    xp = x_prompt
    xs = x_sample.reshape(nb, d)
    kv_p = [[] for _ in ATTN_GROUPS]
    kv_s = [[] for _ in ATTN_GROUPS]
    v_s = []
    for l in range(depth):
        q1, kv1, q2, kv2, q3, kv3, p1, p2, p3 = _qkv_call(l, xp, g_pre, w_in_b, tq_p, tk_p)
        attn = _attn_call(q1, q2, q3, kv1, kv2, kv3, band, causal, batch, seq)
        xp = _mlp_call(l, xp.reshape(batch * seq, d), attn.reshape(batch * seq, GROUP_W), g_pre, w_in_b,
                       bm, vng, vnb, ws_stack, bs_full, wpa_b, wpb_b, wout_b, g_post)
        xp = xp.reshape(batch, seq, d)
        for g, p in enumerate((p1, p2, p3)):
            kv_p[g].append(p)
        kn1, kn2, kn3, qt, kt1, kt2, kt3, ga, u, vn, gb, gates = _sample_in_call(l, xs, g_pre, w_in_b, tq_s, tk_s,
                                                                                 bm, vng, vnb)
        attn_s = _sample_attn_call(l, qt, (kt1, kt2, kt3), caches)
        xs = _sample_out_call(l, xs, attn_s, ga, u, vn, gb, gates, ws0, bs0, wpa_b, wpb_b, wout_b, g_post)
        for g, kn in enumerate((kn1, kn2, kn3)):
            kv_s[g].append(kn.reshape(nb, dec_seq, 2, HEADS_PER_GROUP, HEAD_DIM))
        v_s.append(vn.reshape(nb, dec_seq, GMLP_W))

    new_kv_p = [jnp.transpose(jnp.stack(ps), (0, 1, 5, 2, 3, 4)) for ps in kv_p]
    return (xp, xs.reshape(nb, dec_seq, d), new_kv_p[0], new_kv_p[1], new_kv_p[2],
            jnp.stack(kv_s[0]), jnp.stack(kv_s[1]), jnp.stack(kv_s[2]), jnp.stack(v_s))
```

```python
import functools

import jax
import jax.numpy as jnp
from jax import lax
from jax.experimental import pallas as pl
from jax.experimental.pallas import tpu as pltpu

F32 = jnp.float32
BF16 = jnp.bfloat16

PAST_LEN = 8192
HEAD_DIM = 64
HEADS_PER_GROUP = 4
GROUP_W = HEADS_PER_GROUP * HEAD_DIM
ATTN_GROUPS = ((128, 1), (512, 4), (2048, 16))
N_GROUPS = len(ATTN_GROUPS)
QKV_W = N_GROUPS * GROUP_W
ROT_DIM = HEAD_DIM // 4
ROT_HALF = ROT_DIM // 2
ROPE_THETA = 500000.0
BLK = 128
GMLP_GROUPS = 4
GMLP_GROUP_DIM = 192
GMLP_W = GMLP_GROUPS * GMLP_GROUP_DIM
EPS = 1e-6
LANES = 128
NEG = -1e30

OFF_GA = 0
OFF_U = OFF_GA + GROUP_W
OFF_VB = OFF_U + GMLP_W
OFF_GB = OFF_VB + GMLP_W
OFF_ML = OFF_GB + GMLP_W

QKV_TILE = 512
MLP_TILE = 512
SAMPLE_BT = 2
ATTN_STEP_BLOCKS = 4
VMEM_LIMIT = 56 * 1024 * 1024


def _rms(x, g):
    return (x * lax.rsqrt(jnp.mean(x * x, axis=-1, keepdims=True) + EPS)) * g


def _layer_norm(x, g, b):
    xc = x - jnp.mean(x, axis=-1, keepdims=True)
    return xc * lax.rsqrt(jnp.mean(xc * xc, axis=-1, keepdims=True) + EPS) * g + b


def _rope(z, c, sa, sb):
    outs = []
    for s in range(GROUP_W // LANES):
        zs = z[:, s * LANES:(s + 1) * LANES]
        outs.append(zs * c + pltpu.roll(zs, LANES - ROT_HALF, 1) * sa + pltpu.roll(zs, ROT_HALF, 1) * sb)
    return jnp.concatenate(outs, axis=-1)


def _rope_tables(pos, scale):
    inv_freq = jnp.power(ROPE_THETA, -jnp.arange(ROT_HALF, dtype=F32) / ROT_HALF)
    ang = pos.astype(F32)[:, None] * inv_freq[None, :]
    cos, sin = jnp.cos(ang), jnp.sin(ang)
    n = pos.shape[0]
    rest1 = jnp.ones((n, HEAD_DIM - ROT_DIM), F32)
    rest0 = jnp.zeros((n, HEAD_DIM - ROT_DIM), F32)
    z8 = jnp.zeros((n, ROT_HALF), F32)
    c = jnp.concatenate([cos, cos, rest1], axis=-1)
    sa = jnp.concatenate([-sin, z8, rest0], axis=-1)
    sb = jnp.concatenate([z8, sin, rest0], axis=-1)
    reps = LANES // HEAD_DIM
    return tuple(jnp.tile(t, (1, reps)) * scale for t in (c, sa, sb))


def _activations(z, b_merge, vng, vnb):
    ga = jax.nn.silu(z[:, OFF_GA:OFF_U])
    u = jax.nn.gelu(z[:, OFF_U:OFF_VB])
    vn = _layer_norm(jax.nn.gelu(z[:, OFF_VB:OFF_GB]), vng, vnb)
    gb = jax.nn.silu(z[:, OFF_GB:OFF_ML])
    gates = jax.nn.sigmoid(z[:, OFF_ML:] + b_merge)
    return ga, u, vn, gb, gates


def _merge_out(x, a_in, b_in, gates, wpa, wpb, wout, gpost):
    d = x.shape[-1]
    ba = jnp.dot(a_in.astype(BF16), wpa, preferred_element_type=F32)
    bb = jnp.dot(b_in.astype(BF16), wpb, preferred_element_type=F32)
    merged = gates[:, :d] * ba + gates[:, d:] * bb
    y = jnp.dot(merged.astype(BF16), wout, preferred_element_type=F32)
    return x + _rms(y, gpost)


def _const_spec(shape):
    nd = len(shape)
    return pl.BlockSpec(shape, lambda *_: (0,) * nd)


def _layer_spec(arr, layer, block_cols=None, col_block=0):
    rows, cols = arr.shape[1:]
    return pl.BlockSpec((None, rows, cols if block_cols is None else block_cols), lambda *_: (layer, 0, col_block))


def _qkv_kernel(seq, first_layer, x_ref, g_ref, w_ref, cq_ref, saq_ref, sbq_ref, ck_ref, sak_ref, sbk_ref, *refs):
    if not first_layer:
        refs = refs[N_GROUPS:]
    q1_ref, kv1_ref, q2_ref, kv2_ref, q3_ref, kv3_ref, p1_ref, p2_ref, p3_ref, dei_s = refs
    last_tile = pl.program_id(1) == pl.num_programs(1) - 1
    t = x_ref.shape[0]
    hb = _rms(x_ref[...], g_ref[...]).astype(BF16)
    q_refs = (q1_ref, q2_ref, q3_ref)
    kv_refs = (kv1_ref, kv2_ref, kv3_ref)
    p_refs = (p1_ref, p2_ref, p3_ref)
    cq, saq, sbq = cq_ref[...], saq_ref[...], sbq_ref[...]
    ck, sak, sbk = ck_ref[...], sak_ref[...], sbk_ref[...]
    nslab = GROUP_W // LANES

    slots = iter(range(dei_s.shape[0]))

    def split_by_residue(z, dst_ref, lane0, dil):
        slot = next(slots)
        for s in range(nslab):
            dei_s[slot, s] = z[:, s * LANES:(s + 1) * LANES]
        for r in range(dil):
            rows = [dei_s[slot, s, pl.ds(r, t // dil, stride=dil), :] for s in range(nslab)]
            dst_ref[r, :, lane0:lane0 + GROUP_W] = jnp.concatenate(rows, axis=-1).astype(BF16)

    def store_transposed(p_ref, idx, z):
        zt = z.T.reshape(HEADS_PER_GROUP, HEAD_DIM, z.shape[0])
        if first_layer:
            p_ref[0, idx] = zt
            if p_ref.shape[0] > 1:
                p_ref[1:, idx] = jnp.zeros((p_ref.shape[0] - 1,) + zt.shape, F32)
        else:
            p_ref[idx] = zt

    for g, (window, dil) in enumerate(ATTN_GROUPS):
        zq = jnp.dot(hb, w_ref[:, g * GROUP_W:(g + 1) * GROUP_W], preferred_element_type=F32)
        zq = _rope(zq, cq, saq, sbq)
        zk = jnp.dot(hb, w_ref[:, QKV_W + g * GROUP_W:QKV_W + (g + 1) * GROUP_W], preferred_element_type=F32)
        zk = _rope(zk, ck, sak, sbk)
        zv = jnp.dot(hb, w_ref[:, 2 * QKV_W + g * GROUP_W:2 * QKV_W + (g + 1) * GROUP_W],
                     preferred_element_type=F32)
        if dil == 1:
            q_refs[g][...] = zq.astype(BF16)
            kv_refs[g][:, 0:GROUP_W] = zk.astype(BF16)
            kv_refs[g][:, GROUP_W:2 * GROUP_W] = zv.astype(BF16)
        else:
            split_by_residue(zq, q_refs[g], 0, dil)
            split_by_residue(zk, kv_refs[g], 0, dil)
            split_by_residue(zv, kv_refs[g], GROUP_W, dil)
        keep = p_refs[g].shape[-1]
        if window >= seq:
            store_transposed(p_refs[g], 0, zk)
            store_transposed(p_refs[g], 1, zv)
        else:
            @pl.when(last_tile)
            def _(g=g, zk=zk, zv=zv, keep=keep):
                store_transposed(p_refs[g], 0, zk[t - keep:])
                store_transposed(p_refs[g], 1, zv[t - keep:])


def _qkv_call(layer, x, g_pre, w_in, tabs_q, tabs_k, accs):
    batch, seq, d = x.shape
    depth = w_in.shape[0]
    first_layer = accs is None
    t = QKV_TILE
    nt = seq // t
    tab_spec = pl.BlockSpec((t, LANES), lambda b, c: (c, 0))
    in_specs = [pl.BlockSpec((None, t, d), lambda b, c: (b, c, 0)), _layer_spec(g_pre, layer),
                _layer_spec(w_in, layer, 3 * QKV_W, 0)]
    out_specs, out_shape = [], []
    for window, dil in ATTN_GROUPS:
        assert t % (dil * 16) == 0
        if dil == 1:
            q_blk, kv_blk = (None, t, GROUP_W), (None, t, 2 * GROUP_W)
            q_shape, kv_shape = (batch, seq, GROUP_W), (batch, seq, 2 * GROUP_W)
            idx = lambda b, c: (b, c, 0)
        else:
            q_blk, kv_blk = (None, dil, t // dil, GROUP_W), (None, dil, t // dil, 2 * GROUP_W)
            q_shape, kv_shape = (batch, dil, seq // dil, GROUP_W), (batch, dil, seq // dil, 2 * GROUP_W)
            idx = lambda b, c: (b, 0, c, 0)
        out_specs += [pl.BlockSpec(q_blk, idx), pl.BlockSpec(kv_blk, idx)]
        out_shape += [jax.ShapeDtypeStruct(q_shape, BF16), jax.ShapeDtypeStruct(kv_shape, BF16)]
    for window, dil in ATTN_GROUPS:
        keep = min(window, seq)
        assert keep == seq or keep <= t, "a partial window must fit in the last tile"
        blk = min(keep, t)
        every_tile = keep == seq
        if first_layer:
            p_blk = (depth, None, 2, HEADS_PER_GROUP, HEAD_DIM, blk)
            p_idx = lambda b, c, every_tile=every_tile: (0, b, 0, 0, 0, c if every_tile else 0)
        else:
            p_blk = (None, None, 2, HEADS_PER_GROUP, HEAD_DIM, blk)
            p_idx = lambda b, c, every_tile=every_tile: (layer, b, 0, 0, 0, c if every_tile else 0)
        out_specs.append(pl.BlockSpec(p_blk, p_idx))
        out_shape.append(jax.ShapeDtypeStruct((depth, batch, 2, HEADS_PER_GROUP, HEAD_DIM, keep), F32))
    n_in = len(in_specs) + 6
    acc_specs = [] if first_layer else [pl.BlockSpec(memory_space=pl.ANY)] * N_GROUPS
    aliases = {} if first_layer else {n_in + g: 2 * N_GROUPS + g for g in range(N_GROUPS)}
    return pl.pallas_call(
        functools.partial(_qkv_kernel, seq, first_layer),
        grid=(batch, nt),
        in_specs=in_specs + [tab_spec] * 6 + acc_specs,
        out_specs=out_specs,
        out_shape=out_shape,
        input_output_aliases=aliases,
        scratch_shapes=[pltpu.VMEM((3 * sum(dil > 1 for _, dil in ATTN_GROUPS), GROUP_W // LANES, t, LANES), F32)],
        compiler_params=pltpu.CompilerParams(dimension_semantics=("arbitrary", "arbitrary"),
                                             vmem_limit_bytes=VMEM_LIMIT),
        name="prompt_qkv",
    )(x, g_pre, w_in, *tabs_q, *tabs_k, *(() if first_layer else accs))


def _attn_block(q, k, v, bias):
    nk = k.shape[0]
    hid = lax.shift_right_logical(lax.broadcasted_iota(jnp.int32, (BLK, GROUP_W), 1), HEAD_DIM.bit_length() - 1)
    qf = q.astype(F32)
    qs = jnp.concatenate([jnp.where(hid == h, qf, 0.0) for h in range(HEADS_PER_GROUP)], axis=0).astype(BF16)
    s = lax.dot_general(qs, k, (((1,), (1,)), ((), ())), preferred_element_type=F32)
    s = s.reshape(HEADS_PER_GROUP, BLK, nk) + bias[None]
    m = jnp.max(s, axis=-1, keepdims=True)
    p = jnp.exp(s - m)
    l = jnp.sum(p, axis=-1, keepdims=True)
    pv = jnp.dot(p.reshape(HEADS_PER_GROUP * BLK, nk).astype(BF16), v, preferred_element_type=F32)
    pv = pv.reshape(HEADS_PER_GROUP, BLK, GROUP_W)
    inv = 1.0 / l
    lse = m + jnp.log(l)
    o = jnp.zeros((BLK, GROUP_W), F32)
    ls = jnp.zeros((BLK, GROUP_W), F32)
    for h in range(HEADS_PER_GROUP):
        o = jnp.where(hid == h, pv[h] * inv[h], o)
        ls = jnp.where(hid == h, lse[h], ls)
    return o, ls


def _attn_kernel(q1_ref, q2_ref, q3_ref, kv1_ref, kv2_ref, kv3_ref, band_ref, causal_ref, out_ref,
                 o1_s, l1_s, o2_s, l2_s, o3_s, l3_s):
    step = pl.program_id(1)
    nslab = GROUP_W // LANES
    d2 = ATTN_GROUPS[1][1]
    d3 = ATTN_GROUPS[2][1]
    blocks_per_res2 = out_ref.shape[0] // BLK // d2

    def store(o_s, l_s, rows, o, ls):
        for s in range(nslab):
            o_s[s, rows, :] = o[:, s * LANES:(s + 1) * LANES]
            l_s[s, rows, :] = ls[:, s * LANES:(s + 1) * LANES]

    for jj in range(ATTN_STEP_BLOCKS):
        j = step * ATTN_STEP_BLOCKS + jj
        qrows = slice(jj * BLK, (jj + 1) * BLK)

        start = pl.multiple_of(jnp.maximum(j - 1, 0) * BLK, BLK)
        o, ls = _attn_block(q1_ref[qrows, :], kv1_ref[pl.ds(start, 2 * BLK), 0:GROUP_W],
                            kv1_ref[pl.ds(start, 2 * BLK), GROUP_W:2 * GROUP_W], band_ref[jnp.minimum(j, 1)])
        store(o1_s, l1_s, pl.ds(pl.multiple_of(j * BLK, BLK), BLK), o, ls)

        n = j % blocks_per_res2
        start = pl.multiple_of(jnp.maximum(n - 1, 0) * BLK, BLK)
        o, ls = _attn_block(q2_ref[qrows, :], kv2_ref[pl.ds(start, 2 * BLK), 0:GROUP_W],
                            kv2_ref[pl.ds(start, 2 * BLK), GROUP_W:2 * GROUP_W], band_ref[jnp.minimum(n, 1)])
        store(o2_s, l2_s, pl.ds(n * (BLK * d2) + j // blocks_per_res2, BLK, stride=d2), o, ls)

        o, ls = _attn_block(q3_ref[jj], kv3_ref[jj, :, 0:GROUP_W], kv3_ref[jj, :, GROUP_W:2 * GROUP_W], causal_ref[...])
        store(o3_s, l3_s, pl.ds(j, BLK, stride=d3), o, ls)

    @pl.when(step == pl.num_programs(1) - 1)
    def _():
        seq = out_ref.shape[0]
        rows_per_step = 256
        for s in range(nslab):
            def body(c, carry):
                rows = pl.ds(pl.multiple_of(c * rows_per_step, rows_per_step), rows_per_step)
                l1, l2, l3 = l1_s[s, rows, :], l2_s[s, rows, :], l3_s[s, rows, :]
                m = jnp.maximum(jnp.maximum(l1, l2), l3)
                e1, e2, e3 = jnp.exp(l1 - m), jnp.exp(l2 - m), jnp.exp(l3 - m)
                inv = 1.0 / (e1 + e2 + e3)
                out = (e1 * inv) * o1_s[s, rows, :] + (e2 * inv) * o2_s[s, rows, :] + (e3 * inv) * o3_s[s, rows, :]
                out_ref[rows, s * LANES:(s + 1) * LANES] = out
                return carry
            lax.fori_loop(0, seq // rows_per_step, body, 0)


def _attn_call(q1, q2, q3, kv1, kv2, kv3, band, causal, batch, seq):
    d2 = ATTN_GROUPS[1][1]
    d3 = ATTN_GROUPS[2][1]
    nblk = seq // BLK
    assert seq // d3 == BLK and nblk % d2 == 0 and nblk == d3
    nsb = ATTN_STEP_BLOCKS
    spr2 = nblk // d2 // nsb
    assert (nblk // d2) % nsb == 0
    scratch = [pltpu.VMEM((GROUP_W // LANES, seq, LANES), F32)] * 6
    return pl.pallas_call(
        _attn_kernel,
        grid=(batch, nblk // nsb),
        in_specs=[
            pl.BlockSpec((None, nsb * BLK, GROUP_W), lambda b, j: (b, j, 0)),
            pl.BlockSpec((None, None, nsb * BLK, GROUP_W), lambda b, j: (b, j // spr2, j % spr2, 0)),
            pl.BlockSpec((None, nsb, BLK, GROUP_W), lambda b, j: (b, j, 0, 0)),
            pl.BlockSpec((None, seq, 2 * GROUP_W), lambda b, j: (b, 0, 0)),
            pl.BlockSpec((None, None, seq // d2, 2 * GROUP_W), lambda b, j: (b, j // spr2, 0, 0)),
            pl.BlockSpec((None, nsb, BLK, 2 * GROUP_W), lambda b, j: (b, j, 0, 0)),
            _const_spec(band.shape),
            _const_spec(causal.shape),
        ],
        out_specs=pl.BlockSpec((None, seq, GROUP_W), lambda b, j: (b, 0, 0)),
        out_shape=jax.ShapeDtypeStruct((batch, seq, GROUP_W), F32),
        scratch_shapes=scratch,
        compiler_params=pltpu.CompilerParams(dimension_semantics=("arbitrary", "arbitrary"),
                                             vmem_limit_bytes=VMEM_LIMIT),
        name="prompt_attn",
    )(q1, q2, q3, kv1, kv2, kv3, band, causal)


def _attn_bias_tables():
    qi = jnp.arange(BLK)[:, None]
    kj = jnp.arange(2 * BLK)[None, :]
    general = (kj >= qi) & (kj <= qi + BLK)
    first = kj <= qi
    band = jnp.where(jnp.stack([first, general]), 0.0, NEG).astype(F32)
    causal = jnp.where(jnp.arange(BLK)[None, :] <= qi, 0.0, NEG).astype(F32)
    return band, causal


def _spatial_mix(ws, v):
    gd, two = GMLP_GROUP_DIM, 2 * LANES
    starts = [g * gd // LANES * LANES for g in range(GMLP_GROUPS)]
    assert all(s + two >= (g + 1) * gd for g, s in enumerate(starts))
    r = [jnp.dot(ws[g * BLK:(g + 1) * BLK], v[:, s:s + two], preferred_element_type=F32) for g, s in enumerate(starts)]
    lane = lax.broadcasted_iota(jnp.int32, (BLK, LANES), 1)
    slabs = []
    for lo in range(0, GMLP_W, LANES):
        owners = sorted({lo // gd, (lo + LANES - 1) // gd})
        pieces = [r[g][:, lo - starts[g]:lo - starts[g] + LANES] for g in owners]
        slabs.append(pieces[0] if len(owners) == 1 else jnp.where(lane < owners[1] * gd - lo, pieces[0], pieces[1]))
    return jnp.concatenate(slabs, axis=-1)


def _mlp_kernel(x_ref, attn_ref, gpre_ref, wlo_ref, whi_ref, bm_ref, vng_ref, vnb_ref, ws_ref, bsp_ref,
                wpa_ref, wpb_ref, wout_ref, gpost_ref, out_ref):
    x = x_ref[...]
    rows = x.shape[0]
    hb = _rms(x, gpre_ref[...]).astype(BF16)
    z = jnp.concatenate([jnp.dot(hb, wlo_ref[...], preferred_element_type=F32),
                         jnp.dot(hb, whi_ref[...], preferred_element_type=F32)], axis=-1)
    ga, u, vn, gb, gates = _activations(z, bm_ref[...], vng_ref[...], vnb_ref[...])
    wr = lax.broadcasted_iota(jnp.int32, ws_ref.shape, 0) & (BLK - 1)
    wc = lax.broadcasted_iota(jnp.int32, ws_ref.shape, 1)
    ws = jnp.where(wc <= wr, ws_ref[...], 0.0).astype(BF16)
    vnb16 = vn.astype(BF16)
    mix = jnp.concatenate([_spatial_mix(ws, vnb16[c * BLK:(c + 1) * BLK]) + bsp_ref[...]
                           for c in range(rows // BLK)], axis=0)
    out_ref[...] = _merge_out(x, attn_ref[...] * ga, (u * mix) * gb, gates,
                              wpa_ref[...], wpb_ref[...], wout_ref[...], gpost_ref[...])


def _mlp_call(layer, x, attn, g_pre, w_in, b_merge, vng, vnb, ws_stack, bs_full, wpa, wpb, wout, g_post):
    n, d = x.shape
    t = MLP_TILE
    row = lambda i: (i, 0)
    rest_w = w_in.shape[-1] - 3 * QKV_W
    assert rest_w == 2 * 3 * QKV_W, "rest columns are addressed as column blocks 1 and 2 of width 3 * QKV_W"
    w_specs = [_layer_spec(w_in, layer, 3 * QKV_W, 1), _layer_spec(w_in, layer, 3 * QKV_W, 2)]
    stacks = (b_merge, vng, vnb, ws_stack, bs_full, wpa, wpb, wout, g_post)
    return pl.pallas_call(
        _mlp_kernel,
        grid=(n // t,),
        in_specs=[pl.BlockSpec((t, d), row), pl.BlockSpec((t, GROUP_W), row), _layer_spec(g_pre, layer)] + w_specs
        + [_layer_spec(s, layer) for s in stacks],
        out_specs=pl.BlockSpec((t, d), row),
        out_shape=jax.ShapeDtypeStruct((n, d), F32),
        compiler_params=pltpu.CompilerParams(dimension_semantics=("arbitrary",), vmem_limit_bytes=VMEM_LIMIT),
        name="prompt_mlp",
    )(x, attn, g_pre, w_in, w_in, *stacks)


def _sample_in_kernel(x_ref, g_ref, w_ref, cq_ref, saq_ref, sbq_ref, ck_ref, sak_ref, sbk_ref,
                      bm_ref, vng_ref, vnb_ref,
                      kv1_ref, kv2_ref, kv3_ref, qt_ref, kvt1_ref, kvt2_ref, kvt3_ref,
                      ga_ref, u_ref, vn_ref, gb_ref, gates_ref):
    hb = _rms(x_ref[...], g_ref[...]).astype(BF16)
    z = jnp.dot(hb, w_ref[...], preferred_element_type=F32)
    bt = qt_ref.shape[-1]

    def store_tiles(dst_ref, rows):
        cols = rows.T
        for i in range(dst_ref.shape[0]):
            dst_ref[i] = cols[:, i * bt:(i + 1) * bt]

    qs = []
    for g, (kv_ref, kvt_ref) in enumerate(((kv1_ref, kvt1_ref), (kv2_ref, kvt2_ref), (kv3_ref, kvt3_ref))):
        lo = g * GROUP_W
        qs.append(_rope(z[:, lo:lo + GROUP_W], cq_ref[...], saq_ref[...], sbq_ref[...]))
        k = _rope(z[:, QKV_W + lo:QKV_W + lo + GROUP_W], ck_ref[...], sak_ref[...], sbk_ref[...])
        kv = jnp.concatenate([k, z[:, 2 * QKV_W + lo:2 * QKV_W + lo + GROUP_W]], axis=-1)
        kv_ref[...] = kv
        store_tiles(kvt_ref, kv)
    store_tiles(qt_ref, jnp.concatenate(qs, axis=-1))
    ga, u, vn, gb, gates = _activations(z[:, 3 * QKV_W:], bm_ref[...], vng_ref[...], vnb_ref[...])
    ga_ref[...] = ga
    u_ref[...] = u
    vn_ref[...] = vn
    gb_ref[...] = gb
    gates_ref[...] = gates


def _sample_in_call(layer, x, g_pre, w_in, tabs_q, tabs_k, b_merge, vng, vnb):
    n, d = x.shape
    bt = SAMPLE_BT
    shapes = ([(n, 2 * GROUP_W)] * 3 + [(n // bt, QKV_W, bt)] + [(n // bt, 2 * GROUP_W, bt)] * 3
              + [(n, w) for w in (GROUP_W, GMLP_W, GMLP_W, GMLP_W, 2 * d)])
    tabs = (*tabs_q, *tabs_k)
    return pl.pallas_call(
        _sample_in_kernel,
        grid=(1,),
        in_specs=[_const_spec(x.shape), _layer_spec(g_pre, layer), _layer_spec(w_in, layer)]
        + [_const_spec(t.shape) for t in tabs] + [_layer_spec(s, layer) for s in (b_merge, vng, vnb)],
        out_specs=[_const_spec(s) for s in shapes],
        out_shape=[jax.ShapeDtypeStruct(s, F32) for s in shapes],
        compiler_params=pltpu.CompilerParams(dimension_semantics=("arbitrary",), vmem_limit_bytes=VMEM_LIMIT),
        name="sample_in",
    )(x, g_pre, w_in, *tabs, b_merge, vng, vnb)


def _sample_attn_kernel(q_ref, kvn1_ref, kvn2_ref, kvn3_ref, c1_ref, c2_ref, c3_ref, out_ref):
    def column(ref, lo, b):
        return ref[lo:lo + GROUP_W, b:b + 1].reshape(HEADS_PER_GROUP, HEAD_DIM, 1)

    for b in range(q_ref.shape[-1]):
        outs, lses = [], []
        for g, (kvn_ref, c_ref) in enumerate(((kvn1_ref, c1_ref), (kvn2_ref, c2_ref), (kvn3_ref, c3_ref))):
            dil = ATTN_GROUPS[g][1]
            q = column(q_ref, g * GROUP_W, b)
            kc, vc = c_ref[b, 0], c_ref[b, 1]
            kn, vn = column(kvn_ref, 0, b), column(kvn_ref, GROUP_W, b)
            rows = kc.shape[-1]
            sc = jnp.sum(kc * q, axis=1, keepdims=True)
            r = lax.broadcasted_iota(jnp.int32, (1, 1, rows), 2)
            sc = jnp.where((r & (dil - 1)) == 0, sc, NEG)
            ss = jnp.sum(kn * q, axis=1, keepdims=True)
            m = jnp.maximum(jnp.max(sc, axis=-1, keepdims=True), ss)
            pc = jnp.exp(sc - m)
            ps = jnp.exp(ss - m)
            l = jnp.sum(pc, axis=-1, keepdims=True) + ps
            o = jnp.sum(vc * pc, axis=-1, keepdims=True) + ps * vn
            outs.append(o / l)
            lses.append(m + jnp.log(l))
        m = jnp.maximum(jnp.maximum(lses[0], lses[1]), lses[2])
        es = [jnp.exp(l - m) for l in lses]
        inv = 1.0 / (es[0] + es[1] + es[2])
        merged = (es[0] * inv) * outs[0] + (es[1] * inv) * outs[1] + (es[2] * inv) * outs[2]
        out_ref[:, b:b + 1] = merged.reshape(GROUP_W, 1)


def _sample_attn_call(layer, q, kvn, caches):
    tiles, _, bt = q.shape
    n = tiles * bt
    tile = lambda i: (i, 0, 0)
    cache_specs = []
    cache_views = []
    for (window, dil), c in zip(ATTN_GROUPS, caches):
        depth, nb, rows = c.shape[:3]
        assert rows == window and dil & (dil - 1) == 0, "cache must hold exactly the window"
        cache_views.append(jnp.transpose(c, (0, 1, 3, 4, 5, 2)))
        cache_specs.append(pl.BlockSpec((None, bt, 2, HEADS_PER_GROUP, HEAD_DIM, rows),
                                        lambda i: (layer, i, 0, 0, 0, 0)))
    return pl.pallas_call(
        _sample_attn_kernel,
        grid=(n // bt,),
        in_specs=[pl.BlockSpec((None, QKV_W, bt), tile)] + [pl.BlockSpec((None, 2 * GROUP_W, bt), tile)] * 3 + cache_specs,
        out_specs=pl.BlockSpec((None, GROUP_W, bt), tile),
        out_shape=jax.ShapeDtypeStruct((tiles, GROUP_W, bt), F32),
        compiler_params=pltpu.CompilerParams(dimension_semantics=("arbitrary",), vmem_limit_bytes=VMEM_LIMIT),
        name="sample_attn",
    )(q, *kvn, *cache_views)


def _sample_out_kernel(x_ref, attn_ref, ga_ref, u_ref, vn_ref, gb_ref, gates_ref, ws0_ref, bs0_ref,
                       wpa_ref, wpb_ref, wout_ref, gpost_ref, out_ref):
    mix = vn_ref[...] * ws0_ref[...] + bs0_ref[...]
    attn = jnp.concatenate([attn_ref[i] for i in range(attn_ref.shape[0])], axis=-1).T
    out_ref[...] = _merge_out(x_ref[...], attn * ga_ref[...], (u_ref[...] * mix) * gb_ref[...],
                              gates_ref[...], wpa_ref[...], wpb_ref[...], wout_ref[...], gpost_ref[...])


def _sample_out_call(layer, x, attn, ga, u, vn, gb, gates, ws0, bs0, wpa, wpb, wout, g_post):
    acts = (x, attn, ga, u, vn, gb, gates)
    stacks = (ws0, bs0, wpa, wpb, wout, g_post)
    return pl.pallas_call(
        _sample_out_kernel,
        grid=(1,),
        in_specs=[_const_spec(a.shape) for a in acts] + [_layer_spec(s, layer) for s in stacks],
        out_specs=_const_spec(x.shape),
        out_shape=jax.ShapeDtypeStruct(x.shape, F32),
        compiler_params=pltpu.CompilerParams(dimension_semantics=("arbitrary",), vmem_limit_bytes=VMEM_LIMIT),
        name="sample_out",
    )(*acts, *stacks)


def kernel(x_prompt, x_sample, cache_kv_w128, cache_kv_w512, cache_kv_w2048, norm_pre, w_in, b_merge, v_norm_g, v_norm_b, w_spatial, b_spatial, w_proj_a, w_proj_b, w_out, norm_post):
    batch, seq, d = x_prompt.shape
    nb, dec_seq, _ = x_sample.shape
    depth = w_in.shape[0]
    assert dec_seq == 1, "the sample group decodes one position per step"
    assert seq % QKV_TILE == 0 and (batch * seq) % MLP_TILE == 0 and nb % SAMPLE_BT == 0
    caches = (cache_kv_w128, cache_kv_w512, cache_kv_w2048)

    w_in_b = w_in.astype(BF16)
    wpa_b, wpb_b, wout_b = w_proj_a.astype(BF16), w_proj_b.astype(BF16), w_out.astype(BF16)
    ws_stack = w_spatial.reshape(depth, GMLP_GROUPS * BLK, BLK)
    chan_group = jnp.arange(GMLP_W) // GMLP_GROUP_DIM
    bs_full = jnp.swapaxes(b_spatial, 1, 2)[:, :, chan_group]
    ws0 = w_spatial[:, :, 0, 0][:, chan_group][:, None, :]
    bs0 = b_spatial[:, :, 0][:, chan_group][:, None, :]
    row2 = lambda a: a[:, None, :]
    g_pre, g_post, bm, vng, vnb = row2(norm_pre), row2(norm_post), row2(b_merge), row2(v_norm_g), row2(v_norm_b)

    scale = HEAD_DIM ** -0.5
    pos_p = jnp.arange(seq, dtype=jnp.int32)
    pos_s = PAST_LEN + jnp.arange(dec_seq, dtype=jnp.int32)
    tq_p, tk_p = _rope_tables(pos_p, scale), _rope_tables(pos_p, 1.0)
    tq_s, tk_s = _rope_tables(pos_s, scale), _rope_tables(pos_s, 1.0)
    band, causal = _attn_bias_tables()

    xp = x_prompt
    xs = x_sample.reshape(nb, d)
    kv_p = None
    kv_s = [[] for _ in ATTN_GROUPS]
    v_s = []
    for l in range(depth):
        q1, kv1, q2, kv2, q3, kv3, *kv_p = _qkv_call(l, xp, g_pre, w_in_b, tq_p, tk_p, kv_p)
        attn = _attn_call(q1, q2, q3, kv1, kv2, kv3, band, causal, batch, seq)
        xp = _mlp_call(l, xp.reshape(batch * seq, d), attn.reshape(batch * seq, GROUP_W), g_pre, w_in_b,
                       bm, vng, vnb, ws_stack, bs_full, wpa_b, wpb_b, wout_b, g_post)
        xp = xp.reshape(batch, seq, d)
        kn1, kn2, kn3, qt, kt1, kt2, kt3, ga, u, vn, gb, gates = _sample_in_call(l, xs, g_pre, w_in_b, tq_s, tk_s,
                                                                                 bm, vng, vnb)
        attn_s = _sample_attn_call(l, qt, (kt1, kt2, kt3), caches)
        xs = _sample_out_call(l, xs, attn_s, ga, u, vn, gb, gates, ws0, bs0, wpa_b, wpb_b, wout_b, g_post)
        for g, kn in enumerate((kn1, kn2, kn3)):
            kv_s[g].append(kn.reshape(nb, dec_seq, 2, HEADS_PER_GROUP, HEAD_DIM))
        v_s.append(vn.reshape(nb, dec_seq, GMLP_W))

    new_kv_p = [jnp.transpose(p, (0, 1, 5, 2, 3, 4)) for p in kv_p]
    return (xp, xs.reshape(nb, dec_seq, d), new_kv_p[0], new_kv_p[1], new_kv_p[2],
            jnp.stack(kv_s[0]), jnp.stack(kv_s[1]), jnp.stack(kv_s[2]), jnp.stack(v_s))
```

```python
import functools

import jax
import jax.numpy as jnp
from jax import lax
from jax.experimental import pallas as pl
from jax.experimental.pallas import tpu as pltpu

F32 = jnp.float32
BF16 = jnp.bfloat16

PAST_LEN = 8192
HEAD_DIM = 64
HEADS_PER_GROUP = 4
GROUP_W = HEADS_PER_GROUP * HEAD_DIM
ATTN_GROUPS = ((128, 1), (512, 4), (2048, 16))
N_GROUPS = len(ATTN_GROUPS)
QKV_W = N_GROUPS * GROUP_W
ROT_DIM = HEAD_DIM // 4
ROT_HALF = ROT_DIM // 2
ROPE_THETA = 500000.0
BLK = 128
GMLP_GROUPS = 4
GMLP_GROUP_DIM = 192
GMLP_W = GMLP_GROUPS * GMLP_GROUP_DIM
EPS = 1e-6
LANES = 128
NEG = -1e30

OFF_GA = 0
OFF_U = OFF_GA + GROUP_W
OFF_VB = OFF_U + GMLP_W
OFF_GB = OFF_VB + GMLP_W
OFF_ML = OFF_GB + GMLP_W

QKV_TILE = 512
MLP_TILE = 512
SAMPLE_BT = 2
ATTN_STEP_BLOCKS = 4
VMEM_LIMIT = 56 * 1024 * 1024


def _rms(x, g):
    return (x * lax.rsqrt(jnp.mean(x * x, axis=-1, keepdims=True) + EPS)) * g


def _layer_norm(x, g, b):
    xc = x - jnp.mean(x, axis=-1, keepdims=True)
    return xc * lax.rsqrt(jnp.mean(xc * xc, axis=-1, keepdims=True) + EPS) * g + b


def _rope(z, c, sa, sb):
    outs = []
    for s in range(GROUP_W // LANES):
        zs = z[:, s * LANES:(s + 1) * LANES]
        outs.append(zs * c + pltpu.roll(zs, LANES - ROT_HALF, 1) * sa + pltpu.roll(zs, ROT_HALF, 1) * sb)
    return jnp.concatenate(outs, axis=-1)


def _rope_tables(pos, scale):
    inv_freq = jnp.power(ROPE_THETA, -jnp.arange(ROT_HALF, dtype=F32) / ROT_HALF)
    ang = pos.astype(F32)[:, None] * inv_freq[None, :]
    cos, sin = jnp.cos(ang), jnp.sin(ang)
    n = pos.shape[0]
    rest1 = jnp.ones((n, HEAD_DIM - ROT_DIM), F32)
    rest0 = jnp.zeros((n, HEAD_DIM - ROT_DIM), F32)
    z8 = jnp.zeros((n, ROT_HALF), F32)
    c = jnp.concatenate([cos, cos, rest1], axis=-1)
    sa = jnp.concatenate([-sin, z8, rest0], axis=-1)
    sb = jnp.concatenate([z8, sin, rest0], axis=-1)
    reps = LANES // HEAD_DIM
    return tuple(jnp.tile(t, (1, reps)) * scale for t in (c, sa, sb))


def _activations(z, b_merge, vng, vnb):
    ga = jax.nn.silu(z[:, OFF_GA:OFF_U])
    u = jax.nn.gelu(z[:, OFF_U:OFF_VB])
    vn = _layer_norm(jax.nn.gelu(z[:, OFF_VB:OFF_GB]), vng, vnb)
    gb = jax.nn.silu(z[:, OFF_GB:OFF_ML])
    gates = jax.nn.sigmoid(z[:, OFF_ML:] + b_merge)
    return ga, u, vn, gb, gates


def _merge_out(x, a_in, b_in, gates, wpa, wpb, wout, gpost):
    d = x.shape[-1]
    ba = jnp.dot(a_in.astype(BF16), wpa, preferred_element_type=F32)
    bb = jnp.dot(b_in.astype(BF16), wpb, preferred_element_type=F32)
    merged = gates[:, :d] * ba + gates[:, d:] * bb
    y = jnp.dot(merged.astype(BF16), wout, preferred_element_type=F32)
    return x + _rms(y, gpost)


def _const_spec(shape):
    nd = len(shape)
    return pl.BlockSpec(shape, lambda *_: (0,) * nd)


def _layer_spec(arr, layer, block_cols=None, col_block=0):
    rows, cols = arr.shape[1:]
    return pl.BlockSpec((None, rows, cols if block_cols is None else block_cols), lambda *_: (layer, 0, col_block))


def _qkv_kernel(seq, first_layer, x_ref, g_ref, w_ref, cq_ref, saq_ref, sbq_ref, ck_ref, sak_ref, sbk_ref, *refs):
    if not first_layer:
        refs = refs[N_GROUPS:]
    q1_ref, kv1_ref, q2_ref, kv2_ref, q3_ref, kv3_ref, p1_ref, p2_ref, p3_ref, za_s, zb_s, dei_s = refs
    step = pl.program_id(0)
    t = x_ref.shape[0]
    tiles_per_seq = seq // t
    last_tile = (jnp.maximum(step - 1, 0) % tiles_per_seq) == tiles_per_seq - 1
    q_refs = (q1_ref, q2_ref, q3_ref)
    kv_refs = (kv1_ref, kv2_ref, kv3_ref)
    p_refs = (p1_ref, p2_ref, p3_ref)
    nslab = GROUP_W // LANES

    @pl.when(step == 0)
    def _():
        zb_s[...] = jnp.zeros(zb_s.shape, F32)

    def store_transposed(p_ref, idx, z):
        zt = z.T.reshape(HEADS_PER_GROUP, HEAD_DIM, z.shape[0])
        if first_layer:
            p_ref[0, idx] = zt
            if p_ref.shape[0] > 1:
                p_ref[1:, idx] = jnp.zeros((p_ref.shape[0] - 1,) + zt.shape, F32)
        else:
            p_ref[idx] = zt

    def rope_k(z, g, rows=slice(None)):
        return _rope(z[rows, QKV_W + g * GROUP_W:QKV_W + (g + 1) * GROUP_W], ck_ref[rows, :], sak_ref[rows, :], sbk_ref[rows, :])

    def body(z_new, z):
        hb = _rms(x_ref[...], g_ref[...]).astype(BF16)
        z_new[...] = jnp.dot(hb, w_ref[...], preferred_element_type=F32)
        slots = iter(range(dei_s.shape[0]))

        def split_by_residue(v, dst_ref, lane0, dil):
            slot = next(slots)
            for s in range(nslab):
                dei_s[slot, s] = v[:, s * LANES:(s + 1) * LANES]
            for r in range(dil):
                rows = [dei_s[slot, s, pl.ds(r, t // dil, stride=dil), :] for s in range(nslab)]
                dst_ref[r, :, lane0:lane0 + GROUP_W] = jnp.concatenate(rows, axis=-1).astype(BF16)

        for g, (window, dil) in enumerate(ATTN_GROUPS):
            zq = _rope(z[:, g * GROUP_W:(g + 1) * GROUP_W], cq_ref[...], saq_ref[...], sbq_ref[...])
            zk = rope_k(z, g)
            zv = z[:, 2 * QKV_W + g * GROUP_W:2 * QKV_W + (g + 1) * GROUP_W]
            if dil == 1:
                q_refs[g][...] = zq.astype(BF16)
                kv_refs[g][:, 0:GROUP_W] = zk.astype(BF16)
                kv_refs[g][:, GROUP_W:2 * GROUP_W] = zv.astype(BF16)
            else:
                split_by_residue(zq, q_refs[g], 0, dil)
                split_by_residue(zk, kv_refs[g], 0, dil)
                split_by_residue(zv, kv_refs[g], GROUP_W, dil)
            if window >= seq:
                store_transposed(p_refs[g], 0, zk)
                store_transposed(p_refs[g], 1, zv)

        @pl.when(last_tile)
        def _():
            for g, (window, dil) in enumerate(ATTN_GROUPS):
                if window < seq:
                    tail = slice(t - p_refs[g].shape[-1], t)
                    store_transposed(p_refs[g], 0, rope_k(z, g, tail))
                    store_transposed(p_refs[g], 1, z[tail, 2 * QKV_W + g * GROUP_W:2 * QKV_W + (g + 1) * GROUP_W])

    @pl.when(step % 2 == 0)
    def _():
        body(za_s, zb_s)

    @pl.when(step % 2 == 1)
    def _():
        body(zb_s, za_s)


def _qkv_call(layer, x, g_pre, w_in, tabs_q, tabs_k, accs):
    batch, seq, d = x.shape
    depth = w_in.shape[0]
    first_layer = accs is None
    t = QKV_TILE
    nt = seq // t
    n_tiles = batch * nt
    proj = lambda s: jnp.minimum(s, n_tiles - 1)
    fin = lambda s: jnp.maximum(s - 1, 0)
    tab_spec = pl.BlockSpec((t, LANES), lambda s: (fin(s) % nt, 0))
    in_specs = [pl.BlockSpec((None, t, d), lambda s: (proj(s) // nt, proj(s) % nt, 0)), _layer_spec(g_pre, layer),
                _layer_spec(w_in, layer, 3 * QKV_W, 0)]
    out_specs, out_shape = [], []
    for window, dil in ATTN_GROUPS:
        assert t % (dil * 16) == 0
        if dil == 1:
            q_blk, kv_blk = (None, t, GROUP_W), (None, t, 2 * GROUP_W)
            q_shape, kv_shape = (batch, seq, GROUP_W), (batch, seq, 2 * GROUP_W)
            idx = lambda s: (fin(s) // nt, fin(s) % nt, 0)
        else:
            q_blk, kv_blk = (None, dil, t // dil, GROUP_W), (None, dil, t // dil, 2 * GROUP_W)
            q_shape, kv_shape = (batch, dil, seq // dil, GROUP_W), (batch, dil, seq // dil, 2 * GROUP_W)
            idx = lambda s: (fin(s) // nt, 0, fin(s) % nt, 0)
        out_specs += [pl.BlockSpec(q_blk, idx), pl.BlockSpec(kv_blk, idx)]
        out_shape += [jax.ShapeDtypeStruct(q_shape, BF16), jax.ShapeDtypeStruct(kv_shape, BF16)]
    for window, dil in ATTN_GROUPS:
        keep = min(window, seq)
        assert keep == seq or keep <= t, "a partial window must fit in the last tile"
        blk = min(keep, t)
        every_tile = keep == seq
        lead = 0 if first_layer else layer
        p_blk = (depth if first_layer else None, None, 2, HEADS_PER_GROUP, HEAD_DIM, blk)
        p_idx = lambda s, every_tile=every_tile: (lead, fin(s) // nt, 0, 0, 0, fin(s) % nt if every_tile else 0)
        out_specs.append(pl.BlockSpec(p_blk, p_idx))
        out_shape.append(jax.ShapeDtypeStruct((depth, batch, 2, HEADS_PER_GROUP, HEAD_DIM, keep), F32))
    n_in = len(in_specs) + 6
    acc_specs = [] if first_layer else [pl.BlockSpec(memory_space=pl.ANY)] * N_GROUPS
    aliases = {} if first_layer else {n_in + g: 2 * N_GROUPS + g for g in range(N_GROUPS)}
    n_split = 3 * sum(dil > 1 for _, dil in ATTN_GROUPS)
    return pl.pallas_call(
        functools.partial(_qkv_kernel, seq, first_layer),
        grid=(n_tiles + 1,),
        in_specs=in_specs + [tab_spec] * 6 + acc_specs,
        out_specs=out_specs,
        out_shape=out_shape,
        input_output_aliases=aliases,
        scratch_shapes=[pltpu.VMEM((t, 3 * QKV_W), F32), pltpu.VMEM((t, 3 * QKV_W), F32),
                        pltpu.VMEM((n_split, GROUP_W // LANES, t, LANES), F32)],
        compiler_params=pltpu.CompilerParams(dimension_semantics=("arbitrary",), vmem_limit_bytes=VMEM_LIMIT),
        name="prompt_qkv",
    )(x, g_pre, w_in, *tabs_q, *tabs_k, *(() if first_layer else accs))


def _attn_block(q, k, v, bias):
    nk = k.shape[0]
    hid = lax.shift_right_logical(lax.broadcasted_iota(jnp.int32, (BLK, GROUP_W), 1), HEAD_DIM.bit_length() - 1)
    qf = q.astype(F32)
    qs = jnp.concatenate([jnp.where(hid == h, qf, 0.0) for h in range(HEADS_PER_GROUP)], axis=0).astype(BF16)
    s = lax.dot_general(qs, k, (((1,), (1,)), ((), ())), preferred_element_type=F32)
    s = s.reshape(HEADS_PER_GROUP, BLK, nk) + bias[None]
    m = jnp.max(s, axis=-1, keepdims=True)
    p = jnp.exp(s - m)
    l = jnp.sum(p, axis=-1, keepdims=True)
    pv = jnp.dot(p.reshape(HEADS_PER_GROUP * BLK, nk).astype(BF16), v, preferred_element_type=F32)
    pv = pv.reshape(HEADS_PER_GROUP, BLK, GROUP_W)
    inv = 1.0 / l
    lse = m + jnp.log(l)
    o = jnp.zeros((BLK, GROUP_W), F32)
    ls = jnp.zeros((BLK, GROUP_W), F32)
    for h in range(HEADS_PER_GROUP):
        o = jnp.where(hid == h, pv[h] * inv[h], o)
        ls = jnp.where(hid == h, lse[h], ls)
    return o, ls


def _attn_kernel(q1_ref, q2_ref, q3_ref, kv1_ref, kv2_ref, kv3_ref, band_ref, causal_ref, out_ref,
                 o1_s, l1_s, o2_s, l2_s, o3_s, l3_s):
    step = pl.program_id(1)
    nslab = GROUP_W // LANES
    d2 = ATTN_GROUPS[1][1]
    d3 = ATTN_GROUPS[2][1]
    blocks_per_res2 = out_ref.shape[0] // BLK // d2

    def store(o_s, l_s, rows, o, ls):
        for s in range(nslab):
            o_s[s, rows, :] = o[:, s * LANES:(s + 1) * LANES]
            l_s[s, rows, :] = ls[:, s * LANES:(s + 1) * LANES]

    for jj in range(ATTN_STEP_BLOCKS):
        j = step * ATTN_STEP_BLOCKS + jj
        qrows = slice(jj * BLK, (jj + 1) * BLK)

        start = pl.multiple_of(jnp.maximum(j - 1, 0) * BLK, BLK)
        o, ls = _attn_block(q1_ref[qrows, :], kv1_ref[pl.ds(start, 2 * BLK), 0:GROUP_W],
                            kv1_ref[pl.ds(start, 2 * BLK), GROUP_W:2 * GROUP_W], band_ref[jnp.minimum(j, 1)])
        store(o1_s, l1_s, pl.ds(pl.multiple_of(j * BLK, BLK), BLK), o, ls)

        n = j % blocks_per_res2
        start = pl.multiple_of(jnp.maximum(n - 1, 0) * BLK, BLK)
        o, ls = _attn_block(q2_ref[qrows, :], kv2_ref[pl.ds(start, 2 * BLK), 0:GROUP_W],
                            kv2_ref[pl.ds(start, 2 * BLK), GROUP_W:2 * GROUP_W], band_ref[jnp.minimum(n, 1)])
        store(o2_s, l2_s, pl.ds(n * (BLK * d2) + j // blocks_per_res2, BLK, stride=d2), o, ls)

        o, ls = _attn_block(q3_ref[jj], kv3_ref[jj, :, 0:GROUP_W], kv3_ref[jj, :, GROUP_W:2 * GROUP_W], causal_ref[...])
        store(o3_s, l3_s, pl.ds(j, BLK, stride=d3), o, ls)

    @pl.when(step == pl.num_programs(1) - 1)
    def _():
        seq = out_ref.shape[0]
        rows_per_step = 256
        for s in range(nslab):
            def body(c, carry):
                rows = pl.ds(pl.multiple_of(c * rows_per_step, rows_per_step), rows_per_step)
                l1, l2, l3 = l1_s[s, rows, :], l2_s[s, rows, :], l3_s[s, rows, :]
                m = jnp.maximum(jnp.maximum(l1, l2), l3)
                e1, e2, e3 = jnp.exp(l1 - m), jnp.exp(l2 - m), jnp.exp(l3 - m)
                inv = 1.0 / (e1 + e2 + e3)
                out = (e1 * inv) * o1_s[s, rows, :] + (e2 * inv) * o2_s[s, rows, :] + (e3 * inv) * o3_s[s, rows, :]
                out_ref[rows, s * LANES:(s + 1) * LANES] = out
                return carry
            lax.fori_loop(0, seq // rows_per_step, body, 0)


def _attn_call(q1, q2, q3, kv1, kv2, kv3, band, causal, batch, seq):
    d2 = ATTN_GROUPS[1][1]
    d3 = ATTN_GROUPS[2][1]
    nblk = seq // BLK
    assert seq // d3 == BLK and nblk % d2 == 0 and nblk == d3
    nsb = ATTN_STEP_BLOCKS
    spr2 = nblk // d2 // nsb
    assert (nblk // d2) % nsb == 0
    scratch = [pltpu.VMEM((GROUP_W // LANES, seq, LANES), F32)] * 6
    return pl.pallas_call(
        _attn_kernel,
        grid=(batch, nblk // nsb),
        in_specs=[
            pl.BlockSpec((None, nsb * BLK, GROUP_W), lambda b, j: (b, j, 0)),
            pl.BlockSpec((None, None, nsb * BLK, GROUP_W), lambda b, j: (b, j // spr2, j % spr2, 0)),
            pl.BlockSpec((None, nsb, BLK, GROUP_W), lambda b, j: (b, j, 0, 0)),
            pl.BlockSpec((None, seq, 2 * GROUP_W), lambda b, j: (b, 0, 0)),
            pl.BlockSpec((None, None, seq // d2, 2 * GROUP_W), lambda b, j: (b, j // spr2, 0, 0)),
            pl.BlockSpec((None, nsb, BLK, 2 * GROUP_W), lambda b, j: (b, j, 0, 0)),
            _const_spec(band.shape),
            _const_spec(causal.shape),
        ],
        out_specs=pl.BlockSpec((None, seq, GROUP_W), lambda b, j: (b, 0, 0)),
        out_shape=jax.ShapeDtypeStruct((batch, seq, GROUP_W), F32),
        scratch_shapes=scratch,
        compiler_params=pltpu.CompilerParams(dimension_semantics=("arbitrary", "arbitrary"),
                                             vmem_limit_bytes=VMEM_LIMIT),
        name="prompt_attn",
    )(q1, q2, q3, kv1, kv2, kv3, band, causal)


def _attn_bias_tables():
    qi = jnp.arange(BLK)[:, None]
    kj = jnp.arange(2 * BLK)[None, :]
    general = (kj >= qi) & (kj <= qi + BLK)
    first = kj <= qi
    band = jnp.where(jnp.stack([first, general]), 0.0, NEG).astype(F32)
    causal = jnp.where(jnp.arange(BLK)[None, :] <= qi, 0.0, NEG).astype(F32)
    return band, causal


def _spatial_mix(ws, v):
    gd, two = GMLP_GROUP_DIM, 2 * LANES
    starts = [g * gd // LANES * LANES for g in range(GMLP_GROUPS)]
    assert all(s + two >= (g + 1) * gd for g, s in enumerate(starts))
    r = [jnp.dot(ws[g * BLK:(g + 1) * BLK], v[:, s:s + two], preferred_element_type=F32) for g, s in enumerate(starts)]
    lane = lax.broadcasted_iota(jnp.int32, (BLK, LANES), 1)
    slabs = []
    for lo in range(0, GMLP_W, LANES):
        owners = sorted({lo // gd, (lo + LANES - 1) // gd})
        pieces = [r[g][:, lo - starts[g]:lo - starts[g] + LANES] for g in owners]
        slabs.append(pieces[0] if len(owners) == 1 else jnp.where(lane < owners[1] * gd - lo, pieces[0], pieces[1]))
    return jnp.concatenate(slabs, axis=-1)


def _mlp_kernel(x_ref, attn_ref, gpre_ref, wlo_ref, whi_ref, bm_ref, vng_ref, vnb_ref, ws_ref, bsp_ref,
                wpa_ref, wpb_ref, wout_ref, gpost_ref, out_ref):
    x = x_ref[...]
    rows = x.shape[0]
    hb = _rms(x, gpre_ref[...]).astype(BF16)
    z = jnp.concatenate([jnp.dot(hb, wlo_ref[...], preferred_element_type=F32),
                         jnp.dot(hb, whi_ref[...], preferred_element_type=F32)], axis=-1)
    ga, u, vn, gb, gates = _activations(z, bm_ref[...], vng_ref[...], vnb_ref[...])
    wr = lax.broadcasted_iota(jnp.int32, ws_ref.shape, 0) & (BLK - 1)
    wc = lax.broadcasted_iota(jnp.int32, ws_ref.shape, 1)
    ws = jnp.where(wc <= wr, ws_ref[...], 0.0).astype(BF16)
    vnb16 = vn.astype(BF16)
    mix = jnp.concatenate([_spatial_mix(ws, vnb16[c * BLK:(c + 1) * BLK]) + bsp_ref[...]
                           for c in range(rows // BLK)], axis=0)
    out_ref[...] = _merge_out(x, attn_ref[...] * ga, (u * mix) * gb, gates,
                              wpa_ref[...], wpb_ref[...], wout_ref[...], gpost_ref[...])


def _mlp_call(layer, x, attn, g_pre, w_in, b_merge, vng, vnb, ws_stack, bs_full, wpa, wpb, wout, g_post):
    n, d = x.shape
    t = MLP_TILE
    row = lambda i: (i, 0)
    rest_w = w_in.shape[-1] - 3 * QKV_W
    assert rest_w == 2 * 3 * QKV_W, "rest columns are addressed as column blocks 1 and 2 of width 3 * QKV_W"
    w_specs = [_layer_spec(w_in, layer, 3 * QKV_W, 1), _layer_spec(w_in, layer, 3 * QKV_W, 2)]
    stacks = (b_merge, vng, vnb, ws_stack, bs_full, wpa, wpb, wout, g_post)
    return pl.pallas_call(
        _mlp_kernel,
        grid=(n // t,),
        in_specs=[pl.BlockSpec((t, d), row), pl.BlockSpec((t, GROUP_W), row), _layer_spec(g_pre, layer)] + w_specs
        + [_layer_spec(s, layer) for s in stacks],
        out_specs=pl.BlockSpec((t, d), row),
        out_shape=jax.ShapeDtypeStruct((n, d), F32),
        compiler_params=pltpu.CompilerParams(dimension_semantics=("arbitrary",), vmem_limit_bytes=VMEM_LIMIT),
        name="prompt_mlp",
    )(x, attn, g_pre, w_in, w_in, *stacks)


def _sample_in_kernel(x_ref, g_ref, w_ref, cq_ref, saq_ref, sbq_ref, ck_ref, sak_ref, sbk_ref,
                      bm_ref, vng_ref, vnb_ref,
                      kv1_ref, kv2_ref, kv3_ref, qt_ref, kvt1_ref, kvt2_ref, kvt3_ref,
                      ga_ref, u_ref, vn_ref, gb_ref, gates_ref):
    hb = _rms(x_ref[...], g_ref[...]).astype(BF16)
    z = jnp.dot(hb, w_ref[...], preferred_element_type=F32)
    bt = qt_ref.shape[-1]

    def store_tiles(dst_ref, rows):
        cols = rows.T
        for i in range(dst_ref.shape[0]):
            dst_ref[i] = cols[:, i * bt:(i + 1) * bt]

    qs = []
    for g, (kv_ref, kvt_ref) in enumerate(((kv1_ref, kvt1_ref), (kv2_ref, kvt2_ref), (kv3_ref, kvt3_ref))):
        lo = g * GROUP_W
        qs.append(_rope(z[:, lo:lo + GROUP_W], cq_ref[...], saq_ref[...], sbq_ref[...]))
        k = _rope(z[:, QKV_W + lo:QKV_W + lo + GROUP_W], ck_ref[...], sak_ref[...], sbk_ref[...])
        kv = jnp.concatenate([k, z[:, 2 * QKV_W + lo:2 * QKV_W + lo + GROUP_W]], axis=-1)
        kv_ref[...] = kv
        store_tiles(kvt_ref, kv)
    store_tiles(qt_ref, jnp.concatenate(qs, axis=-1))
    ga, u, vn, gb, gates = _activations(z[:, 3 * QKV_W:], bm_ref[...], vng_ref[...], vnb_ref[...])
    ga_ref[...] = ga
    u_ref[...] = u
    vn_ref[...] = vn
    gb_ref[...] = gb
    gates_ref[...] = gates


def _sample_in_call(layer, x, g_pre, w_in, tabs_q, tabs_k, b_merge, vng, vnb):
    n, d = x.shape
    bt = SAMPLE_BT
    shapes = ([(n, 2 * GROUP_W)] * 3 + [(n // bt, QKV_W, bt)] + [(n // bt, 2 * GROUP_W, bt)] * 3
              + [(n, w) for w in (GROUP_W, GMLP_W, GMLP_W, GMLP_W, 2 * d)])
    tabs = (*tabs_q, *tabs_k)
    return pl.pallas_call(
        _sample_in_kernel,
        grid=(1,),
        in_specs=[_const_spec(x.shape), _layer_spec(g_pre, layer), _layer_spec(w_in, layer)]
        + [_const_spec(t.shape) for t in tabs] + [_layer_spec(s, layer) for s in (b_merge, vng, vnb)],
        out_specs=[_const_spec(s) for s in shapes],
        out_shape=[jax.ShapeDtypeStruct(s, F32) for s in shapes],
        compiler_params=pltpu.CompilerParams(dimension_semantics=("arbitrary",), vmem_limit_bytes=VMEM_LIMIT),
        name="sample_in",
    )(x, g_pre, w_in, *tabs, b_merge, vng, vnb)


def _sample_attn_kernel(q_ref, kvn1_ref, kvn2_ref, kvn3_ref, c1_ref, c2_ref, c3_ref, out_ref):
    def column(ref, lo, b):
        return ref[lo:lo + GROUP_W, b:b + 1].reshape(HEADS_PER_GROUP, HEAD_DIM, 1)

    for b in range(q_ref.shape[-1]):
        outs, lses = [], []
        for g, (kvn_ref, c_ref) in enumerate(((kvn1_ref, c1_ref), (kvn2_ref, c2_ref), (kvn3_ref, c3_ref))):
            dil = ATTN_GROUPS[g][1]
            q = column(q_ref, g * GROUP_W, b)
            kc, vc = c_ref[b, 0], c_ref[b, 1]
            kn, vn = column(kvn_ref, 0, b), column(kvn_ref, GROUP_W, b)
            rows = kc.shape[-1]
            sc = jnp.sum(kc * q, axis=1, keepdims=True)
            r = lax.broadcasted_iota(jnp.int32, (1, 1, rows), 2)
            sc = jnp.where((r & (dil - 1)) == 0, sc, NEG)
            ss = jnp.sum(kn * q, axis=1, keepdims=True)
            m = jnp.maximum(jnp.max(sc, axis=-1, keepdims=True), ss)
            pc = jnp.exp(sc - m)
            ps = jnp.exp(ss - m)
            l = jnp.sum(pc, axis=-1, keepdims=True) + ps
            o = jnp.sum(vc * pc, axis=-1, keepdims=True) + ps * vn
            outs.append(o / l)
            lses.append(m + jnp.log(l))
        m = jnp.maximum(jnp.maximum(lses[0], lses[1]), lses[2])
        es = [jnp.exp(l - m) for l in lses]
        inv = 1.0 / (es[0] + es[1] + es[2])
        merged = (es[0] * inv) * outs[0] + (es[1] * inv) * outs[1] + (es[2] * inv) * outs[2]
        out_ref[:, b:b + 1] = merged.reshape(GROUP_W, 1)


def _sample_attn_call(layer, q, kvn, caches):
    tiles, _, bt = q.shape
    n = tiles * bt
    tile = lambda i: (i, 0, 0)
    cache_specs = []
    cache_views = []
    for (window, dil), c in zip(ATTN_GROUPS, caches):
        depth, nb, rows = c.shape[:3]
        assert rows == window and dil & (dil - 1) == 0, "cache must hold exactly the window"
        cache_views.append(jnp.transpose(c, (0, 1, 3, 4, 5, 2)))
        cache_specs.append(pl.BlockSpec((None, bt, 2, HEADS_PER_GROUP, HEAD_DIM, rows),
                                        lambda i: (layer, i, 0, 0, 0, 0)))
    return pl.pallas_call(
        _sample_attn_kernel,
        grid=(n // bt,),
        in_specs=[pl.BlockSpec((None, QKV_W, bt), tile)] + [pl.BlockSpec((None, 2 * GROUP_W, bt), tile)] * 3 + cache_specs,
        out_specs=pl.BlockSpec((None, GROUP_W, bt), tile),
        out_shape=jax.ShapeDtypeStruct((tiles, GROUP_W, bt), F32),
        compiler_params=pltpu.CompilerParams(dimension_semantics=("arbitrary",), vmem_limit_bytes=VMEM_LIMIT),
        name="sample_attn",
    )(q, *kvn, *cache_views)


def _sample_out_kernel(x_ref, attn_ref, ga_ref, u_ref, vn_ref, gb_ref, gates_ref, ws0_ref, bs0_ref,
                       wpa_ref, wpb_ref, wout_ref, gpost_ref, out_ref):
    mix = vn_ref[...] * ws0_ref[...] + bs0_ref[...]
    attn = jnp.concatenate([attn_ref[i] for i in range(attn_ref.shape[0])], axis=-1).T
    out_ref[...] = _merge_out(x_ref[...], attn * ga_ref[...], (u_ref[...] * mix) * gb_ref[...],
                              gates_ref[...], wpa_ref[...], wpb_ref[...], wout_ref[...], gpost_ref[...])


def _sample_out_call(layer, x, attn, ga, u, vn, gb, gates, ws0, bs0, wpa, wpb, wout, g_post):
    acts = (x, attn, ga, u, vn, gb, gates)
    stacks = (ws0, bs0, wpa, wpb, wout, g_post)
    return pl.pallas_call(
        _sample_out_kernel,
        grid=(1,),
        in_specs=[_const_spec(a.shape) for a in acts] + [_layer_spec(s, layer) for s in stacks],
        out_specs=_const_spec(x.shape),
        out_shape=jax.ShapeDtypeStruct(x.shape, F32),
        compiler_params=pltpu.CompilerParams(dimension_semantics=("arbitrary",), vmem_limit_bytes=VMEM_LIMIT),
        name="sample_out",
    )(*acts, *stacks)


def kernel(x_prompt, x_sample, cache_kv_w128, cache_kv_w512, cache_kv_w2048, norm_pre, w_in, b_merge, v_norm_g, v_norm_b, w_spatial, b_spatial, w_proj_a, w_proj_b, w_out, norm_post):
    batch, seq, d = x_prompt.shape
    nb, dec_seq, _ = x_sample.shape
    depth = w_in.shape[0]
    assert dec_seq == 1, "the sample group decodes one position per step"
    assert seq % QKV_TILE == 0 and (batch * seq) % MLP_TILE == 0 and nb % SAMPLE_BT == 0
    caches = (cache_kv_w128, cache_kv_w512, cache_kv_w2048)

    w_in_b = w_in.astype(BF16)
    wpa_b, wpb_b, wout_b = w_proj_a.astype(BF16), w_proj_b.astype(BF16), w_out.astype(BF16)
    ws_stack = w_spatial.reshape(depth, GMLP_GROUPS * BLK, BLK)
    chan_group = jnp.arange(GMLP_W) // GMLP_GROUP_DIM
    bs_full = jnp.swapaxes(b_spatial, 1, 2)[:, :, chan_group]
    ws0 = w_spatial[:, :, 0, 0][:, chan_group][:, None, :]
    bs0 = b_spatial[:, :, 0][:, chan_group][:, None, :]
    row2 = lambda a: a[:, None, :]
    g_pre, g_post, bm, vng, vnb = row2(norm_pre), row2(norm_post), row2(b_merge), row2(v_norm_g), row2(v_norm_b)

    scale = HEAD_DIM ** -0.5
    pos_p = jnp.arange(seq, dtype=jnp.int32)
    pos_s = PAST_LEN + jnp.arange(dec_seq, dtype=jnp.int32)
    tq_p, tk_p = _rope_tables(pos_p, scale), _rope_tables(pos_p, 1.0)
    tq_s, tk_s = _rope_tables(pos_s, scale), _rope_tables(pos_s, 1.0)
    band, causal = _attn_bias_tables()

    xp = x_prompt
    xs = x_sample.reshape(nb, d)
    kv_p = None
    kv_s = [[] for _ in ATTN_GROUPS]
    v_s = []
    for l in range(depth):
        q1, kv1, q2, kv2, q3, kv3, *kv_p = _qkv_call(l, xp, g_pre, w_in_b, tq_p, tk_p, kv_p)
        attn = _attn_call(q1, q2, q3, kv1, kv2, kv3, band, causal, batch, seq)
        xp = _mlp_call(l, xp.reshape(batch * seq, d), attn.reshape(batch * seq, GROUP_W), g_pre, w_in_b,
                       bm, vng, vnb, ws_stack, bs_full, wpa_b, wpb_b, wout_b, g_post)
        xp = xp.reshape(batch, seq, d)
        kn1, kn2, kn3, qt, kt1, kt2, kt3, ga, u, vn, gb, gates = _sample_in_call(l, xs, g_pre, w_in_b, tq_s, tk_s,
                                                                                 bm, vng, vnb)
        attn_s = _sample_attn_call(l, qt, (kt1, kt2, kt3), caches)
        xs = _sample_out_call(l, xs, attn_s, ga, u, vn, gb, gates, ws0, bs0, wpa_b, wpb_b, wout_b, g_post)
        for g, kn in enumerate((kn1, kn2, kn3)):
            kv_s[g].append(kn.reshape(nb, dec_seq, 2, HEADS_PER_GROUP, HEAD_DIM))
        v_s.append(vn.reshape(nb, dec_seq, GMLP_W))

    new_kv_p = [jnp.transpose(p, (0, 1, 5, 2, 3, 4)) for p in kv_p]
    return (xp, xs.reshape(nb, dec_seq, d), new_kv_p[0], new_kv_p[1], new_kv_p[2],
            jnp.stack(kv_s[0]), jnp.stack(kv_s[1]), jnp.stack(kv_s[2]), jnp.stack(v_s))
```

```python
import functools

import jax
import jax.numpy as jnp
from jax import lax
from jax.experimental import pallas as pl
from jax.experimental.pallas import tpu as pltpu

F32 = jnp.float32
BF16 = jnp.bfloat16

PAST_LEN = 8192
HEAD_DIM = 64
HEADS_PER_GROUP = 4
GROUP_W = HEADS_PER_GROUP * HEAD_DIM
ATTN_GROUPS = ((128, 1), (512, 4), (2048, 16))
N_GROUPS = len(ATTN_GROUPS)
QKV_W = N_GROUPS * GROUP_W
ROT_DIM = HEAD_DIM // 4
ROT_HALF = ROT_DIM // 2
ROPE_THETA = 500000.0
BLK = 128
GMLP_GROUPS = 4
GMLP_GROUP_DIM = 192
GMLP_W = GMLP_GROUPS * GMLP_GROUP_DIM
EPS = 1e-6
LANES = 128
NEG = -1e30

OFF_GA = 0
OFF_U = OFF_GA + GROUP_W
OFF_VB = OFF_U + GMLP_W
OFF_GB = OFF_VB + GMLP_W
OFF_ML = OFF_GB + GMLP_W

QKV_TILE = 512
MLP_TILE = 512
ATTN_STEP_BLOCKS = 4
VMEM_LIMIT = 56 * 1024 * 1024


def _rms(x, g):
    return (x * lax.rsqrt(jnp.mean(x * x, axis=-1, keepdims=True) + EPS)) * g


def _layer_norm(x, g, b):
    xc = x - jnp.mean(x, axis=-1, keepdims=True)
    return xc * lax.rsqrt(jnp.mean(xc * xc, axis=-1, keepdims=True) + EPS) * g + b


def _rope(z, c, sa, sb):
    outs = []
    for s in range(GROUP_W // LANES):
        zs = z[:, s * LANES:(s + 1) * LANES]
        outs.append(zs * c + pltpu.roll(zs, LANES - ROT_HALF, 1) * sa + pltpu.roll(zs, ROT_HALF, 1) * sb)
    return jnp.concatenate(outs, axis=-1)


def _rope_tables(pos, scale):
    inv_freq = jnp.power(ROPE_THETA, -jnp.arange(ROT_HALF, dtype=F32) / ROT_HALF)
    ang = pos.astype(F32)[:, None] * inv_freq[None, :]
    cos, sin = jnp.cos(ang), jnp.sin(ang)
    n = pos.shape[0]
    rest1 = jnp.ones((n, HEAD_DIM - ROT_DIM), F32)
    rest0 = jnp.zeros((n, HEAD_DIM - ROT_DIM), F32)
    z8 = jnp.zeros((n, ROT_HALF), F32)
    c = jnp.concatenate([cos, cos, rest1], axis=-1)
    sa = jnp.concatenate([-sin, z8, rest0], axis=-1)
    sb = jnp.concatenate([z8, sin, rest0], axis=-1)
    reps = LANES // HEAD_DIM
    return tuple(jnp.tile(t, (1, reps)) * scale for t in (c, sa, sb))


def _activations(z, b_merge, vng, vnb):
    ga = jax.nn.silu(z[:, OFF_GA:OFF_U])
    u = jax.nn.gelu(z[:, OFF_U:OFF_VB])
    vn = _layer_norm(jax.nn.gelu(z[:, OFF_VB:OFF_GB]), vng, vnb)
    gb = jax.nn.silu(z[:, OFF_GB:OFF_ML])
    gates = jax.nn.sigmoid(z[:, OFF_ML:] + b_merge)
    return ga, u, vn, gb, gates


def _merge_out(x, a_in, b_in, gates, wpa, wpb, wout, gpost):
    d = x.shape[-1]
    ba = jnp.dot(a_in.astype(BF16), wpa, preferred_element_type=F32)
    bb = jnp.dot(b_in.astype(BF16), wpb, preferred_element_type=F32)
    merged = gates[:, :d] * ba + gates[:, d:] * bb
    y = jnp.dot(merged.astype(BF16), wout, preferred_element_type=F32)
    return x + _rms(y, gpost)


def _const_spec(shape):
    nd = len(shape)
    return pl.BlockSpec(shape, lambda *_: (0,) * nd)


def _layer_spec(arr, layer, block_cols=None, col_block=0):
    rows, cols = arr.shape[1:]
    return pl.BlockSpec((None, rows, cols if block_cols is None else block_cols), lambda *_: (layer, 0, col_block))


def _qkv_kernel(seq, first_layer, x_ref, g_ref, w_ref, cq_ref, saq_ref, sbq_ref, ck_ref, sak_ref, sbk_ref, *refs):
    if not first_layer:
        refs = refs[N_GROUPS:]
    q1_ref, kv1_ref, q2_ref, kv2_ref, q3_ref, kv3_ref, p1_ref, p2_ref, p3_ref, za_s, zb_s, dei_s = refs
    step = pl.program_id(0)
    t = x_ref.shape[0]
    tiles_per_seq = seq // t
    last_tile = (jnp.maximum(step - 1, 0) % tiles_per_seq) == tiles_per_seq - 1
    q_refs = (q1_ref, q2_ref, q3_ref)
    kv_refs = (kv1_ref, kv2_ref, kv3_ref)
    p_refs = (p1_ref, p2_ref, p3_ref)
    nslab = GROUP_W // LANES

    @pl.when(step == 0)
    def _():
        zb_s[...] = jnp.zeros(zb_s.shape, F32)

    def store_transposed(p_ref, idx, z):
        zt = z.T.reshape(HEADS_PER_GROUP, HEAD_DIM, z.shape[0])
        if first_layer:
            p_ref[0, idx] = zt
            if p_ref.shape[0] > 1:
                p_ref[1:, idx] = jnp.zeros((p_ref.shape[0] - 1,) + zt.shape, F32)
        else:
            p_ref[idx] = zt

    def rope_k(z, g, rows=slice(None)):
        return _rope(z[rows, QKV_W + g * GROUP_W:QKV_W + (g + 1) * GROUP_W], ck_ref[rows, :], sak_ref[rows, :], sbk_ref[rows, :])

    def body(z_new, z):
        hb = _rms(x_ref[...], g_ref[...]).astype(BF16)
        z_new[...] = jnp.dot(hb, w_ref[...], preferred_element_type=F32)
        slots = iter(range(dei_s.shape[0]))

        def split_by_residue(v, dst_ref, lane0, dil):
            slot = next(slots)
            for s in range(nslab):
                dei_s[slot, s] = v[:, s * LANES:(s + 1) * LANES]
            for r in range(dil):
                rows = [dei_s[slot, s, pl.ds(r, t // dil, stride=dil), :] for s in range(nslab)]
                dst_ref[r, :, lane0:lane0 + GROUP_W] = jnp.concatenate(rows, axis=-1).astype(BF16)

        for g, (window, dil) in enumerate(ATTN_GROUPS):
            zq = _rope(z[:, g * GROUP_W:(g + 1) * GROUP_W], cq_ref[...], saq_ref[...], sbq_ref[...])
            zk = rope_k(z, g)
            zv = z[:, 2 * QKV_W + g * GROUP_W:2 * QKV_W + (g + 1) * GROUP_W]
            if dil == 1:
                q_refs[g][...] = zq.astype(BF16)
                kv_refs[g][:, 0:GROUP_W] = zk.astype(BF16)
                kv_refs[g][:, GROUP_W:2 * GROUP_W] = zv.astype(BF16)
            else:
                split_by_residue(zq, q_refs[g], 0, dil)
                split_by_residue(zk, kv_refs[g], 0, dil)
                split_by_residue(zv, kv_refs[g], GROUP_W, dil)
            if window >= seq:
                store_transposed(p_refs[g], 0, zk)
                store_transposed(p_refs[g], 1, zv)

        @pl.when(last_tile)
        def _():
            for g, (window, dil) in enumerate(ATTN_GROUPS):
                if window < seq:
                    tail = slice(t - p_refs[g].shape[-1], t)
                    store_transposed(p_refs[g], 0, rope_k(z, g, tail))
                    store_transposed(p_refs[g], 1, z[tail, 2 * QKV_W + g * GROUP_W:2 * QKV_W + (g + 1) * GROUP_W])

    @pl.when(step % 2 == 0)
    def _():
        body(za_s, zb_s)

    @pl.when(step % 2 == 1)
    def _():
        body(zb_s, za_s)


def _qkv_call(layer, x, g_pre, w_in, tabs_q, tabs_k, accs):
    batch, seq, d = x.shape
    depth = w_in.shape[0]
    first_layer = accs is None
    t = QKV_TILE
    nt = seq // t
    n_tiles = batch * nt
    proj = lambda s: jnp.minimum(s, n_tiles - 1)
    fin = lambda s: jnp.maximum(s - 1, 0)
    tab_spec = pl.BlockSpec((t, LANES), lambda s: (fin(s) % nt, 0))
    in_specs = [pl.BlockSpec((None, t, d), lambda s: (proj(s) // nt, proj(s) % nt, 0)), _layer_spec(g_pre, layer),
                _layer_spec(w_in, layer, 3 * QKV_W, 0)]
    out_specs, out_shape = [], []
    for window, dil in ATTN_GROUPS:
        assert t % (dil * 16) == 0
        if dil == 1:
            q_blk, kv_blk = (None, t, GROUP_W), (None, t, 2 * GROUP_W)
            q_shape, kv_shape = (batch, seq, GROUP_W), (batch, seq, 2 * GROUP_W)
            idx = lambda s: (fin(s) // nt, fin(s) % nt, 0)
        else:
            q_blk, kv_blk = (None, dil, t // dil, GROUP_W), (None, dil, t // dil, 2 * GROUP_W)
            q_shape, kv_shape = (batch, dil, seq // dil, GROUP_W), (batch, dil, seq // dil, 2 * GROUP_W)
            idx = lambda s: (fin(s) // nt, 0, fin(s) % nt, 0)
        out_specs += [pl.BlockSpec(q_blk, idx), pl.BlockSpec(kv_blk, idx)]
        out_shape += [jax.ShapeDtypeStruct(q_shape, BF16), jax.ShapeDtypeStruct(kv_shape, BF16)]
    for window, dil in ATTN_GROUPS:
        keep = min(window, seq)
        assert keep == seq or keep <= t, "a partial window must fit in the last tile"
        blk = min(keep, t)
        every_tile = keep == seq
        lead = 0 if first_layer else layer
        p_blk = (depth if first_layer else None, None, 2, HEADS_PER_GROUP, HEAD_DIM, blk)
        p_idx = lambda s, every_tile=every_tile: (lead, fin(s) // nt, 0, 0, 0, fin(s) % nt if every_tile else 0)
        out_specs.append(pl.BlockSpec(p_blk, p_idx))
        out_shape.append(jax.ShapeDtypeStruct((depth, batch, 2, HEADS_PER_GROUP, HEAD_DIM, keep), F32))
    n_in = len(in_specs) + 6
    acc_specs = [] if first_layer else [pl.BlockSpec(memory_space=pl.ANY)] * N_GROUPS
    aliases = {} if first_layer else {n_in + g: 2 * N_GROUPS + g for g in range(N_GROUPS)}
    n_split = 3 * sum(dil > 1 for _, dil in ATTN_GROUPS)
    return pl.pallas_call(
        functools.partial(_qkv_kernel, seq, first_layer),
        grid=(n_tiles + 1,),
        in_specs=in_specs + [tab_spec] * 6 + acc_specs,
        out_specs=out_specs,
        out_shape=out_shape,
        input_output_aliases=aliases,
        scratch_shapes=[pltpu.VMEM((t, 3 * QKV_W), F32), pltpu.VMEM((t, 3 * QKV_W), F32),
                        pltpu.VMEM((n_split, GROUP_W // LANES, t, LANES), F32)],
        compiler_params=pltpu.CompilerParams(dimension_semantics=("arbitrary",), vmem_limit_bytes=VMEM_LIMIT),
        name="prompt_qkv",
    )(x, g_pre, w_in, *tabs_q, *tabs_k, *(() if first_layer else accs))


def _attn_block(q, k, v, bias):
    nk = k.shape[0]
    hid = lax.shift_right_logical(lax.broadcasted_iota(jnp.int32, (BLK, GROUP_W), 1), HEAD_DIM.bit_length() - 1)
    qf = q.astype(F32)
    qs = jnp.concatenate([jnp.where(hid == h, qf, 0.0) for h in range(HEADS_PER_GROUP)], axis=0).astype(BF16)
    s = lax.dot_general(qs, k, (((1,), (1,)), ((), ())), preferred_element_type=F32)
    s = s.reshape(HEADS_PER_GROUP, BLK, nk) + bias[None]
    m = jnp.max(s, axis=-1, keepdims=True)
    p = jnp.exp(s - m)
    l = jnp.sum(p, axis=-1, keepdims=True)
    pv = jnp.dot(p.reshape(HEADS_PER_GROUP * BLK, nk).astype(BF16), v, preferred_element_type=F32)
    pv = pv.reshape(HEADS_PER_GROUP, BLK, GROUP_W)
    inv = 1.0 / l
    lse = m + jnp.log(l)
    o = jnp.zeros((BLK, GROUP_W), F32)
    ls = jnp.zeros((BLK, GROUP_W), F32)
    for h in range(HEADS_PER_GROUP):
        o = jnp.where(hid == h, pv[h] * inv[h], o)
        ls = jnp.where(hid == h, lse[h], ls)
    return o, ls


def _attn_kernel(q1_ref, q2_ref, q3_ref, kv1_ref, kv2_ref, kv3_ref, band_ref, causal_ref, out_ref,
                 o1_s, l1_s, o2_s, l2_s, o3_s, l3_s):
    step = pl.program_id(1)
    nslab = GROUP_W // LANES
    d2 = ATTN_GROUPS[1][1]
    d3 = ATTN_GROUPS[2][1]
    blocks_per_res2 = out_ref.shape[0] // BLK // d2

    def store(o_s, l_s, rows, o, ls):
        for s in range(nslab):
            o_s[s, rows, :] = o[:, s * LANES:(s + 1) * LANES]
            l_s[s, rows, :] = ls[:, s * LANES:(s + 1) * LANES]

    for jj in range(ATTN_STEP_BLOCKS):
        j = step * ATTN_STEP_BLOCKS + jj
        qrows = slice(jj * BLK, (jj + 1) * BLK)

        start = pl.multiple_of(jnp.maximum(j - 1, 0) * BLK, BLK)
        o, ls = _attn_block(q1_ref[qrows, :], kv1_ref[pl.ds(start, 2 * BLK), 0:GROUP_W],
                            kv1_ref[pl.ds(start, 2 * BLK), GROUP_W:2 * GROUP_W], band_ref[jnp.minimum(j, 1)])
        store(o1_s, l1_s, pl.ds(pl.multiple_of(j * BLK, BLK), BLK), o, ls)

        n = j % blocks_per_res2
        start = pl.multiple_of(jnp.maximum(n - 1, 0) * BLK, BLK)
        o, ls = _attn_block(q2_ref[qrows, :], kv2_ref[pl.ds(start, 2 * BLK), 0:GROUP_W],
                            kv2_ref[pl.ds(start, 2 * BLK), GROUP_W:2 * GROUP_W], band_ref[jnp.minimum(n, 1)])
        store(o2_s, l2_s, pl.ds(n * (BLK * d2) + j // blocks_per_res2, BLK, stride=d2), o, ls)

        o, ls = _attn_block(q3_ref[jj], kv3_ref[jj, :, 0:GROUP_W], kv3_ref[jj, :, GROUP_W:2 * GROUP_W], causal_ref[...])
        store(o3_s, l3_s, pl.ds(j, BLK, stride=d3), o, ls)

    @pl.when(step == pl.num_programs(1) - 1)
    def _():
        seq = out_ref.shape[0]
        rows_per_step = 256
        for s in range(nslab):
            def body(c, carry):
                rows = pl.ds(pl.multiple_of(c * rows_per_step, rows_per_step), rows_per_step)
                l1, l2, l3 = l1_s[s, rows, :], l2_s[s, rows, :], l3_s[s, rows, :]
                m = jnp.maximum(jnp.maximum(l1, l2), l3)
                e1, e2, e3 = jnp.exp(l1 - m), jnp.exp(l2 - m), jnp.exp(l3 - m)
                inv = 1.0 / (e1 + e2 + e3)
                out = (e1 * inv) * o1_s[s, rows, :] + (e2 * inv) * o2_s[s, rows, :] + (e3 * inv) * o3_s[s, rows, :]
                out_ref[rows, s * LANES:(s + 1) * LANES] = out
                return carry
            lax.fori_loop(0, seq // rows_per_step, body, 0)


def _attn_call(q1, q2, q3, kv1, kv2, kv3, band, causal, batch, seq):
    d2 = ATTN_GROUPS[1][1]
    d3 = ATTN_GROUPS[2][1]
    nblk = seq // BLK
    assert seq // d3 == BLK and nblk % d2 == 0 and nblk == d3
    nsb = ATTN_STEP_BLOCKS
    spr2 = nblk // d2 // nsb
    assert (nblk // d2) % nsb == 0
    scratch = [pltpu.VMEM((GROUP_W // LANES, seq, LANES), F32)] * 6
    return pl.pallas_call(
        _attn_kernel,
        grid=(batch, nblk // nsb),
        in_specs=[
            pl.BlockSpec((None, nsb * BLK, GROUP_W), lambda b, j: (b, j, 0)),
            pl.BlockSpec((None, None, nsb * BLK, GROUP_W), lambda b, j: (b, j // spr2, j % spr2, 0)),
            pl.BlockSpec((None, nsb, BLK, GROUP_W), lambda b, j: (b, j, 0, 0)),
            pl.BlockSpec((None, seq, 2 * GROUP_W), lambda b, j: (b, 0, 0)),
            pl.BlockSpec((None, None, seq // d2, 2 * GROUP_W), lambda b, j: (b, j // spr2, 0, 0)),
            pl.BlockSpec((None, nsb, BLK, 2 * GROUP_W), lambda b, j: (b, j, 0, 0)),
            _const_spec(band.shape),
            _const_spec(causal.shape),
        ],
        out_specs=pl.BlockSpec((None, seq, GROUP_W), lambda b, j: (b, 0, 0)),
        out_shape=jax.ShapeDtypeStruct((batch, seq, GROUP_W), F32),
        scratch_shapes=scratch,
        compiler_params=pltpu.CompilerParams(dimension_semantics=("arbitrary", "arbitrary"),
                                             vmem_limit_bytes=VMEM_LIMIT),
        name="prompt_attn",
    )(q1, q2, q3, kv1, kv2, kv3, band, causal)


def _attn_bias_tables():
    qi = jnp.arange(BLK)[:, None]
    kj = jnp.arange(2 * BLK)[None, :]
    general = (kj >= qi) & (kj <= qi + BLK)
    first = kj <= qi
    band = jnp.where(jnp.stack([first, general]), 0.0, NEG).astype(F32)
    causal = jnp.where(jnp.arange(BLK)[None, :] <= qi, 0.0, NEG).astype(F32)
    return band, causal


def _spatial_mix(ws, v):
    gd, two = GMLP_GROUP_DIM, 2 * LANES
    starts = [g * gd // LANES * LANES for g in range(GMLP_GROUPS)]
    assert all(s + two >= (g + 1) * gd for g, s in enumerate(starts))
    r = [jnp.dot(ws[g * BLK:(g + 1) * BLK], v[:, s:s + two], preferred_element_type=F32) for g, s in enumerate(starts)]
    lane = lax.broadcasted_iota(jnp.int32, (BLK, LANES), 1)
    slabs = []
    for lo in range(0, GMLP_W, LANES):
        owners = sorted({lo // gd, (lo + LANES - 1) // gd})
        pieces = [r[g][:, lo - starts[g]:lo - starts[g] + LANES] for g in owners]
        slabs.append(pieces[0] if len(owners) == 1 else jnp.where(lane < owners[1] * gd - lo, pieces[0], pieces[1]))
    return jnp.concatenate(slabs, axis=-1)


def _sample_attend(q, kvn, caches):
    column = lambda v, lo: v[lo:lo + GROUP_W].reshape(HEADS_PER_GROUP, HEAD_DIM, 1)
    outs, lses = [], []
    for g, (window, dil) in enumerate(ATTN_GROUPS):
        qg = column(q, g * GROUP_W)
        kc, vc = caches[g]
        kn, vn = column(kvn[g], 0), column(kvn[g], GROUP_W)
        rows = kc.shape[-1]
        sc = jnp.sum(kc * qg, axis=1, keepdims=True)
        r = lax.broadcasted_iota(jnp.int32, (1, 1, rows), 2)
        sc = jnp.where((r & (dil - 1)) == 0, sc, NEG)
        ss = jnp.sum(kn * qg, axis=1, keepdims=True)
        m = jnp.maximum(jnp.max(sc, axis=-1, keepdims=True), ss)
        pc = jnp.exp(sc - m)
        ps = jnp.exp(ss - m)
        l = jnp.sum(pc, axis=-1, keepdims=True) + ps
        o = jnp.sum(vc * pc, axis=-1, keepdims=True) + ps * vn
        outs.append(o / l)
        lses.append(m + jnp.log(l))
    m = jnp.maximum(jnp.maximum(lses[0], lses[1]), lses[2])
    es = [jnp.exp(l - m) for l in lses]
    inv = 1.0 / (es[0] + es[1] + es[2])
    merged = (es[0] * inv) * outs[0] + (es[1] * inv) * outs[1] + (es[2] * inv) * outs[2]
    return merged.reshape(GROUP_W, 1)


def _mlp_kernel(x_ref, attn_ref, gpre_ref, wlo_ref, whi_ref, bm_ref, vng_ref, vnb_ref, ws_ref, bsp_ref,
                wpa_ref, wpb_ref, wout_ref, gpost_ref, qt_ref, kt1_ref, kt2_ref, kt3_ref, c1_ref, c2_ref, c3_ref,
                out_ref, sattn_ref, sacc_s):
    step = pl.program_id(0)

    @pl.when(step == 0)
    def _():
        sacc_s[...] = jnp.zeros(sacc_s.shape, F32)

    lane = lax.broadcasted_iota(jnp.int32, (1, sacc_s.shape[-1]), 1)
    for k in range(c1_ref.shape[0]):
        sel = lane == step * c1_ref.shape[0] + k
        pick = lambda ref: jnp.sum(jnp.where(sel, ref[...], 0.0), axis=-1, keepdims=True)
        res = _sample_attend(pick(qt_ref), [pick(r) for r in (kt1_ref, kt2_ref, kt3_ref)],
                             [(c[k, 0], c[k, 1]) for c in (c1_ref, c2_ref, c3_ref)])
        sacc_s[...] = jnp.where(sel, res, sacc_s[...])
    sattn_ref[...] = sacc_s[...]

    x = x_ref[...]
    rows = x.shape[0]
    hb = _rms(x, gpre_ref[...]).astype(BF16)
    z = jnp.concatenate([jnp.dot(hb, wlo_ref[...], preferred_element_type=F32),
                         jnp.dot(hb, whi_ref[...], preferred_element_type=F32)], axis=-1)
    ga, u, vn, gb, gates = _activations(z, bm_ref[...], vng_ref[...], vnb_ref[...])
    wr = lax.broadcasted_iota(jnp.int32, ws_ref.shape, 0) & (BLK - 1)
    wc = lax.broadcasted_iota(jnp.int32, ws_ref.shape, 1)
    ws = jnp.where(wc <= wr, ws_ref[...], 0.0).astype(BF16)
    vnb16 = vn.astype(BF16)
    mix = jnp.concatenate([_spatial_mix(ws, vnb16[c * BLK:(c + 1) * BLK]) + bsp_ref[...]
                           for c in range(rows // BLK)], axis=0)
    out_ref[...] = _merge_out(x, attn_ref[...] * ga, (u * mix) * gb, gates,
                              wpa_ref[...], wpb_ref[...], wout_ref[...], gpost_ref[...])


def _mlp_call(layer, x, attn, g_pre, w_in, b_merge, vng, vnb, ws_stack, bs_full, wpa, wpb, wout, g_post,
              sample_q, sample_kv, caches):
    n, d = x.shape
    t = MLP_TILE
    steps = n // t
    nb = sample_q.shape[-1]
    assert nb % steps == 0, "every grid step carries the same number of samples"
    spb = nb // steps
    row = lambda i: (i, 0)
    rest_w = w_in.shape[-1] - 3 * QKV_W
    assert rest_w == 2 * 3 * QKV_W, "rest columns are addressed as column blocks 1 and 2 of width 3 * QKV_W"
    w_specs = [_layer_spec(w_in, layer, 3 * QKV_W, 1), _layer_spec(w_in, layer, 3 * QKV_W, 2)]
    stacks = (b_merge, vng, vnb, ws_stack, bs_full, wpa, wpb, wout, g_post)
    cache_specs, cache_views = [], []
    for (window, dil), c in zip(ATTN_GROUPS, caches):
        rows = c.shape[2]
        assert rows == window and window % dil == 0 and dil & (dil - 1) == 0, "cache must hold exactly the window"
        cache_views.append(jnp.transpose(c, (0, 1, 3, 4, 5, 2)))
        cache_specs.append(pl.BlockSpec((None, spb, 2, HEADS_PER_GROUP, HEAD_DIM, rows),
                                        lambda i: (layer, i, 0, 0, 0, 0)))
    return pl.pallas_call(
        _mlp_kernel,
        grid=(steps,),
        in_specs=[pl.BlockSpec((t, d), row), pl.BlockSpec((t, GROUP_W), row), _layer_spec(g_pre, layer)] + w_specs
        + [_layer_spec(s, layer) for s in stacks]
        + [_const_spec(sample_q.shape)] + [_const_spec(kv.shape) for kv in sample_kv] + cache_specs,
        out_specs=[pl.BlockSpec((t, d), row), _const_spec((GROUP_W, nb))],
        out_shape=[jax.ShapeDtypeStruct((n, d), F32), jax.ShapeDtypeStruct((GROUP_W, nb), F32)],
        scratch_shapes=[pltpu.VMEM((GROUP_W, nb), F32)],
        compiler_params=pltpu.CompilerParams(dimension_semantics=("arbitrary",), vmem_limit_bytes=VMEM_LIMIT),
        name="prompt_mlp",
    )(x, attn, g_pre, w_in, w_in, *stacks, sample_q, *sample_kv, *cache_views)


def _sample_in_kernel(x_ref, g_ref, w_ref, cq_ref, saq_ref, sbq_ref, ck_ref, sak_ref, sbk_ref,
                      bm_ref, vng_ref, vnb_ref,
                      kv1_ref, kv2_ref, kv3_ref, qt_ref, kvt1_ref, kvt2_ref, kvt3_ref,
                      ga_ref, u_ref, vn_ref, gb_ref, gates_ref):
    hb = _rms(x_ref[...], g_ref[...]).astype(BF16)
    z = jnp.dot(hb, w_ref[...], preferred_element_type=F32)

    def store_tiles(dst_ref, rows):
        dst_ref[...] = rows.T

    qs = []
    for g, (kv_ref, kvt_ref) in enumerate(((kv1_ref, kvt1_ref), (kv2_ref, kvt2_ref), (kv3_ref, kvt3_ref))):
        lo = g * GROUP_W
        qs.append(_rope(z[:, lo:lo + GROUP_W], cq_ref[...], saq_ref[...], sbq_ref[...]))
        k = _rope(z[:, QKV_W + lo:QKV_W + lo + GROUP_W], ck_ref[...], sak_ref[...], sbk_ref[...])
        kv = jnp.concatenate([k, z[:, 2 * QKV_W + lo:2 * QKV_W + lo + GROUP_W]], axis=-1)
        kv_ref[...] = kv
        store_tiles(kvt_ref, kv)
    store_tiles(qt_ref, jnp.concatenate(qs, axis=-1))
    ga, u, vn, gb, gates = _activations(z[:, 3 * QKV_W:], bm_ref[...], vng_ref[...], vnb_ref[...])
    ga_ref[...] = ga
    u_ref[...] = u
    vn_ref[...] = vn
    gb_ref[...] = gb
    gates_ref[...] = gates


def _sample_in_call(layer, x, g_pre, w_in, tabs_q, tabs_k, b_merge, vng, vnb):
    n, d = x.shape
    shapes = ([(n, 2 * GROUP_W)] * 3 + [(QKV_W, n)] + [(2 * GROUP_W, n)] * 3
              + [(n, w) for w in (GROUP_W, GMLP_W, GMLP_W, GMLP_W, 2 * d)])
    tabs = (*tabs_q, *tabs_k)
    return pl.pallas_call(
        _sample_in_kernel,
        grid=(1,),
        in_specs=[_const_spec(x.shape), _layer_spec(g_pre, layer), _layer_spec(w_in, layer)]
        + [_const_spec(t.shape) for t in tabs] + [_layer_spec(s, layer) for s in (b_merge, vng, vnb)],
        out_specs=[_const_spec(s) for s in shapes],
        out_shape=[jax.ShapeDtypeStruct(s, F32) for s in shapes],
        compiler_params=pltpu.CompilerParams(dimension_semantics=("arbitrary",), vmem_limit_bytes=VMEM_LIMIT),
        name="sample_in",
    )(x, g_pre, w_in, *tabs, b_merge, vng, vnb)


def _sample_out_kernel(x_ref, attn_ref, ga_ref, u_ref, vn_ref, gb_ref, gates_ref, ws0_ref, bs0_ref,
                       wpa_ref, wpb_ref, wout_ref, gpost_ref, out_ref):
    mix = vn_ref[...] * ws0_ref[...] + bs0_ref[...]
    attn = attn_ref[...].T
    out_ref[...] = _merge_out(x_ref[...], attn * ga_ref[...], (u_ref[...] * mix) * gb_ref[...],
                              gates_ref[...], wpa_ref[...], wpb_ref[...], wout_ref[...], gpost_ref[...])


def _sample_out_call(layer, x, attn, ga, u, vn, gb, gates, ws0, bs0, wpa, wpb, wout, g_post):
    acts = (x, attn, ga, u, vn, gb, gates)
    stacks = (ws0, bs0, wpa, wpb, wout, g_post)
    return pl.pallas_call(
        _sample_out_kernel,
        grid=(1,),
        in_specs=[_const_spec(a.shape) for a in acts] + [_layer_spec(s, layer) for s in stacks],
        out_specs=_const_spec(x.shape),
        out_shape=jax.ShapeDtypeStruct(x.shape, F32),
        compiler_params=pltpu.CompilerParams(dimension_semantics=("arbitrary",), vmem_limit_bytes=VMEM_LIMIT),
        name="sample_out",
    )(*acts, *stacks)


def kernel(x_prompt, x_sample, cache_kv_w128, cache_kv_w512, cache_kv_w2048, norm_pre, w_in, b_merge, v_norm_g, v_norm_b, w_spatial, b_spatial, w_proj_a, w_proj_b, w_out, norm_post):
    batch, seq, d = x_prompt.shape
    nb, dec_seq, _ = x_sample.shape
    depth = w_in.shape[0]
    assert dec_seq == 1, "the sample group decodes one position per step"
    assert seq % QKV_TILE == 0 and (batch * seq) % MLP_TILE == 0
    caches = (cache_kv_w128, cache_kv_w512, cache_kv_w2048)

    w_in_b = w_in.astype(BF16)
    wpa_b, wpb_b, wout_b = w_proj_a.astype(BF16), w_proj_b.astype(BF16), w_out.astype(BF16)
    ws_stack = w_spatial.reshape(depth, GMLP_GROUPS * BLK, BLK)
    chan_group = jnp.arange(GMLP_W) // GMLP_GROUP_DIM
    bs_full = jnp.swapaxes(b_spatial, 1, 2)[:, :, chan_group]
    ws0 = w_spatial[:, :, 0, 0][:, chan_group][:, None, :]
    bs0 = b_spatial[:, :, 0][:, chan_group][:, None, :]
    row2 = lambda a: a[:, None, :]
    g_pre, g_post, bm, vng, vnb = row2(norm_pre), row2(norm_post), row2(b_merge), row2(v_norm_g), row2(v_norm_b)

    scale = HEAD_DIM ** -0.5
    pos_p = jnp.arange(seq, dtype=jnp.int32)
    pos_s = PAST_LEN + jnp.arange(dec_seq, dtype=jnp.int32)
    tq_p, tk_p = _rope_tables(pos_p, scale), _rope_tables(pos_p, 1.0)
    tq_s, tk_s = _rope_tables(pos_s, scale), _rope_tables(pos_s, 1.0)
    band, causal = _attn_bias_tables()

    xp = x_prompt
    xs = x_sample.reshape(nb, d)
    kv_p = None
    kv_s = [[] for _ in ATTN_GROUPS]
    v_s = []
    for l in range(depth):
        q1, kv1, q2, kv2, q3, kv3, *kv_p = _qkv_call(l, xp, g_pre, w_in_b, tq_p, tk_p, kv_p)
        attn = _attn_call(q1, q2, q3, kv1, kv2, kv3, band, causal, batch, seq)
        kn1, kn2, kn3, qt, kt1, kt2, kt3, ga, u, vn, gb, gates = _sample_in_call(l, xs, g_pre, w_in_b, tq_s, tk_s,
                                                                                 bm, vng, vnb)
        xp, attn_s = _mlp_call(l, xp.reshape(batch * seq, d), attn.reshape(batch * seq, GROUP_W), g_pre, w_in_b,
                               bm, vng, vnb, ws_stack, bs_full, wpa_b, wpb_b, wout_b, g_post,
                               qt, (kt1, kt2, kt3), caches)
        xp = xp.reshape(batch, seq, d)
        xs = _sample_out_call(l, xs, attn_s, ga, u, vn, gb, gates, ws0, bs0, wpa_b, wpb_b, wout_b, g_post)
        for g, kn in enumerate((kn1, kn2, kn3)):
            kv_s[g].append(kn.reshape(nb, dec_seq, 2, HEADS_PER_GROUP, HEAD_DIM))
        v_s.append(vn.reshape(nb, dec_seq, GMLP_W))

    new_kv_p = [jnp.transpose(p, (0, 1, 5, 2, 3, 4)) for p in kv_p]
    return (xp, xs.reshape(nb, dec_seq, d), new_kv_p[0], new_kv_p[1], new_kv_p[2],
            jnp.stack(kv_s[0]), jnp.stack(kv_s[1]), jnp.stack(kv_s[2]), jnp.stack(v_s))
```

```python
import functools

import jax
import jax.numpy as jnp
from jax import lax
from jax.experimental import pallas as pl
from jax.experimental.pallas import tpu as pltpu

F32 = jnp.float32
BF16 = jnp.bfloat16

PAST_LEN = 8192
HEAD_DIM = 64
HEADS_PER_GROUP = 4
GROUP_W = HEADS_PER_GROUP * HEAD_DIM
ATTN_GROUPS = ((128, 1), (512, 4), (2048, 16))
N_GROUPS = len(ATTN_GROUPS)
QKV_W = N_GROUPS * GROUP_W
ROT_DIM = HEAD_DIM // 4
ROT_HALF = ROT_DIM // 2
ROPE_THETA = 500000.0
BLK = 128
GMLP_GROUPS = 4
GMLP_GROUP_DIM = 192
GMLP_W = GMLP_GROUPS * GMLP_GROUP_DIM
EPS = 1e-6
LANES = 128
NEG = -1e30

OFF_GA = 0
OFF_U = OFF_GA + GROUP_W
OFF_VB = OFF_U + GMLP_W
OFF_GB = OFF_VB + GMLP_W
OFF_ML = OFF_GB + GMLP_W

QKV_TILE = 512
MLP_TILE = 512
VMEM_LIMIT = 56 * 1024 * 1024


def _rms(x, g):
    return (x * lax.rsqrt(jnp.mean(x * x, axis=-1, keepdims=True) + EPS)) * g


def _layer_norm(x, g, b):
    xc = x - jnp.mean(x, axis=-1, keepdims=True)
    return xc * lax.rsqrt(jnp.mean(xc * xc, axis=-1, keepdims=True) + EPS) * g + b


def _rope(z, c, sa, sb):
    outs = []
    for s in range(GROUP_W // LANES):
        zs = z[:, s * LANES:(s + 1) * LANES]
        outs.append(zs * c + pltpu.roll(zs, LANES - ROT_HALF, 1) * sa + pltpu.roll(zs, ROT_HALF, 1) * sb)
    return jnp.concatenate(outs, axis=-1)


def _rope_tables(pos, scale):
    inv_freq = jnp.power(ROPE_THETA, -jnp.arange(ROT_HALF, dtype=F32) / ROT_HALF)
    ang = pos.astype(F32)[:, None] * inv_freq[None, :]
    cos, sin = jnp.cos(ang), jnp.sin(ang)
    n = pos.shape[0]
    rest1 = jnp.ones((n, HEAD_DIM - ROT_DIM), F32)
    rest0 = jnp.zeros((n, HEAD_DIM - ROT_DIM), F32)
    z8 = jnp.zeros((n, ROT_HALF), F32)
    c = jnp.concatenate([cos, cos, rest1], axis=-1)
    sa = jnp.concatenate([-sin, z8, rest0], axis=-1)
    sb = jnp.concatenate([z8, sin, rest0], axis=-1)
    reps = LANES // HEAD_DIM
    return tuple(jnp.tile(t, (1, reps)) * scale for t in (c, sa, sb))


_PACK_ROT = HEADS_PER_GROUP * ROT_HALF
_PACK_REST = HEAD_DIM - ROT_DIM


def _packed_order():
    heads = range(HEADS_PER_GROUP)
    first = [HEAD_DIM * h + i for h in heads for i in range(ROT_HALF)]
    second = [HEAD_DIM * h + ROT_HALF + i for h in heads for i in range(ROT_HALF)]
    rest = [HEAD_DIM * h + ROT_DIM + j for h in heads for j in range(_PACK_REST)]
    return first + second + rest


def _packed_head_of_lane(shape):
    lane = lax.broadcasted_iota(jnp.int32, shape, len(shape) - 1)
    rot_head = lax.shift_right_logical(lane, ROT_HALF.bit_length() - 1) & (HEADS_PER_GROUP - 1)
    rest_head = sum((lane >= 2 * _PACK_ROT + h * _PACK_REST).astype(jnp.int32) for h in range(1, HEADS_PER_GROUP))
    return jnp.where(lane < 2 * _PACK_ROT, rot_head, rest_head)


def _rope_packed(z, c, sa, sb):
    zs = z[:, :LANES]
    rot = zs * c + pltpu.roll(zs, LANES - _PACK_ROT, 1) * sa + pltpu.roll(zs, _PACK_ROT, 1) * sb
    return jnp.concatenate([rot, z[:, LANES:]], axis=-1)


def _rope_tables_packed(pos):
    inv_freq = jnp.power(ROPE_THETA, -jnp.arange(ROT_HALF, dtype=F32) / ROT_HALF)
    ang = pos.astype(F32)[:, None] * inv_freq[None, :]
    cos = jnp.tile(jnp.cos(ang), (1, HEADS_PER_GROUP))
    sin = jnp.tile(jnp.sin(ang), (1, HEADS_PER_GROUP))
    n = pos.shape[0]
    zeros = lambda w: jnp.zeros((n, w), F32)
    c = jnp.concatenate([cos, cos, jnp.ones((n, LANES - 2 * _PACK_ROT), F32)], axis=-1)
    sa = jnp.concatenate([-sin, zeros(LANES - _PACK_ROT)], axis=-1)
    sb = jnp.concatenate([zeros(_PACK_ROT), sin, zeros(LANES - 2 * _PACK_ROT)], axis=-1)
    return c, sa, sb


def _unpack_rows(zt):
    heads = []
    for h in range(HEADS_PER_GROUP):
        heads.append(jnp.concatenate([
            zt[ROT_HALF * h:ROT_HALF * (h + 1)],
            zt[_PACK_ROT + ROT_HALF * h:_PACK_ROT + ROT_HALF * (h + 1)],
            zt[2 * _PACK_ROT + _PACK_REST * h:2 * _PACK_ROT + _PACK_REST * (h + 1)]], axis=0))
    return jnp.stack(heads, axis=0)


def _activations(z, b_merge, vng, vnb):
    ga = jax.nn.silu(z[:, OFF_GA:OFF_U])
    u = jax.nn.gelu(z[:, OFF_U:OFF_VB])
    vn = _layer_norm(jax.nn.gelu(z[:, OFF_VB:OFF_GB]), vng, vnb)
    gb = jax.nn.silu(z[:, OFF_GB:OFF_ML])
    gates = jax.nn.sigmoid(z[:, OFF_ML:] + b_merge)
    return ga, u, vn, gb, gates


def _merge_out(x, a_in, b_in, gates, wpa, wpb, wout, gpost):
    d = x.shape[-1]
    ba = jnp.dot(a_in.astype(BF16), wpa, preferred_element_type=F32)
    bb = jnp.dot(b_in.astype(BF16), wpb, preferred_element_type=F32)
    merged = gates[:, :d] * ba + gates[:, d:] * bb
    y = jnp.dot(merged.astype(BF16), wout, preferred_element_type=F32)
    return x + _rms(y, gpost)


def _const_spec(shape):
    nd = len(shape)
    return pl.BlockSpec(shape, lambda *_: (0,) * nd)


def _layer_spec(arr, layer, block_cols=None, col_block=0):
    rows, cols = arr.shape[1:]
    return pl.BlockSpec((None, rows, cols if block_cols is None else block_cols), lambda *_: (layer, 0, col_block))


def _qkv_kernel(seq, first_layer, x_ref, g_ref, w_ref, c_ref, sa_ref, sb_ref, *refs):
    if not first_layer:
        refs = refs[N_GROUPS:]
    q1_ref, kv1_ref, q2_ref, kv2_ref, q3_ref, kv3_ref, p1_ref, p2_ref, p3_ref, za_s, zb_s, dei_s = refs
    step = pl.program_id(0)
    t = x_ref.shape[0]
    tiles_per_seq = seq // t
    last_tile = (jnp.maximum(step - 1, 0) % tiles_per_seq) == tiles_per_seq - 1
    q_refs = (q1_ref, q2_ref, q3_ref)
    kv_refs = (kv1_ref, kv2_ref, kv3_ref)
    p_refs = (p1_ref, p2_ref, p3_ref)
    nslab = GROUP_W // LANES

    @pl.when(step == 0)
    def _():
        zb_s[...] = jnp.zeros(zb_s.shape, F32)

    def store_transposed(p_ref, idx, z, packed):
        zt = _unpack_rows(z.T) if packed else z.T.reshape(HEADS_PER_GROUP, HEAD_DIM, z.shape[0])
        if first_layer:
            p_ref[0, idx] = zt
            if p_ref.shape[0] > 1:
                p_ref[1:, idx] = jnp.zeros((p_ref.shape[0] - 1,) + zt.shape, F32)
        else:
            p_ref[idx] = zt

    def rope(z, col0, rows=slice(None)):
        return _rope_packed(z[rows, col0:col0 + GROUP_W], c_ref[rows, :], sa_ref[rows, :], sb_ref[rows, :])

    def body(z_new, z):
        hb = _rms(x_ref[...], g_ref[...]).astype(BF16)
        z_new[...] = jnp.dot(hb, w_ref[...], preferred_element_type=F32)
        slots = iter(range(dei_s.shape[0]))

        def split_by_residue(v, dst_ref, lane0, dil):
            slot = next(slots)
            for s in range(nslab):
                dei_s[slot, s] = v[:, s * LANES:(s + 1) * LANES]
            for r in range(dil):
                rows = [dei_s[slot, s, pl.ds(r, t // dil, stride=dil), :] for s in range(nslab)]
                dst_ref[r, :, lane0:lane0 + GROUP_W] = jnp.concatenate(rows, axis=-1).astype(BF16)

        for g, (window, dil) in enumerate(ATTN_GROUPS):
            zq = rope(z, g * GROUP_W)
            zk = rope(z, QKV_W + g * GROUP_W)
            zv = z[:, 2 * QKV_W + g * GROUP_W:2 * QKV_W + (g + 1) * GROUP_W]
            if dil == 1:
                q_refs[g][...] = zq.astype(BF16)
                kv_refs[g][:, 0:GROUP_W] = zk.astype(BF16)
                kv_refs[g][:, GROUP_W:2 * GROUP_W] = zv.astype(BF16)
            else:
                split_by_residue(zq, q_refs[g], 0, dil)
                split_by_residue(zk, kv_refs[g], 0, dil)
                split_by_residue(zv, kv_refs[g], GROUP_W, dil)
            if window >= seq:
                store_transposed(p_refs[g], 0, zk, True)
                store_transposed(p_refs[g], 1, zv, False)

        @pl.when(last_tile)
        def _():
            for g, (window, dil) in enumerate(ATTN_GROUPS):
                if window < seq:
                    tail = slice(t - p_refs[g].shape[-1], t)
                    store_transposed(p_refs[g], 0, rope(z, QKV_W + g * GROUP_W, tail), True)
                    store_transposed(p_refs[g], 1, z[tail, 2 * QKV_W + g * GROUP_W:2 * QKV_W + (g + 1) * GROUP_W], False)

    @pl.when(step % 2 == 0)
    def _():
        body(za_s, zb_s)

    @pl.when(step % 2 == 1)
    def _():
        body(zb_s, za_s)


def _qkv_call(layer, x, g_pre, w_qkv, tabs, accs):
    batch, seq, d = x.shape
    depth = w_qkv.shape[0]
    first_layer = accs is None
    t = QKV_TILE
    nt = seq // t
    n_tiles = batch * nt
    proj = lambda s: jnp.minimum(s, n_tiles - 1)
    fin = lambda s: jnp.maximum(s - 1, 0)
    tab_spec = pl.BlockSpec((t, LANES), lambda s: (fin(s) % nt, 0))
    in_specs = [pl.BlockSpec((None, t, d), lambda s: (proj(s) // nt, proj(s) % nt, 0)), _layer_spec(g_pre, layer),
                _layer_spec(w_qkv, layer)]
    out_specs, out_shape = [], []
    for window, dil in ATTN_GROUPS:
        assert t % (dil * 16) == 0
        if dil == 1:
            q_blk, kv_blk = (None, t, GROUP_W), (None, t, 2 * GROUP_W)
            q_shape, kv_shape = (batch, seq, GROUP_W), (batch, seq, 2 * GROUP_W)
            idx = lambda s: (fin(s) // nt, fin(s) % nt, 0)
        else:
            q_blk, kv_blk = (None, dil, t // dil, GROUP_W), (None, dil, t // dil, 2 * GROUP_W)
            q_shape, kv_shape = (batch, dil, seq // dil, GROUP_W), (batch, dil, seq // dil, 2 * GROUP_W)
            idx = lambda s: (fin(s) // nt, 0, fin(s) % nt, 0)
        out_specs += [pl.BlockSpec(q_blk, idx), pl.BlockSpec(kv_blk, idx)]
        out_shape += [jax.ShapeDtypeStruct(q_shape, BF16), jax.ShapeDtypeStruct(kv_shape, BF16)]
    for window, dil in ATTN_GROUPS:
        keep = min(window, seq)
        assert keep == seq or keep <= t, "a partial window must fit in the last tile"
        blk = min(keep, t)
        every_tile = keep == seq
        lead = 0 if first_layer else layer
        p_blk = (depth if first_layer else None, None, 2, HEADS_PER_GROUP, HEAD_DIM, blk)
        p_idx = lambda s, every_tile=every_tile: (lead, fin(s) // nt, 0, 0, 0, fin(s) % nt if every_tile else 0)
        out_specs.append(pl.BlockSpec(p_blk, p_idx))
        out_shape.append(jax.ShapeDtypeStruct((depth, batch, 2, HEADS_PER_GROUP, HEAD_DIM, keep), F32))
    n_in = len(in_specs) + len(tabs)
    acc_specs = [] if first_layer else [pl.BlockSpec(memory_space=pl.ANY)] * N_GROUPS
    aliases = {} if first_layer else {n_in + g: 2 * N_GROUPS + g for g in range(N_GROUPS)}
    n_split = 3 * sum(dil > 1 for _, dil in ATTN_GROUPS)
    return pl.pallas_call(
        functools.partial(_qkv_kernel, seq, first_layer),
        grid=(n_tiles + 1,),
        in_specs=in_specs + [tab_spec] * len(tabs) + acc_specs,
        out_specs=out_specs,
        out_shape=out_shape,
        input_output_aliases=aliases,
        scratch_shapes=[pltpu.VMEM((t, 3 * QKV_W), F32), pltpu.VMEM((t, 3 * QKV_W), F32),
                        pltpu.VMEM((n_split, GROUP_W // LANES, t, LANES), F32)],
        compiler_params=pltpu.CompilerParams(dimension_semantics=("arbitrary",), vmem_limit_bytes=VMEM_LIMIT),
        name="prompt_qkv",
    )(x, g_pre, w_qkv, *tabs, *(() if first_layer else accs))


def _attn_block(q, k, v, bias):
    nk = k.shape[0]
    hid = lax.shift_right_logical(lax.broadcasted_iota(jnp.int32, (BLK, GROUP_W), 1), HEAD_DIM.bit_length() - 1)
    hid_qk = _packed_head_of_lane((BLK, GROUP_W))
    qf = q.astype(F32)
    qs = jnp.concatenate([jnp.where(hid_qk == h, qf, 0.0) for h in range(HEADS_PER_GROUP)], axis=0).astype(BF16)
    s = lax.dot_general(qs, k, (((1,), (1,)), ((), ())), preferred_element_type=F32)
    s = s.reshape(HEADS_PER_GROUP, BLK, nk) + bias[None]
    m = jnp.max(s, axis=-1, keepdims=True)
    p = jnp.exp(s - m)
    l = jnp.sum(p, axis=-1, keepdims=True)
    pv = jnp.dot(p.reshape(HEADS_PER_GROUP * BLK, nk).astype(BF16), v, preferred_element_type=F32)
    pv = pv.reshape(HEADS_PER_GROUP, BLK, GROUP_W)
    inv = 1.0 / l
    lse = m + jnp.log(l)
    o = jnp.zeros((BLK, GROUP_W), F32)
    ls = jnp.zeros((BLK, GROUP_W), F32)
    for h in range(HEADS_PER_GROUP):
        o = jnp.where(hid == h, pv[h] * inv[h], o)
        ls = jnp.where(hid == h, lse[h], ls)
    return o, ls


def _attn_kernel(q1_ref, q2_ref, q3_ref, kv1_ref, kv2_ref, kv3_ref, band_ref, causal_ref, out_ref,
                 o1_s, l1_s, o2_s, l2_s, o3_s, l3_s):
    seq = out_ref.shape[0]
    nslab = GROUP_W // LANES
    d2 = ATTN_GROUPS[1][1]
    d3 = ATTN_GROUPS[2][1]
    nblk = seq // BLK
    blocks_per_res2 = nblk // d2
    kcols, vcols = slice(0, GROUP_W), slice(GROUP_W, 2 * GROUP_W)

    def store(o_s, l_s, rows, o, ls):
        for s in range(nslab):
            o_s[s, rows, :] = o[:, s * LANES:(s + 1) * LANES]
            l_s[s, rows, :] = ls[:, s * LANES:(s + 1) * LANES]

    def band_block(q, kv, n):
        if n == 0:
            return _attn_block(q, kv[0:BLK, kcols], kv[0:BLK, vcols], causal_ref[...])
        keys = slice((n - 1) * BLK, (n + 1) * BLK)
        return _attn_block(q, kv[keys, kcols], kv[keys, vcols], band_ref[...])

    for j in range(nblk):
        o, ls = band_block(q1_ref[j * BLK:(j + 1) * BLK, :], kv1_ref, j)
        store(o1_s, l1_s, slice(j * BLK, (j + 1) * BLK), o, ls)
        r, n = divmod(j, blocks_per_res2)
        o, ls = band_block(q2_ref[r, n * BLK:(n + 1) * BLK, :], kv2_ref.at[r], n)
        store(o2_s, l2_s, pl.ds(n * (BLK * d2) + r, BLK, stride=d2), o, ls)
        o, ls = _attn_block(q3_ref[j], kv3_ref[j, :, kcols], kv3_ref[j, :, vcols], causal_ref[...])
        store(o3_s, l3_s, pl.ds(j, BLK, stride=d3), o, ls)

    rows_per_step = 256
    for s in range(nslab):
        def body(c, carry):
            rows = pl.ds(pl.multiple_of(c * rows_per_step, rows_per_step), rows_per_step)
            l1, l2, l3 = l1_s[s, rows, :], l2_s[s, rows, :], l3_s[s, rows, :]
            m = jnp.maximum(jnp.maximum(l1, l2), l3)
            e1, e2, e3 = jnp.exp(l1 - m), jnp.exp(l2 - m), jnp.exp(l3 - m)
            inv = 1.0 / (e1 + e2 + e3)
            out = (e1 * inv) * o1_s[s, rows, :] + (e2 * inv) * o2_s[s, rows, :] + (e3 * inv) * o3_s[s, rows, :]
            out_ref[rows, s * LANES:(s + 1) * LANES] = out
            return carry
        lax.fori_loop(0, seq // rows_per_step, body, 0)


def _attn_call(q1, q2, q3, kv1, kv2, kv3, band, causal, batch, seq):
    d2 = ATTN_GROUPS[1][1]
    d3 = ATTN_GROUPS[2][1]
    nblk = seq // BLK
    assert seq // d3 == BLK and nblk % d2 == 0 and nblk == d3
    per_seq = lambda a: pl.BlockSpec((None,) + a.shape[1:], lambda b: (b,) + (0,) * (a.ndim - 1))
    scratch = [pltpu.VMEM((GROUP_W // LANES, seq, LANES), F32)] * 6
    return pl.pallas_call(
        _attn_kernel,
        grid=(batch,),
        in_specs=[per_seq(a) for a in (q1, q2, q3, kv1, kv2, kv3)] + [_const_spec(band.shape), _const_spec(causal.shape)],
        out_specs=pl.BlockSpec((None, seq, GROUP_W), lambda b: (b, 0, 0)),
        out_shape=jax.ShapeDtypeStruct((batch, seq, GROUP_W), F32),
        scratch_shapes=scratch,
        compiler_params=pltpu.CompilerParams(dimension_semantics=("arbitrary",), vmem_limit_bytes=VMEM_LIMIT),
        name="prompt_attn",
    )(q1, q2, q3, kv1, kv2, kv3, band, causal)


def _attn_bias_tables():
    qi = jnp.arange(BLK)[:, None]
    kj = jnp.arange(2 * BLK)[None, :]
    band = jnp.where((kj >= qi) & (kj <= qi + BLK), 0.0, NEG).astype(F32)
    causal = jnp.where(jnp.arange(BLK)[None, :] <= qi, 0.0, NEG).astype(F32)
    return band, causal


def _spatial_mix(ws, v):
    gd, two = GMLP_GROUP_DIM, 2 * LANES
    starts = [g * gd // LANES * LANES for g in range(GMLP_GROUPS)]
    assert all(s + two >= (g + 1) * gd for g, s in enumerate(starts))
    r = [jnp.dot(ws[g * BLK:(g + 1) * BLK], v[:, s:s + two], preferred_element_type=F32) for g, s in enumerate(starts)]
    lane = lax.broadcasted_iota(jnp.int32, (BLK, LANES), 1)
    slabs = []
    for lo in range(0, GMLP_W, LANES):
        owners = sorted({lo // gd, (lo + LANES - 1) // gd})
        pieces = [r[g][:, lo - starts[g]:lo - starts[g] + LANES] for g in owners]
        slabs.append(pieces[0] if len(owners) == 1 else jnp.where(lane < owners[1] * gd - lo, pieces[0], pieces[1]))
    return jnp.concatenate(slabs, axis=-1)


def _sample_attend(q, kvn, caches):
    column = lambda v, lo: v[lo:lo + GROUP_W].reshape(HEADS_PER_GROUP, HEAD_DIM, 1)
    outs, lses = [], []
    for g, (window, dil) in enumerate(ATTN_GROUPS):
        qg = column(q, g * GROUP_W)
        kc, vc = caches[g]
        kn, vn = column(kvn[g], 0), column(kvn[g], GROUP_W)
        rows = kc.shape[-1]
        sc = jnp.sum(kc * qg, axis=1, keepdims=True)
        r = lax.broadcasted_iota(jnp.int32, (1, 1, rows), 2)
        sc = jnp.where((r & (dil - 1)) == 0, sc, NEG)
        ss = jnp.sum(kn * qg, axis=1, keepdims=True)
        m = jnp.maximum(jnp.max(sc, axis=-1, keepdims=True), ss)
        pc = jnp.exp(sc - m)
        ps = jnp.exp(ss - m)
        l = jnp.sum(pc, axis=-1, keepdims=True) + ps
        o = jnp.sum(vc * pc, axis=-1, keepdims=True) + ps * vn
        outs.append(o / l)
        lses.append(m + jnp.log(l))
    m = jnp.maximum(jnp.maximum(lses[0], lses[1]), lses[2])
    es = [jnp.exp(l - m) for l in lses]
    inv = 1.0 / (es[0] + es[1] + es[2])
    merged = (es[0] * inv) * outs[0] + (es[1] * inv) * outs[1] + (es[2] * inv) * outs[2]
    return merged.reshape(GROUP_W, 1)


def _mlp_kernel(x_ref, attn_ref, gpre_ref, wlo_ref, whi_ref, bm_ref, vng_ref, vnb_ref, ws_ref, bsp_ref,
                wpa_ref, wpb_ref, wout_ref, gpost_ref, qt_ref, kt1_ref, kt2_ref, kt3_ref, c1_ref, c2_ref, c3_ref,
                out_ref, sattn_ref, sacc_s):
    step = pl.program_id(0)

    @pl.when(step == 0)
    def _():
        sacc_s[...] = jnp.zeros(sacc_s.shape, F32)

    lane = lax.broadcasted_iota(jnp.int32, (1, sacc_s.shape[-1]), 1)
    for k in range(c1_ref.shape[0]):
        sel = lane == step * c1_ref.shape[0] + k
        pick = lambda ref: jnp.sum(jnp.where(sel, ref[...], 0.0), axis=-1, keepdims=True)
        res = _sample_attend(pick(qt_ref), [pick(r) for r in (kt1_ref, kt2_ref, kt3_ref)],
                             [(c[k, 0], c[k, 1]) for c in (c1_ref, c2_ref, c3_ref)])
        sacc_s[...] = jnp.where(sel, res, sacc_s[...])
    sattn_ref[...] = sacc_s[...]

    x = x_ref[...]
    rows = x.shape[0]
    hb = _rms(x, gpre_ref[...]).astype(BF16)
    z = jnp.concatenate([jnp.dot(hb, wlo_ref[...], preferred_element_type=F32),
                         jnp.dot(hb, whi_ref[...], preferred_element_type=F32)], axis=-1)
    ga, u, vn, gb, gates = _activations(z, bm_ref[...], vng_ref[...], vnb_ref[...])
    wr = lax.broadcasted_iota(jnp.int32, ws_ref.shape, 0) & (BLK - 1)
    wc = lax.broadcasted_iota(jnp.int32, ws_ref.shape, 1)
    ws = jnp.where(wc <= wr, ws_ref[...], 0.0).astype(BF16)
    vnb16 = vn.astype(BF16)
    mix = jnp.concatenate([_spatial_mix(ws, vnb16[c * BLK:(c + 1) * BLK]) + bsp_ref[...]
                           for c in range(rows // BLK)], axis=0)
    out_ref[...] = _merge_out(x, attn_ref[...] * ga, (u * mix) * gb, gates,
                              wpa_ref[...], wpb_ref[...], wout_ref[...], gpost_ref[...])


def _mlp_call(layer, x, attn, g_pre, w_in, b_merge, vng, vnb, ws_stack, bs_full, wpa, wpb, wout, g_post,
              sample_q, sample_kv, caches):
    n, d = x.shape
    t = MLP_TILE
    steps = n // t
    nb = sample_q.shape[-1]
    assert nb % steps == 0, "every grid step carries the same number of samples"
    spb = nb // steps
    row = lambda i: (i, 0)
    rest_w = w_in.shape[-1] - 3 * QKV_W
    assert rest_w == 2 * 3 * QKV_W, "rest columns are addressed as column blocks 1 and 2 of width 3 * QKV_W"
    w_specs = [_layer_spec(w_in, layer, 3 * QKV_W, 1), _layer_spec(w_in, layer, 3 * QKV_W, 2)]
    stacks = (b_merge, vng, vnb, ws_stack, bs_full, wpa, wpb, wout, g_post)
    cache_specs, cache_views = [], []
    for (window, dil), c in zip(ATTN_GROUPS, caches):
        rows = c.shape[2]
        assert rows == window and window % dil == 0 and dil & (dil - 1) == 0, "cache must hold exactly the window"
        cache_views.append(jnp.transpose(c, (0, 1, 3, 4, 5, 2)))
        cache_specs.append(pl.BlockSpec((None, spb, 2, HEADS_PER_GROUP, HEAD_DIM, rows),
                                        lambda i: (layer, i, 0, 0, 0, 0)))
    return pl.pallas_call(
        _mlp_kernel,
        grid=(steps,),
        in_specs=[pl.BlockSpec((t, d), row), pl.BlockSpec((t, GROUP_W), row), _layer_spec(g_pre, layer)] + w_specs
        + [_layer_spec(s, layer) for s in stacks]
        + [_const_spec(sample_q.shape)] + [_const_spec(kv.shape) for kv in sample_kv] + cache_specs,
        out_specs=[pl.BlockSpec((t, d), row), _const_spec((GROUP_W, nb))],
        out_shape=[jax.ShapeDtypeStruct((n, d), F32), jax.ShapeDtypeStruct((GROUP_W, nb), F32)],
        scratch_shapes=[pltpu.VMEM((GROUP_W, nb), F32)],
        compiler_params=pltpu.CompilerParams(dimension_semantics=("arbitrary",), vmem_limit_bytes=VMEM_LIMIT),
        name="prompt_mlp",
    )(x, attn, g_pre, w_in, w_in, *stacks, sample_q, *sample_kv, *cache_views)


def _sample_in_kernel(x_ref, g_ref, w_ref, cq_ref, saq_ref, sbq_ref, ck_ref, sak_ref, sbk_ref,
                      bm_ref, vng_ref, vnb_ref,
                      kv1_ref, kv2_ref, kv3_ref, qt_ref, kvt1_ref, kvt2_ref, kvt3_ref,
                      ga_ref, u_ref, vn_ref, gb_ref, gates_ref):
    hb = _rms(x_ref[...], g_ref[...]).astype(BF16)
    z = jnp.dot(hb, w_ref[...], preferred_element_type=F32)

    def store_tiles(dst_ref, rows):
        dst_ref[...] = rows.T

    qs = []
    for g, (kv_ref, kvt_ref) in enumerate(((kv1_ref, kvt1_ref), (kv2_ref, kvt2_ref), (kv3_ref, kvt3_ref))):
        lo = g * GROUP_W
        qs.append(_rope(z[:, lo:lo + GROUP_W], cq_ref[...], saq_ref[...], sbq_ref[...]))
        k = _rope(z[:, QKV_W + lo:QKV_W + lo + GROUP_W], ck_ref[...], sak_ref[...], sbk_ref[...])
        kv = jnp.concatenate([k, z[:, 2 * QKV_W + lo:2 * QKV_W + lo + GROUP_W]], axis=-1)
        kv_ref[...] = kv
        store_tiles(kvt_ref, kv)
    store_tiles(qt_ref, jnp.concatenate(qs, axis=-1))
    ga, u, vn, gb, gates = _activations(z[:, 3 * QKV_W:], bm_ref[...], vng_ref[...], vnb_ref[...])
    ga_ref[...] = ga
    u_ref[...] = u
    vn_ref[...] = vn
    gb_ref[...] = gb
    gates_ref[...] = gates


def _sample_in_call(layer, x, g_pre, w_in, tabs_q, tabs_k, b_merge, vng, vnb):
    n, d = x.shape
    shapes = ([(n, 2 * GROUP_W)] * 3 + [(QKV_W, n)] + [(2 * GROUP_W, n)] * 3
              + [(n, w) for w in (GROUP_W, GMLP_W, GMLP_W, GMLP_W, 2 * d)])
    tabs = (*tabs_q, *tabs_k)
    return pl.pallas_call(
        _sample_in_kernel,
        grid=(1,),
        in_specs=[_const_spec(x.shape), _layer_spec(g_pre, layer), _layer_spec(w_in, layer)]
        + [_const_spec(t.shape) for t in tabs] + [_layer_spec(s, layer) for s in (b_merge, vng, vnb)],
        out_specs=[_const_spec(s) for s in shapes],
        out_shape=[jax.ShapeDtypeStruct(s, F32) for s in shapes],
        compiler_params=pltpu.CompilerParams(dimension_semantics=("arbitrary",), vmem_limit_bytes=VMEM_LIMIT),
        name="sample_in",
    )(x, g_pre, w_in, *tabs, b_merge, vng, vnb)


def _sample_out_kernel(x_ref, attn_ref, ga_ref, u_ref, vn_ref, gb_ref, gates_ref, ws0_ref, bs0_ref,
                       wpa_ref, wpb_ref, wout_ref, gpost_ref, out_ref):
    mix = vn_ref[...] * ws0_ref[...] + bs0_ref[...]
    attn = attn_ref[...].T
    out_ref[...] = _merge_out(x_ref[...], attn * ga_ref[...], (u_ref[...] * mix) * gb_ref[...],
                              gates_ref[...], wpa_ref[...], wpb_ref[...], wout_ref[...], gpost_ref[...])


def _sample_out_call(layer, x, attn, ga, u, vn, gb, gates, ws0, bs0, wpa, wpb, wout, g_post):
    acts = (x, attn, ga, u, vn, gb, gates)
    stacks = (ws0, bs0, wpa, wpb, wout, g_post)
    return pl.pallas_call(
        _sample_out_kernel,
        grid=(1,),
        in_specs=[_const_spec(a.shape) for a in acts] + [_layer_spec(s, layer) for s in stacks],
        out_specs=_const_spec(x.shape),
        out_shape=jax.ShapeDtypeStruct(x.shape, F32),
        compiler_params=pltpu.CompilerParams(dimension_semantics=("arbitrary",), vmem_limit_bytes=VMEM_LIMIT),
        name="sample_out",
    )(*acts, *stacks)


def kernel(x_prompt, x_sample, cache_kv_w128, cache_kv_w512, cache_kv_w2048, norm_pre, w_in, b_merge, v_norm_g, v_norm_b, w_spatial, b_spatial, w_proj_a, w_proj_b, w_out, norm_post):
    batch, seq, d = x_prompt.shape
    nb, dec_seq, _ = x_sample.shape
    depth = w_in.shape[0]
    assert dec_seq == 1, "the sample group decodes one position per step"
    assert seq % QKV_TILE == 0 and (batch * seq) % MLP_TILE == 0
    caches = (cache_kv_w128, cache_kv_w512, cache_kv_w2048)

    w_in_b = w_in.astype(BF16)
    wpa_b, wpb_b, wout_b = w_proj_a.astype(BF16), w_proj_b.astype(BF16), w_out.astype(BF16)
    ws_stack = w_spatial.reshape(depth, GMLP_GROUPS * BLK, BLK)
    chan_group = jnp.arange(GMLP_W) // GMLP_GROUP_DIM
    bs_full = jnp.swapaxes(b_spatial, 1, 2)[:, :, chan_group]
    ws0 = w_spatial[:, :, 0, 0][:, chan_group][:, None, :]
    bs0 = b_spatial[:, :, 0][:, chan_group][:, None, :]
    row2 = lambda a: a[:, None, :]
    g_pre, g_post, bm, vng, vnb = row2(norm_pre), row2(norm_post), row2(b_merge), row2(v_norm_g), row2(v_norm_b)

    scale = HEAD_DIM ** -0.5
    pos_p = jnp.arange(seq, dtype=jnp.int32)
    pos_s = PAST_LEN + jnp.arange(dec_seq, dtype=jnp.int32)
    t_p = _rope_tables_packed(pos_p)
    tq_s, tk_s = _rope_tables(pos_s, scale), _rope_tables(pos_s, 1.0)
    order = jnp.asarray(_packed_order())
    slab_scale = jnp.asarray([scale] * N_GROUPS + [1.0] * N_GROUPS, F32)[None, None, :, None]
    w_qk = w_in_b[:, :, :2 * QKV_W].reshape(depth, d, 2 * N_GROUPS, GROUP_W)
    w_qk = (jnp.take(w_qk, order, axis=-1) * slab_scale).astype(BF16).reshape(depth, d, 2 * QKV_W)
    w_qkv_p = jnp.concatenate([w_qk, w_in_b[:, :, 2 * QKV_W:3 * QKV_W]], axis=-1)
    band, causal = _attn_bias_tables()

    xp = x_prompt
    xs = x_sample.reshape(nb, d)
    kv_p = None
    kv_s = [[] for _ in ATTN_GROUPS]
    v_s = []
    for l in range(depth):
        q1, kv1, q2, kv2, q3, kv3, *kv_p = _qkv_call(l, xp, g_pre, w_qkv_p, t_p, kv_p)
        attn = _attn_call(q1, q2, q3, kv1, kv2, kv3, band, causal, batch, seq)
        kn1, kn2, kn3, qt, kt1, kt2, kt3, ga, u, vn, gb, gates = _sample_in_call(l, xs, g_pre, w_in_b, tq_s, tk_s,
                                                                                 bm, vng, vnb)
        xp, attn_s = _mlp_call(l, xp.reshape(batch * seq, d), attn.reshape(batch * seq, GROUP_W), g_pre, w_in_b,
                               bm, vng, vnb, ws_stack, bs_full, wpa_b, wpb_b, wout_b, g_post,
                               qt, (kt1, kt2, kt3), caches)
        xp = xp.reshape(batch, seq, d)
        xs = _sample_out_call(l, xs, attn_s, ga, u, vn, gb, gates, ws0, bs0, wpa_b, wpb_b, wout_b, g_post)
        for g, kn in enumerate((kn1, kn2, kn3)):
            kv_s[g].append(kn.reshape(nb, dec_seq, 2, HEADS_PER_GROUP, HEAD_DIM))
        v_s.append(vn.reshape(nb, dec_seq, GMLP_W))

    new_kv_p = [jnp.transpose(p, (0, 1, 5, 2, 3, 4)) for p in kv_p]
    return (xp, xs.reshape(nb, dec_seq, d), new_kv_p[0], new_kv_p[1], new_kv_p[2],
            jnp.stack(kv_s[0]), jnp.stack(kv_s[1]), jnp.stack(kv_s[2]), jnp.stack(v_s))
```

```python
import functools

import jax
import jax.numpy as jnp
from jax import lax
from jax.experimental import pallas as pl
from jax.experimental.pallas import tpu as pltpu

F32 = jnp.float32
BF16 = jnp.bfloat16

PAST_LEN = 8192
HEAD_DIM = 64
HEADS_PER_GROUP = 4
GROUP_W = HEADS_PER_GROUP * HEAD_DIM
ATTN_GROUPS = ((128, 1), (512, 4), (2048, 16))
N_GROUPS = len(ATTN_GROUPS)
QKV_W = N_GROUPS * GROUP_W
ROT_DIM = HEAD_DIM // 4
ROT_HALF = ROT_DIM // 2
ROPE_THETA = 500000.0
BLK = 128
GMLP_GROUPS = 4
GMLP_GROUP_DIM = 192
GMLP_W = GMLP_GROUPS * GMLP_GROUP_DIM
EPS = 1e-6
LANES = 128
NEG = -1e30

OFF_GA = 0
OFF_U = OFF_GA + GROUP_W
OFF_VB = OFF_U + GMLP_W
OFF_GB = OFF_VB + GMLP_W
OFF_ML = OFF_GB + GMLP_W

QKV_TILE = 512
MLP_TILE = 512
VMEM_LIMIT = 56 * 1024 * 1024


def _rms(x, g):
    return (x * lax.rsqrt(jnp.mean(x * x, axis=-1, keepdims=True) + EPS)) * g


def _layer_norm(x, g, b):
    xc = x - jnp.mean(x, axis=-1, keepdims=True)
    return xc * lax.rsqrt(jnp.mean(xc * xc, axis=-1, keepdims=True) + EPS) * g + b


def _rope(z, c, sa, sb):
    outs = []
    for s in range(GROUP_W // LANES):
        zs = z[:, s * LANES:(s + 1) * LANES]
        outs.append(zs * c + pltpu.roll(zs, LANES - ROT_HALF, 1) * sa + pltpu.roll(zs, ROT_HALF, 1) * sb)
    return jnp.concatenate(outs, axis=-1)


def _rope_tables(pos, scale):
    inv_freq = jnp.power(ROPE_THETA, -jnp.arange(ROT_HALF, dtype=F32) / ROT_HALF)
    ang = pos.astype(F32)[:, None] * inv_freq[None, :]
    cos, sin = jnp.cos(ang), jnp.sin(ang)
    n = pos.shape[0]
    rest1 = jnp.ones((n, HEAD_DIM - ROT_DIM), F32)
    rest0 = jnp.zeros((n, HEAD_DIM - ROT_DIM), F32)
    z8 = jnp.zeros((n, ROT_HALF), F32)
    c = jnp.concatenate([cos, cos, rest1], axis=-1)
    sa = jnp.concatenate([-sin, z8, rest0], axis=-1)
    sb = jnp.concatenate([z8, sin, rest0], axis=-1)
    reps = LANES // HEAD_DIM
    return tuple(jnp.tile(t, (1, reps)) * scale for t in (c, sa, sb))


def _activations(z, b_merge, vng, vnb):
    ga = jax.nn.silu(z[:, OFF_GA:OFF_U])
    u = jax.nn.gelu(z[:, OFF_U:OFF_VB])
    vn = _layer_norm(jax.nn.gelu(z[:, OFF_VB:OFF_GB]), vng, vnb)
    gb = jax.nn.silu(z[:, OFF_GB:OFF_ML])
    gates = jax.nn.sigmoid(z[:, OFF_ML:] + b_merge)
    return ga, u, vn, gb, gates


def _merge_out(x, a_in, b_in, gates, wpa, wpb, wout, gpost):
    d = x.shape[-1]
    ba = jnp.dot(a_in.astype(BF16), wpa, preferred_element_type=F32)
    bb = jnp.dot(b_in.astype(BF16), wpb, preferred_element_type=F32)
    merged = gates[:, :d] * ba + gates[:, d:] * bb
    y = jnp.dot(merged.astype(BF16), wout, preferred_element_type=F32)
    return x + _rms(y, gpost)


def _const_spec(shape):
    nd = len(shape)
    return pl.BlockSpec(shape, lambda *_: (0,) * nd)


def _layer_spec(arr, layer, block_cols=None, col_block=0):
    rows, cols = arr.shape[1:]
    return pl.BlockSpec((None, rows, cols if block_cols is None else block_cols), lambda *_: (layer, 0, col_block))


def _qkv_kernel(seq, first_layer, x_ref, g_ref, w_ref, cq_ref, saq_ref, sbq_ref, ck_ref, sak_ref, sbk_ref, *refs):
    if not first_layer:
        refs = refs[N_GROUPS:]
    q1_ref, kv1_ref, q2_ref, kv2_ref, q3_ref, kv3_ref, p1_ref, p2_ref, p3_ref, za_s, zb_s, dei_s = refs
    step = pl.program_id(0)
    t = x_ref.shape[0]
    tiles_per_seq = seq // t
    last_tile = (jnp.maximum(step - 1, 0) % tiles_per_seq) == tiles_per_seq - 1
    q_refs = (q1_ref, q2_ref, q3_ref)
    kv_refs = (kv1_ref, kv2_ref, kv3_ref)
    p_refs = (p1_ref, p2_ref, p3_ref)
    nslab = GROUP_W // LANES

    @pl.when(step == 0)
    def _():
        zb_s[...] = jnp.zeros(zb_s.shape, F32)

    def store_transposed(p_ref, idx, z):
        zt = z.T.reshape(HEADS_PER_GROUP, HEAD_DIM, z.shape[0])
        if first_layer:
            p_ref[0, idx] = zt
            if p_ref.shape[0] > 1:
                p_ref[1:, idx] = jnp.zeros((p_ref.shape[0] - 1,) + zt.shape, F32)
        else:
            p_ref[idx] = zt

    def rope_k(z, g, rows=slice(None)):
        return _rope(z[rows, QKV_W + g * GROUP_W:QKV_W + (g + 1) * GROUP_W], ck_ref[rows, :], sak_ref[rows, :], sbk_ref[rows, :])

    def body(z_new, z):
        hb = _rms(x_ref[...], g_ref[...]).astype(BF16)
        z_new[...] = jnp.dot(hb, w_ref[...], preferred_element_type=F32)
        slots = iter(range(dei_s.shape[0]))

        def split_by_residue(v, dst_ref, lane0, dil):
            slot = next(slots)
            for s in range(nslab):
                dei_s[slot, s] = v[:, s * LANES:(s + 1) * LANES]
            for r in range(dil):
                rows = [dei_s[slot, s, pl.ds(r, t // dil, stride=dil), :] for s in range(nslab)]
                dst_ref[r, :, lane0:lane0 + GROUP_W] = jnp.concatenate(rows, axis=-1).astype(BF16)

        for g, (window, dil) in enumerate(ATTN_GROUPS):
            zq = _rope(z[:, g * GROUP_W:(g + 1) * GROUP_W], cq_ref[...], saq_ref[...], sbq_ref[...])
            zk = rope_k(z, g)
            zv = z[:, 2 * QKV_W + g * GROUP_W:2 * QKV_W + (g + 1) * GROUP_W]
            if dil == 1:
                q_refs[g][...] = zq.astype(BF16)
                kv_refs[g][:, 0:GROUP_W] = zk.astype(BF16)
                kv_refs[g][:, GROUP_W:2 * GROUP_W] = zv.astype(BF16)
            else:
                split_by_residue(zq, q_refs[g], 0, dil)
                split_by_residue(zk, kv_refs[g], 0, dil)
                split_by_residue(zv, kv_refs[g], GROUP_W, dil)
            if window >= seq:
                store_transposed(p_refs[g], 0, zk)
                store_transposed(p_refs[g], 1, zv)

        @pl.when(last_tile)
        def _():
            for g, (window, dil) in enumerate(ATTN_GROUPS):
                if window < seq:
                    tail = slice(t - p_refs[g].shape[-1], t)
                    store_transposed(p_refs[g], 0, rope_k(z, g, tail))
                    store_transposed(p_refs[g], 1, z[tail, 2 * QKV_W + g * GROUP_W:2 * QKV_W + (g + 1) * GROUP_W])

    @pl.when(step % 2 == 0)
    def _():
        body(za_s, zb_s)

    @pl.when(step % 2 == 1)
    def _():
        body(zb_s, za_s)


def _qkv_call(layer, x, g_pre, w_in, tabs, accs):
    batch, seq, d = x.shape
    depth = w_in.shape[0]
    first_layer = accs is None
    t = QKV_TILE
    nt = seq // t
    n_tiles = batch * nt
    proj = lambda s: jnp.minimum(s, n_tiles - 1)
    fin = lambda s: jnp.maximum(s - 1, 0)
    tab_spec = pl.BlockSpec((t, LANES), lambda s: (fin(s) % nt, 0))
    in_specs = [pl.BlockSpec((None, t, d), lambda s: (proj(s) // nt, proj(s) % nt, 0)), _layer_spec(g_pre, layer),
                _layer_spec(w_in, layer, 3 * QKV_W, 0)]
    out_specs, out_shape = [], []
    for window, dil in ATTN_GROUPS:
        assert t % (dil * 16) == 0
        if dil == 1:
            q_blk, kv_blk = (None, t, GROUP_W), (None, t, 2 * GROUP_W)
            q_shape, kv_shape = (batch, seq, GROUP_W), (batch, seq, 2 * GROUP_W)
            idx = lambda s: (fin(s) // nt, fin(s) % nt, 0)
        else:
            q_blk, kv_blk = (None, dil, t // dil, GROUP_W), (None, dil, t // dil, 2 * GROUP_W)
            q_shape, kv_shape = (batch, dil, seq // dil, GROUP_W), (batch, dil, seq // dil, 2 * GROUP_W)
            idx = lambda s: (fin(s) // nt, 0, fin(s) % nt, 0)
        out_specs += [pl.BlockSpec(q_blk, idx), pl.BlockSpec(kv_blk, idx)]
        out_shape += [jax.ShapeDtypeStruct(q_shape, BF16), jax.ShapeDtypeStruct(kv_shape, BF16)]
    for window, dil in ATTN_GROUPS:
        keep = min(window, seq)
        assert keep == seq or keep <= t, "a partial window must fit in the last tile"
        blk = min(keep, t)
        every_tile = keep == seq
        lead = 0 if first_layer else layer
        p_blk = (depth if first_layer else None, None, 2, HEADS_PER_GROUP, HEAD_DIM, blk)
        p_idx = lambda s, every_tile=every_tile: (lead, fin(s) // nt, 0, 0, 0, fin(s) % nt if every_tile else 0)
        out_specs.append(pl.BlockSpec(p_blk, p_idx))
        out_shape.append(jax.ShapeDtypeStruct((depth, batch, 2, HEADS_PER_GROUP, HEAD_DIM, keep), F32))
    n_in = len(in_specs) + len(tabs)
    acc_specs = [] if first_layer else [pl.BlockSpec(memory_space=pl.ANY)] * N_GROUPS
    aliases = {} if first_layer else {n_in + g: 2 * N_GROUPS + g for g in range(N_GROUPS)}
    n_split = 3 * sum(dil > 1 for _, dil in ATTN_GROUPS)
    return pl.pallas_call(
        functools.partial(_qkv_kernel, seq, first_layer),
        grid=(n_tiles + 1,),
        in_specs=in_specs + [tab_spec] * len(tabs) + acc_specs,
        out_specs=out_specs,
        out_shape=out_shape,
        input_output_aliases=aliases,
        scratch_shapes=[pltpu.VMEM((t, 3 * QKV_W), F32), pltpu.VMEM((t, 3 * QKV_W), F32),
                        pltpu.VMEM((n_split, GROUP_W // LANES, t, LANES), F32)],
        compiler_params=pltpu.CompilerParams(dimension_semantics=("arbitrary",), vmem_limit_bytes=VMEM_LIMIT),
        name="prompt_qkv",
    )(x, g_pre, w_in, *tabs, *(() if first_layer else accs))


def _attn_block(q, k, v, bias):
    nk = k.shape[0]
    hid = lax.shift_right_logical(lax.broadcasted_iota(jnp.int32, (BLK, GROUP_W), 1), HEAD_DIM.bit_length() - 1)
    qf = q.astype(F32)
    qs = jnp.concatenate([jnp.where(hid == h, qf, 0.0) for h in range(HEADS_PER_GROUP)], axis=0).astype(BF16)
    s = lax.dot_general(qs, k, (((1,), (1,)), ((), ())), preferred_element_type=F32)
    s = s.reshape(HEADS_PER_GROUP, BLK, nk) + bias[None]
    m = jnp.max(s, axis=-1, keepdims=True)
    p = jnp.exp(s - m)
    l = jnp.sum(p, axis=-1, keepdims=True)
    pv = jnp.dot(p.reshape(HEADS_PER_GROUP * BLK, nk).astype(BF16), v, preferred_element_type=F32)
    pv = pv.reshape(HEADS_PER_GROUP, BLK, GROUP_W)
    inv = 1.0 / l
    lse = m + jnp.log(l)
    o = jnp.zeros((BLK, GROUP_W), F32)
    ls = jnp.zeros((BLK, GROUP_W), F32)
    for h in range(HEADS_PER_GROUP):
        o = jnp.where(hid == h, pv[h] * inv[h], o)
        ls = jnp.where(hid == h, lse[h], ls)
    return o, ls


def _attn_kernel(q1_ref, q2_ref, q3_ref, kv1_ref, kv2_ref, kv3_ref, band_ref, causal_ref, out_ref,
                 o1_s, l1_s, o2_s, l2_s, o3_s, l3_s):
    seq = out_ref.shape[0]
    nslab = GROUP_W // LANES
    d2 = ATTN_GROUPS[1][1]
    d3 = ATTN_GROUPS[2][1]
    nblk = seq // BLK
    blocks_per_res2 = nblk // d2
    kcols, vcols = slice(0, GROUP_W), slice(GROUP_W, 2 * GROUP_W)

    def store(o_s, l_s, rows, o, ls):
        for s in range(nslab):
            o_s[s, rows, :] = o[:, s * LANES:(s + 1) * LANES]
            l_s[s, rows, :] = ls[:, s * LANES:(s + 1) * LANES]

    def band_block(q, kv, n):
        if n == 0:
            return _attn_block(q, kv[0:BLK, kcols], kv[0:BLK, vcols], causal_ref[...])
        keys = slice((n - 1) * BLK, (n + 1) * BLK)
        return _attn_block(q, kv[keys, kcols], kv[keys, vcols], band_ref[...])

    for j in range(nblk):
        o, ls = band_block(q1_ref[j * BLK:(j + 1) * BLK, :], kv1_ref, j)
        store(o1_s, l1_s, slice(j * BLK, (j + 1) * BLK), o, ls)
        r, n = divmod(j, blocks_per_res2)
        o, ls = band_block(q2_ref[r, n * BLK:(n + 1) * BLK, :], kv2_ref.at[r], n)
        store(o2_s, l2_s, pl.ds(n * (BLK * d2) + r, BLK, stride=d2), o, ls)
        o, ls = _attn_block(q3_ref[j], kv3_ref[j, :, kcols], kv3_ref[j, :, vcols], causal_ref[...])
        store(o3_s, l3_s, pl.ds(j, BLK, stride=d3), o, ls)

    rows_per_step = 256
    for s in range(nslab):
        def body(c, carry):
            rows = pl.ds(pl.multiple_of(c * rows_per_step, rows_per_step), rows_per_step)
            l1, l2, l3 = l1_s[s, rows, :], l2_s[s, rows, :], l3_s[s, rows, :]
            m = jnp.maximum(jnp.maximum(l1, l2), l3)
            e1, e2, e3 = jnp.exp(l1 - m), jnp.exp(l2 - m), jnp.exp(l3 - m)
            inv = 1.0 / (e1 + e2 + e3)
            out = (e1 * inv) * o1_s[s, rows, :] + (e2 * inv) * o2_s[s, rows, :] + (e3 * inv) * o3_s[s, rows, :]
            out_ref[rows, s * LANES:(s + 1) * LANES] = out
            return carry
        lax.fori_loop(0, seq // rows_per_step, body, 0)


def _attn_call(q1, q2, q3, kv1, kv2, kv3, band, causal, batch, seq):
    d2 = ATTN_GROUPS[1][1]
    d3 = ATTN_GROUPS[2][1]
    nblk = seq // BLK
    assert seq // d3 == BLK and nblk % d2 == 0 and nblk == d3
    per_seq = lambda a: pl.BlockSpec((None,) + a.shape[1:], lambda b: (b,) + (0,) * (a.ndim - 1))
    scratch = [pltpu.VMEM((GROUP_W // LANES, seq, LANES), F32)] * 6
    return pl.pallas_call(
        _attn_kernel,
        grid=(batch,),
        in_specs=[per_seq(a) for a in (q1, q2, q3, kv1, kv2, kv3)] + [_const_spec(band.shape), _const_spec(causal.shape)],
        out_specs=pl.BlockSpec((None, seq, GROUP_W), lambda b: (b, 0, 0)),
        out_shape=jax.ShapeDtypeStruct((batch, seq, GROUP_W), F32),
        scratch_shapes=scratch,
        compiler_params=pltpu.CompilerParams(dimension_semantics=("arbitrary",), vmem_limit_bytes=VMEM_LIMIT),
        name="prompt_attn",
    )(q1, q2, q3, kv1, kv2, kv3, band, causal)


def _attn_bias_tables():
    qi = jnp.arange(BLK)[:, None]
    kj = jnp.arange(2 * BLK)[None, :]
    band = jnp.where((kj >= qi) & (kj <= qi + BLK), 0.0, NEG).astype(F32)
    causal = jnp.where(jnp.arange(BLK)[None, :] <= qi, 0.0, NEG).astype(F32)
    return band, causal


def _spatial_mix(ws, v):
    gd, two = GMLP_GROUP_DIM, 2 * LANES
    starts = [g * gd // LANES * LANES for g in range(GMLP_GROUPS)]
    assert all(s + two >= (g + 1) * gd for g, s in enumerate(starts))
    r = [jnp.dot(ws[g * BLK:(g + 1) * BLK], v[:, s:s + two], preferred_element_type=F32) for g, s in enumerate(starts)]
    lane = lax.broadcasted_iota(jnp.int32, (BLK, LANES), 1)
    slabs = []
    for lo in range(0, GMLP_W, LANES):
        owners = sorted({lo // gd, (lo + LANES - 1) // gd})
        pieces = [r[g][:, lo - starts[g]:lo - starts[g] + LANES] for g in owners]
        slabs.append(pieces[0] if len(owners) == 1 else jnp.where(lane < owners[1] * gd - lo, pieces[0], pieces[1]))
    return jnp.concatenate(slabs, axis=-1)


def _sample_attend(q, kvn, caches):
    column = lambda v, lo: v[lo:lo + GROUP_W].reshape(HEADS_PER_GROUP, HEAD_DIM, 1)
    outs, lses = [], []
    for g, (window, dil) in enumerate(ATTN_GROUPS):
        qg = column(q, g * GROUP_W)
        kc, vc = caches[g]
        kn, vn = column(kvn[g], 0), column(kvn[g], GROUP_W)
        rows = kc.shape[-1]
        sc = jnp.sum(kc * qg, axis=1, keepdims=True)
        r = lax.broadcasted_iota(jnp.int32, (1, 1, rows), 2)
        sc = jnp.where((r & (dil - 1)) == 0, sc, NEG)
        ss = jnp.sum(kn * qg, axis=1, keepdims=True)
        m = jnp.maximum(jnp.max(sc, axis=-1, keepdims=True), ss)
        pc = jnp.exp(sc - m)
        ps = jnp.exp(ss - m)
        l = jnp.sum(pc, axis=-1, keepdims=True) + ps
        o = jnp.sum(vc * pc, axis=-1, keepdims=True) + ps * vn
        outs.append(o / l)
        lses.append(m + jnp.log(l))
    m = jnp.maximum(jnp.maximum(lses[0], lses[1]), lses[2])
    es = [jnp.exp(l - m) for l in lses]
    inv = 1.0 / (es[0] + es[1] + es[2])
    merged = (es[0] * inv) * outs[0] + (es[1] * inv) * outs[1] + (es[2] * inv) * outs[2]
    return merged.reshape(GROUP_W, 1)


def _mlp_kernel(x_ref, attn_ref, gpre_ref, wlo_ref, whi_ref, bm_ref, vng_ref, vnb_ref, ws_ref, bsp_ref,
                wpa_ref, wpb_ref, wout_ref, gpost_ref, qt_ref, kt1_ref, kt2_ref, kt3_ref, c1_ref, c2_ref, c3_ref,
                out_ref, sattn_ref, sacc_s):
    step = pl.program_id(0)

    @pl.when(step == 0)
    def _():
        sacc_s[...] = jnp.zeros(sacc_s.shape, F32)

    lane = lax.broadcasted_iota(jnp.int32, (1, sacc_s.shape[-1]), 1)
    for k in range(c1_ref.shape[0]):
        sel = lane == step * c1_ref.shape[0] + k
        pick = lambda ref: jnp.sum(jnp.where(sel, ref[...], 0.0), axis=-1, keepdims=True)
        res = _sample_attend(pick(qt_ref), [pick(r) for r in (kt1_ref, kt2_ref, kt3_ref)],
                             [(c[k, 0], c[k, 1]) for c in (c1_ref, c2_ref, c3_ref)])
        sacc_s[...] = jnp.where(sel, res, sacc_s[...])
    sattn_ref[...] = sacc_s[...]

    x = x_ref[...]
    rows = x.shape[0]
    hb = _rms(x, gpre_ref[...]).astype(BF16)
    z = jnp.concatenate([jnp.dot(hb, wlo_ref[...], preferred_element_type=F32),
                         jnp.dot(hb, whi_ref[...], preferred_element_type=F32)], axis=-1)
    ga, u, vn, gb, gates = _activations(z, bm_ref[...], vng_ref[...], vnb_ref[...])
    wr = lax.broadcasted_iota(jnp.int32, ws_ref.shape, 0) & (BLK - 1)
    wc = lax.broadcasted_iota(jnp.int32, ws_ref.shape, 1)
    ws = jnp.where(wc <= wr, ws_ref[...], 0.0).astype(BF16)
    vnb16 = vn.astype(BF16)
    mix = jnp.concatenate([_spatial_mix(ws, vnb16[c * BLK:(c + 1) * BLK]) + bsp_ref[...]
                           for c in range(rows // BLK)], axis=0)
    out_ref[...] = _merge_out(x, attn_ref[...] * ga, (u * mix) * gb, gates,
                              wpa_ref[...], wpb_ref[...], wout_ref[...], gpost_ref[...])


def _mlp_call(layer, x, attn, g_pre, w_in, b_merge, vng, vnb, ws_stack, bs_full, wpa, wpb, wout, g_post,
              sample_q, sample_kv, caches):
    n, d = x.shape
    t = MLP_TILE
    steps = n // t
    nb = sample_q.shape[-1]
    assert nb % steps == 0, "every grid step carries the same number of samples"
    spb = nb // steps
    row = lambda i: (i, 0)
    rest_w = w_in.shape[-1] - 3 * QKV_W
    assert rest_w == 2 * 3 * QKV_W, "rest columns are addressed as column blocks 1 and 2 of width 3 * QKV_W"
    w_specs = [_layer_spec(w_in, layer, 3 * QKV_W, 1), _layer_spec(w_in, layer, 3 * QKV_W, 2)]
    stacks = (b_merge, vng, vnb, ws_stack, bs_full, wpa, wpb, wout, g_post)
    cache_specs, cache_views = [], []
    for (window, dil), c in zip(ATTN_GROUPS, caches):
        rows = c.shape[2]
        assert rows == window and window % dil == 0 and dil & (dil - 1) == 0, "cache must hold exactly the window"
        cache_views.append(jnp.transpose(c, (0, 1, 3, 4, 5, 2)))
        cache_specs.append(pl.BlockSpec((None, spb, 2, HEADS_PER_GROUP, HEAD_DIM, rows),
                                        lambda i: (layer, i, 0, 0, 0, 0)))
    return pl.pallas_call(
        _mlp_kernel,
        grid=(steps,),
        in_specs=[pl.BlockSpec((t, d), row), pl.BlockSpec((t, GROUP_W), row), _layer_spec(g_pre, layer)] + w_specs
        + [_layer_spec(s, layer) for s in stacks]
        + [_const_spec(sample_q.shape)] + [_const_spec(kv.shape) for kv in sample_kv] + cache_specs,
        out_specs=[pl.BlockSpec((t, d), row), _const_spec((GROUP_W, nb))],
        out_shape=[jax.ShapeDtypeStruct((n, d), F32), jax.ShapeDtypeStruct((GROUP_W, nb), F32)],
        scratch_shapes=[pltpu.VMEM((GROUP_W, nb), F32)],
        compiler_params=pltpu.CompilerParams(dimension_semantics=("arbitrary",), vmem_limit_bytes=VMEM_LIMIT),
        name="prompt_mlp",
    )(x, attn, g_pre, w_in, w_in, *stacks, sample_q, *sample_kv, *cache_views)


def _sample_in_kernel(x_ref, g_ref, w_ref, cq_ref, saq_ref, sbq_ref, ck_ref, sak_ref, sbk_ref,
                      bm_ref, vng_ref, vnb_ref,
                      kv1_ref, kv2_ref, kv3_ref, qt_ref, kvt1_ref, kvt2_ref, kvt3_ref,
                      ga_ref, u_ref, vn_ref, gb_ref, gates_ref):
    hb = _rms(x_ref[...], g_ref[...]).astype(BF16)
    z = jnp.dot(hb, w_ref[...], preferred_element_type=F32)

    def store_tiles(dst_ref, rows):
        dst_ref[...] = rows.T

    qs = []
    for g, (kv_ref, kvt_ref) in enumerate(((kv1_ref, kvt1_ref), (kv2_ref, kvt2_ref), (kv3_ref, kvt3_ref))):
        lo = g * GROUP_W
        qs.append(_rope(z[:, lo:lo + GROUP_W], cq_ref[...], saq_ref[...], sbq_ref[...]))
        k = _rope(z[:, QKV_W + lo:QKV_W + lo + GROUP_W], ck_ref[...], sak_ref[...], sbk_ref[...])
        kv = jnp.concatenate([k, z[:, 2 * QKV_W + lo:2 * QKV_W + lo + GROUP_W]], axis=-1)
        kv_ref[...] = kv
        store_tiles(kvt_ref, kv)
    store_tiles(qt_ref, jnp.concatenate(qs, axis=-1))
    ga, u, vn, gb, gates = _activations(z[:, 3 * QKV_W:], bm_ref[...], vng_ref[...], vnb_ref[...])
    ga_ref[...] = ga
    u_ref[...] = u
    vn_ref[...] = vn
    gb_ref[...] = gb
    gates_ref[...] = gates


def _sample_in_call(layer, x, g_pre, w_in, tabs_q, tabs_k, b_merge, vng, vnb):
    n, d = x.shape
    shapes = ([(n, 2 * GROUP_W)] * 3 + [(QKV_W, n)] + [(2 * GROUP_W, n)] * 3
              + [(n, w) for w in (GROUP_W, GMLP_W, GMLP_W, GMLP_W, 2 * d)])
    tabs = (*tabs_q, *tabs_k)
    return pl.pallas_call(
        _sample_in_kernel,
        grid=(1,),
        in_specs=[_const_spec(x.shape), _layer_spec(g_pre, layer), _layer_spec(w_in, layer)]
        + [_const_spec(t.shape) for t in tabs] + [_layer_spec(s, layer) for s in (b_merge, vng, vnb)],
        out_specs=[_const_spec(s) for s in shapes],
        out_shape=[jax.ShapeDtypeStruct(s, F32) for s in shapes],
        compiler_params=pltpu.CompilerParams(dimension_semantics=("arbitrary",), vmem_limit_bytes=VMEM_LIMIT),
        name="sample_in",
    )(x, g_pre, w_in, *tabs, b_merge, vng, vnb)


def _sample_out_kernel(x_ref, attn_ref, ga_ref, u_ref, vn_ref, gb_ref, gates_ref, ws0_ref, bs0_ref,
                       wpa_ref, wpb_ref, wout_ref, gpost_ref, out_ref):
    mix = vn_ref[...] * ws0_ref[...] + bs0_ref[...]
    attn = attn_ref[...].T
    out_ref[...] = _merge_out(x_ref[...], attn * ga_ref[...], (u_ref[...] * mix) * gb_ref[...],
                              gates_ref[...], wpa_ref[...], wpb_ref[...], wout_ref[...], gpost_ref[...])


def _sample_out_call(layer, x, attn, ga, u, vn, gb, gates, ws0, bs0, wpa, wpb, wout, g_post):
    acts = (x, attn, ga, u, vn, gb, gates)
    stacks = (ws0, bs0, wpa, wpb, wout, g_post)
    return pl.pallas_call(
        _sample_out_kernel,
        grid=(1,),
        in_specs=[_const_spec(a.shape) for a in acts] + [_layer_spec(s, layer) for s in stacks],
        out_specs=_const_spec(x.shape),
        out_shape=jax.ShapeDtypeStruct(x.shape, F32),
        compiler_params=pltpu.CompilerParams(dimension_semantics=("arbitrary",), vmem_limit_bytes=VMEM_LIMIT),
        name="sample_out",
    )(*acts, *stacks)


def kernel(x_prompt, x_sample, cache_kv_w128, cache_kv_w512, cache_kv_w2048, norm_pre, w_in, b_merge, v_norm_g, v_norm_b, w_spatial, b_spatial, w_proj_a, w_proj_b, w_out, norm_post):
    batch, seq, d = x_prompt.shape
    nb, dec_seq, _ = x_sample.shape
    depth = w_in.shape[0]
    assert dec_seq == 1, "the sample group decodes one position per step"
    assert seq % QKV_TILE == 0 and (batch * seq) % MLP_TILE == 0
    caches = (cache_kv_w128, cache_kv_w512, cache_kv_w2048)

    w_in_b = w_in.astype(BF16)
    wpa_b, wpb_b, wout_b = w_proj_a.astype(BF16), w_proj_b.astype(BF16), w_out.astype(BF16)
    ws_stack = w_spatial.reshape(depth, GMLP_GROUPS * BLK, BLK)
    chan_group = jnp.arange(GMLP_W) // GMLP_GROUP_DIM
    bs_full = jnp.swapaxes(b_spatial, 1, 2)[:, :, chan_group]
    ws0 = w_spatial[:, :, 0, 0][:, chan_group][:, None, :]
    bs0 = b_spatial[:, :, 0][:, chan_group][:, None, :]
    row2 = lambda a: a[:, None, :]
    g_pre, g_post, bm, vng, vnb = row2(norm_pre), row2(norm_post), row2(b_merge), row2(v_norm_g), row2(v_norm_b)

    scale = HEAD_DIM ** -0.5
    pos_p = jnp.arange(seq, dtype=jnp.int32)
    pos_s = PAST_LEN + jnp.arange(dec_seq, dtype=jnp.int32)
    tq_p, tk_p = _rope_tables(pos_p, scale), _rope_tables(pos_p, 1.0)
    tq_s, tk_s = _rope_tables(pos_s, scale), _rope_tables(pos_s, 1.0)
    band, causal = _attn_bias_tables()

    xp = x_prompt
    xs = x_sample.reshape(nb, d)
    kv_p = None
    kv_s = [[] for _ in ATTN_GROUPS]
    v_s = []
    for l in range(depth):
        q1, kv1, q2, kv2, q3, kv3, *kv_p = _qkv_call(l, xp, g_pre, w_in_b, (*tq_p, *tk_p), kv_p)
        attn = _attn_call(q1, q2, q3, kv1, kv2, kv3, band, causal, batch, seq)
        kn1, kn2, kn3, qt, kt1, kt2, kt3, ga, u, vn, gb, gates = _sample_in_call(l, xs, g_pre, w_in_b, tq_s, tk_s,
                                                                                 bm, vng, vnb)
        xp, attn_s = _mlp_call(l, xp.reshape(batch * seq, d), attn.reshape(batch * seq, GROUP_W), g_pre, w_in_b,
                               bm, vng, vnb, ws_stack, bs_full, wpa_b, wpb_b, wout_b, g_post,
                               qt, (kt1, kt2, kt3), caches)
        xp = xp.reshape(batch, seq, d)
        xs = _sample_out_call(l, xs, attn_s, ga, u, vn, gb, gates, ws0, bs0, wpa_b, wpb_b, wout_b, g_post)
        for g, kn in enumerate((kn1, kn2, kn3)):
            kv_s[g].append(kn.reshape(nb, dec_seq, 2, HEADS_PER_GROUP, HEAD_DIM))
        v_s.append(vn.reshape(nb, dec_seq, GMLP_W))

    new_kv_p = [jnp.transpose(p, (0, 1, 5, 2, 3, 4)) for p in kv_p]
    return (xp, xs.reshape(nb, dec_seq, d), new_kv_p[0], new_kv_p[1], new_kv_p[2],
            jnp.stack(kv_s[0]), jnp.stack(kv_s[1]), jnp.stack(kv_s[2]), jnp.stack(v_s))
```

```python
import functools

import jax
import jax.numpy as jnp
from jax import lax
from jax.experimental import pallas as pl
from jax.experimental.pallas import tpu as pltpu

F32 = jnp.float32
BF16 = jnp.bfloat16

PAST_LEN = 8192
HEAD_DIM = 64
HEADS_PER_GROUP = 4
GROUP_W = HEADS_PER_GROUP * HEAD_DIM
ATTN_GROUPS = ((128, 1), (512, 4), (2048, 16))
N_GROUPS = len(ATTN_GROUPS)
QKV_W = N_GROUPS * GROUP_W
ROT_DIM = HEAD_DIM // 4
ROT_HALF = ROT_DIM // 2
ROPE_THETA = 500000.0
BLK = 128
GMLP_GROUPS = 4
GMLP_GROUP_DIM = 192
GMLP_W = GMLP_GROUPS * GMLP_GROUP_DIM
EPS = 1e-6
LANES = 128
NEG = -1e30

OFF_GA = 0
OFF_U = OFF_GA + GROUP_W
OFF_VB = OFF_U + GMLP_W
OFF_GB = OFF_VB + GMLP_W
OFF_ML = OFF_GB + GMLP_W

QKV_TILE = 512
MLP_TILE = 512
VMEM_LIMIT = 56 * 1024 * 1024


def _rms(x, g):
    return (x * lax.rsqrt(jnp.mean(x * x, axis=-1, keepdims=True) + EPS)) * g


def _layer_norm(x, g, b):
    xc = x - jnp.mean(x, axis=-1, keepdims=True)
    return xc * lax.rsqrt(jnp.mean(xc * xc, axis=-1, keepdims=True) + EPS) * g + b


def _rope(z, c, sa, sb):
    outs = []
    for s in range(GROUP_W // LANES):
        zs = z[:, s * LANES:(s + 1) * LANES]
        outs.append(zs * c + pltpu.roll(zs, LANES - ROT_HALF, 1) * sa + pltpu.roll(zs, ROT_HALF, 1) * sb)
    return jnp.concatenate(outs, axis=-1)


def _rope_tables(pos, scale):
    inv_freq = jnp.power(ROPE_THETA, -jnp.arange(ROT_HALF, dtype=F32) / ROT_HALF)
    ang = pos.astype(F32)[:, None] * inv_freq[None, :]
    cos, sin = jnp.cos(ang), jnp.sin(ang)
    n = pos.shape[0]
    rest1 = jnp.ones((n, HEAD_DIM - ROT_DIM), F32)
    rest0 = jnp.zeros((n, HEAD_DIM - ROT_DIM), F32)
    z8 = jnp.zeros((n, ROT_HALF), F32)
    c = jnp.concatenate([cos, cos, rest1], axis=-1)
    sa = jnp.concatenate([-sin, z8, rest0], axis=-1)
    sb = jnp.concatenate([z8, sin, rest0], axis=-1)
    reps = LANES // HEAD_DIM
    return tuple(jnp.tile(t, (1, reps)) * scale for t in (c, sa, sb))


def _activations(z, b_merge, vng, vnb):
    ga = jax.nn.silu(z[:, OFF_GA:OFF_U])
    u = jax.nn.gelu(z[:, OFF_U:OFF_VB])
    vn = _layer_norm(jax.nn.gelu(z[:, OFF_VB:OFF_GB]), vng, vnb)
    gb = jax.nn.silu(z[:, OFF_GB:OFF_ML])
    gates = jax.nn.sigmoid(z[:, OFF_ML:] + b_merge)
    return ga, u, vn, gb, gates


def _merge_out(x, a_in, b_in, gates, wpa, wpb, wout, gpost):
    d = x.shape[-1]
    ba = jnp.dot(a_in.astype(BF16), wpa, preferred_element_type=F32)
    bb = jnp.dot(b_in.astype(BF16), wpb, preferred_element_type=F32)
    merged = gates[:, :d] * ba + gates[:, d:] * bb
    y = jnp.dot(merged.astype(BF16), wout, preferred_element_type=F32)
    return x + _rms(y, gpost)


def _const_spec(shape):
    nd = len(shape)
    return pl.BlockSpec(shape, lambda *_: (0,) * nd)


def _layer_spec(arr, layer, block_cols=None, col_block=0):
    rows, cols = arr.shape[1:]
    return pl.BlockSpec((None, rows, cols if block_cols is None else block_cols), lambda *_: (layer, 0, col_block))


def _qkv_kernel(seq, first_layer, x_ref, g_ref, w_ref, cq_ref, saq_ref, sbq_ref, ck_ref, sak_ref, sbk_ref, *refs):
    if not first_layer:
        refs = refs[N_GROUPS:]
    q1_ref, kv1_ref, q2_ref, kv2_ref, q3_ref, kv3_ref, p1_ref, p2_ref, p3_ref, za_s, zb_s, dei_s = refs
    step = pl.program_id(0)
    t = x_ref.shape[0]
    tiles_per_seq = seq // t
    last_tile = (jnp.maximum(step - 1, 0) % tiles_per_seq) == tiles_per_seq - 1
    q_refs = (q1_ref, q2_ref, q3_ref)
    kv_refs = (kv1_ref, kv2_ref, kv3_ref)
    p_refs = (p1_ref, p2_ref, p3_ref)
    nslab = GROUP_W // LANES

    @pl.when(step == 0)
    def _():
        zb_s[...] = jnp.zeros(zb_s.shape, F32)

    def store_transposed(p_ref, idx, z):
        zt = z.T.reshape(HEADS_PER_GROUP, HEAD_DIM, z.shape[0])
        if first_layer:
            p_ref[0, idx] = zt
            if p_ref.shape[0] > 1:
                p_ref[1:, idx] = jnp.zeros((p_ref.shape[0] - 1,) + zt.shape, F32)
        else:
            p_ref[idx] = zt

    def rope_k(z, g, rows=slice(None)):
        return _rope(z[rows, QKV_W + g * GROUP_W:QKV_W + (g + 1) * GROUP_W], ck_ref[rows, :], sak_ref[rows, :], sbk_ref[rows, :])

    def body(z_new, z):
        hb = _rms(x_ref[...], g_ref[...]).astype(BF16)
        z_new[...] = jnp.dot(hb, w_ref[...], preferred_element_type=F32)
        slots = iter(range(dei_s.shape[0]))

        def split_by_residue(v, dst_ref, lane0, dil):
            slot = next(slots)
            for s in range(nslab):
                dei_s[slot, s] = v[:, s * LANES:(s + 1) * LANES]
            for r in range(dil):
                rows = [dei_s[slot, s, pl.ds(r, t // dil, stride=dil), :] for s in range(nslab)]
                dst_ref[r, :, lane0:lane0 + GROUP_W] = jnp.concatenate(rows, axis=-1).astype(BF16)

        for g, (window, dil) in enumerate(ATTN_GROUPS):
            zq = _rope(z[:, g * GROUP_W:(g + 1) * GROUP_W], cq_ref[...], saq_ref[...], sbq_ref[...])
            zk = rope_k(z, g)
            zv = z[:, 2 * QKV_W + g * GROUP_W:2 * QKV_W + (g + 1) * GROUP_W]
            if dil == 1:
                q_refs[g][...] = zq.astype(BF16)
                kv_refs[g][:, 0:GROUP_W] = zk.astype(BF16)
                kv_refs[g][:, GROUP_W:2 * GROUP_W] = zv.astype(BF16)
            else:
                split_by_residue(zq, q_refs[g], 0, dil)
                split_by_residue(zk, kv_refs[g], 0, dil)
                split_by_residue(zv, kv_refs[g], GROUP_W, dil)
            if window >= seq:
                store_transposed(p_refs[g], 0, zk)
                store_transposed(p_refs[g], 1, zv)

        @pl.when(last_tile)
        def _():
            for g, (window, dil) in enumerate(ATTN_GROUPS):
                if window < seq:
                    tail = slice(t - p_refs[g].shape[-1], t)
                    store_transposed(p_refs[g], 0, rope_k(z, g, tail))
                    store_transposed(p_refs[g], 1, z[tail, 2 * QKV_W + g * GROUP_W:2 * QKV_W + (g + 1) * GROUP_W])

    @pl.when(step % 2 == 0)
    def _():
        body(za_s, zb_s)

    @pl.when(step % 2 == 1)
    def _():
        body(zb_s, za_s)


def _qkv_call(layer, x, g_pre, w_in, tabs, accs):
    batch, seq, d = x.shape
    depth = w_in.shape[0]
    first_layer = accs is None
    t = QKV_TILE
    nt = seq // t
    n_tiles = batch * nt
    proj = lambda s: jnp.minimum(s, n_tiles - 1)
    fin = lambda s: jnp.maximum(s - 1, 0)
    tab_spec = pl.BlockSpec((t, LANES), lambda s: (fin(s) % nt, 0))
    in_specs = [pl.BlockSpec((None, t, d), lambda s: (proj(s) // nt, proj(s) % nt, 0)), _layer_spec(g_pre, layer),
                _layer_spec(w_in, layer, 3 * QKV_W, 0)]
    out_specs, out_shape = [], []
    for window, dil in ATTN_GROUPS:
        assert t % (dil * 16) == 0
        if dil == 1:
            q_blk, kv_blk = (None, t, GROUP_W), (None, t, 2 * GROUP_W)
            q_shape, kv_shape = (batch, seq, GROUP_W), (batch, seq, 2 * GROUP_W)
            idx = lambda s: (fin(s) // nt, fin(s) % nt, 0)
        else:
            q_blk, kv_blk = (None, dil, t // dil, GROUP_W), (None, dil, t // dil, 2 * GROUP_W)
            q_shape, kv_shape = (batch, dil, seq // dil, GROUP_W), (batch, dil, seq // dil, 2 * GROUP_W)
            idx = lambda s: (fin(s) // nt, 0, fin(s) % nt, 0)
        out_specs += [pl.BlockSpec(q_blk, idx), pl.BlockSpec(kv_blk, idx)]
        out_shape += [jax.ShapeDtypeStruct(q_shape, BF16), jax.ShapeDtypeStruct(kv_shape, BF16)]
    for window, dil in ATTN_GROUPS:
        keep = min(window, seq)
        assert keep == seq or keep <= t, "a partial window must fit in the last tile"
        blk = min(keep, t)
        every_tile = keep == seq
        lead = 0 if first_layer else layer
        p_blk = (depth if first_layer else None, None, 2, HEADS_PER_GROUP, HEAD_DIM, blk)
        p_idx = lambda s, every_tile=every_tile: (lead, fin(s) // nt, 0, 0, 0, fin(s) % nt if every_tile else 0)
        out_specs.append(pl.BlockSpec(p_blk, p_idx))
        out_shape.append(jax.ShapeDtypeStruct((depth, batch, 2, HEADS_PER_GROUP, HEAD_DIM, keep), F32))
    n_in = len(in_specs) + len(tabs)
    acc_specs = [] if first_layer else [pl.BlockSpec(memory_space=pl.ANY)] * N_GROUPS
    aliases = {} if first_layer else {n_in + g: 2 * N_GROUPS + g for g in range(N_GROUPS)}
    n_split = 3 * sum(dil > 1 for _, dil in ATTN_GROUPS)
    return pl.pallas_call(
        functools.partial(_qkv_kernel, seq, first_layer),
        grid=(n_tiles + 1,),
        in_specs=in_specs + [tab_spec] * len(tabs) + acc_specs,
        out_specs=out_specs,
        out_shape=out_shape,
        input_output_aliases=aliases,
        scratch_shapes=[pltpu.VMEM((t, 3 * QKV_W), F32), pltpu.VMEM((t, 3 * QKV_W), F32),
                        pltpu.VMEM((n_split, GROUP_W // LANES, t, LANES), F32)],
        compiler_params=pltpu.CompilerParams(dimension_semantics=("arbitrary",), vmem_limit_bytes=VMEM_LIMIT),
        name="prompt_qkv",
    )(x, g_pre, w_in, *tabs, *(() if first_layer else accs))


def _attn_block(q, k, v, bias):
    nk = k.shape[0]
    hid = lax.shift_right_logical(lax.broadcasted_iota(jnp.int32, (BLK, GROUP_W), 1), HEAD_DIM.bit_length() - 1)
    qf = q.astype(F32)
    qs = jnp.concatenate([jnp.where(hid == h, qf, 0.0) for h in range(HEADS_PER_GROUP)], axis=0).astype(BF16)
    s = lax.dot_general(qs, k, (((1,), (1,)), ((), ())), preferred_element_type=F32)
    s = s.reshape(HEADS_PER_GROUP, BLK, nk) + bias[None]
    m = jnp.max(s, axis=-1, keepdims=True)
    p = jnp.exp(s - m)
    l = jnp.sum(p, axis=-1, keepdims=True)
    pv = jnp.dot(p.reshape(HEADS_PER_GROUP * BLK, nk).astype(BF16), v, preferred_element_type=F32)
    pv = pv.reshape(HEADS_PER_GROUP, BLK, GROUP_W)
    inv = 1.0 / l
    lse = m + jnp.log(l)
    o = jnp.zeros((BLK, GROUP_W), F32)
    ls = jnp.zeros((BLK, GROUP_W), F32)
    for h in range(HEADS_PER_GROUP):
        o = jnp.where(hid == h, pv[h] * inv[h], o)
        ls = jnp.where(hid == h, lse[h], ls)
    return o, ls


def _attn_kernel(q1_ref, q2_ref, q3_ref, kv1_ref, kv2_ref, kv3_ref, band_ref, causal_ref, out_ref,
                 o2_s, l2_s, o3_s, l3_s):
    seq = out_ref.shape[0]
    nslab = GROUP_W // LANES
    d2 = ATTN_GROUPS[1][1]
    d3 = ATTN_GROUPS[2][1]
    nblk = seq // BLK
    blocks_per_res2 = nblk // d2
    kcols, vcols = slice(0, GROUP_W), slice(GROUP_W, 2 * GROUP_W)

    def store(o_s, l_s, rows, o, ls):
        for s in range(nslab):
            o_s[s, rows, :] = o[:, s * LANES:(s + 1) * LANES]
            l_s[s, rows, :] = ls[:, s * LANES:(s + 1) * LANES]

    def band_block(q, kv, n):
        if n == 0:
            return _attn_block(q, kv[0:BLK, kcols], kv[0:BLK, vcols], causal_ref[...])
        keys = slice((n - 1) * BLK, (n + 1) * BLK)
        return _attn_block(q, kv[keys, kcols], kv[keys, vcols], band_ref[...])

    for j in range(nblk):
        r, n = divmod(j, blocks_per_res2)
        o, ls = band_block(q2_ref[r, n * BLK:(n + 1) * BLK, :], kv2_ref.at[r], n)
        store(o2_s, l2_s, pl.ds(n * (BLK * d2) + r, BLK, stride=d2), o, ls)
        o, ls = _attn_block(q3_ref[j], kv3_ref[j, :, kcols], kv3_ref[j, :, vcols], causal_ref[...])
        store(o3_s, l3_s, pl.ds(j, BLK, stride=d3), o, ls)

    for j in range(nblk):
        rows = slice(j * BLK, (j + 1) * BLK)
        o1, l1 = band_block(q1_ref[rows, :], kv1_ref, j)
        for s in range(nslab):
            lanes = slice(s * LANES, (s + 1) * LANES)
            l1s, l2, l3 = l1[:, lanes], l2_s[s, rows, :], l3_s[s, rows, :]
            m = jnp.maximum(jnp.maximum(l1s, l2), l3)
            e1, e2, e3 = jnp.exp(l1s - m), jnp.exp(l2 - m), jnp.exp(l3 - m)
            inv = 1.0 / (e1 + e2 + e3)
            out_ref[rows, lanes] = (e1 * inv) * o1[:, lanes] + (e2 * inv) * o2_s[s, rows, :] + (e3 * inv) * o3_s[s, rows, :]


def _attn_call(q1, q2, q3, kv1, kv2, kv3, tables, batch, seq):
    d2 = ATTN_GROUPS[1][1]
    d3 = ATTN_GROUPS[2][1]
    nblk = seq // BLK
    assert seq // d3 == BLK and nblk % d2 == 0 and nblk == d3
    per_seq = lambda a: pl.BlockSpec((None,) + a.shape[1:], lambda b: (b,) + (0,) * (a.ndim - 1))
    scratch = [pltpu.VMEM((GROUP_W // LANES, seq, LANES), F32)] * 4
    return pl.pallas_call(
        _attn_kernel,
        grid=(batch,),
        in_specs=[per_seq(a) for a in (q1, q2, q3, kv1, kv2, kv3)] + [_const_spec(t.shape) for t in tables],
        out_specs=pl.BlockSpec((None, seq, GROUP_W), lambda b: (b, 0, 0)),
        out_shape=jax.ShapeDtypeStruct((batch, seq, GROUP_W), F32),
        scratch_shapes=scratch,
        compiler_params=pltpu.CompilerParams(dimension_semantics=("arbitrary",), vmem_limit_bytes=VMEM_LIMIT),
        name="prompt_attn",
    )(q1, q2, q3, kv1, kv2, kv3, *tables)


def _attn_bias_tables():
    qi = jnp.arange(BLK)[:, None]
    kj = jnp.arange(2 * BLK)[None, :]
    band = jnp.where((kj >= qi) & (kj <= qi + BLK), 0.0, NEG).astype(F32)
    causal = jnp.where(jnp.arange(BLK)[None, :] <= qi, 0.0, NEG).astype(F32)
    return band, causal


def _spatial_mix(ws, v):
    gd, two = GMLP_GROUP_DIM, 2 * LANES
    starts = [g * gd // LANES * LANES for g in range(GMLP_GROUPS)]
    assert all(s + two >= (g + 1) * gd for g, s in enumerate(starts))
    r = [jnp.dot(ws[g * BLK:(g + 1) * BLK], v[:, s:s + two], preferred_element_type=F32) for g, s in enumerate(starts)]
    lane = lax.broadcasted_iota(jnp.int32, (BLK, LANES), 1)
    slabs = []
    for lo in range(0, GMLP_W, LANES):
        owners = sorted({lo // gd, (lo + LANES - 1) // gd})
        pieces = [r[g][:, lo - starts[g]:lo - starts[g] + LANES] for g in owners]
        slabs.append(pieces[0] if len(owners) == 1 else jnp.where(lane < owners[1] * gd - lo, pieces[0], pieces[1]))
    return jnp.concatenate(slabs, axis=-1)


def _sample_attend(q, kvn, caches):
    column = lambda v, lo: v[lo:lo + GROUP_W].reshape(HEADS_PER_GROUP, HEAD_DIM, 1)
    outs, lses = [], []
    for g, (window, dil) in enumerate(ATTN_GROUPS):
        qg = column(q, g * GROUP_W)
        kc, vc = caches[g]
        kn, vn = column(kvn[g], 0), column(kvn[g], GROUP_W)
        rows = kc.shape[-1]
        sc = jnp.sum(kc * qg, axis=1, keepdims=True)
        r = lax.broadcasted_iota(jnp.int32, (1, 1, rows), 2)
        sc = jnp.where((r & (dil - 1)) == 0, sc, NEG)
        ss = jnp.sum(kn * qg, axis=1, keepdims=True)
        m = jnp.maximum(jnp.max(sc, axis=-1, keepdims=True), ss)
        pc = jnp.exp(sc - m)
        ps = jnp.exp(ss - m)
        l = jnp.sum(pc, axis=-1, keepdims=True) + ps
        o = jnp.sum(vc * pc, axis=-1, keepdims=True) + ps * vn
        outs.append(o / l)
        lses.append(m + jnp.log(l))
    m = jnp.maximum(jnp.maximum(lses[0], lses[1]), lses[2])
    es = [jnp.exp(l - m) for l in lses]
    inv = 1.0 / (es[0] + es[1] + es[2])
    merged = (es[0] * inv) * outs[0] + (es[1] * inv) * outs[1] + (es[2] * inv) * outs[2]
    return merged.reshape(GROUP_W, 1)


def _mlp_kernel(x_ref, attn_ref, gpre_ref, wlo_ref, whi_ref, bm_ref, vng_ref, vnb_ref, ws_ref, bsp_ref,
                wpa_ref, wpb_ref, wout_ref, gpost_ref, qt_ref, kt1_ref, kt2_ref, kt3_ref, c1_ref, c2_ref, c3_ref,
                out_ref, sattn_ref, sacc_s):
    step = pl.program_id(0)

    @pl.when(step == 0)
    def _():
        sacc_s[...] = jnp.zeros(sacc_s.shape, F32)

    lane = lax.broadcasted_iota(jnp.int32, (1, sacc_s.shape[-1]), 1)
    for k in range(c1_ref.shape[0]):
        sel = lane == step * c1_ref.shape[0] + k
        pick = lambda ref: jnp.sum(jnp.where(sel, ref[...], 0.0), axis=-1, keepdims=True)
        res = _sample_attend(pick(qt_ref), [pick(r) for r in (kt1_ref, kt2_ref, kt3_ref)],
                             [(c[k, 0], c[k, 1]) for c in (c1_ref, c2_ref, c3_ref)])
        sacc_s[...] = jnp.where(sel, res, sacc_s[...])
    sattn_ref[...] = sacc_s[...]

    x = x_ref[...]
    rows = x.shape[0]
    hb = _rms(x, gpre_ref[...]).astype(BF16)
    z = jnp.concatenate([jnp.dot(hb, wlo_ref[...], preferred_element_type=F32),
                         jnp.dot(hb, whi_ref[...], preferred_element_type=F32)], axis=-1)
    ga, u, vn, gb, gates = _activations(z, bm_ref[...], vng_ref[...], vnb_ref[...])
    wr = lax.broadcasted_iota(jnp.int32, ws_ref.shape, 0) & (BLK - 1)
    wc = lax.broadcasted_iota(jnp.int32, ws_ref.shape, 1)
    ws = jnp.where(wc <= wr, ws_ref[...], 0.0).astype(BF16)
    vnb16 = vn.astype(BF16)
    mix = jnp.concatenate([_spatial_mix(ws, vnb16[c * BLK:(c + 1) * BLK]) + bsp_ref[...]
                           for c in range(rows // BLK)], axis=0)
    out_ref[...] = _merge_out(x, attn_ref[...] * ga, (u * mix) * gb, gates,
                              wpa_ref[...], wpb_ref[...], wout_ref[...], gpost_ref[...])


def _mlp_call(layer, x, attn, g_pre, w_in, b_merge, vng, vnb, ws_stack, bs_full, wpa, wpb, wout, g_post,
              sample_q, sample_kv, caches):
    n, d = x.shape
    t = MLP_TILE
    steps = n // t
    nb = sample_q.shape[-1]
    assert nb % steps == 0, "every grid step carries the same number of samples"
    spb = nb // steps
    row = lambda i: (i, 0)
    rest_w = w_in.shape[-1] - 3 * QKV_W
    assert rest_w == 2 * 3 * QKV_W, "rest columns are addressed as column blocks 1 and 2 of width 3 * QKV_W"
    w_specs = [_layer_spec(w_in, layer, 3 * QKV_W, 1), _layer_spec(w_in, layer, 3 * QKV_W, 2)]
    stacks = (b_merge, vng, vnb, ws_stack, bs_full, wpa, wpb, wout, g_post)
    cache_specs, cache_views = [], []
    for (window, dil), c in zip(ATTN_GROUPS, caches):
        rows = c.shape[2]
        assert rows == window and window % dil == 0 and dil & (dil - 1) == 0, "cache must hold exactly the window"
        cache_views.append(jnp.transpose(c, (0, 1, 3, 4, 5, 2)))
        cache_specs.append(pl.BlockSpec((None, spb, 2, HEADS_PER_GROUP, HEAD_DIM, rows),
                                        lambda i: (layer, i, 0, 0, 0, 0)))
    return pl.pallas_call(
        _mlp_kernel,
        grid=(steps,),
        in_specs=[pl.BlockSpec((t, d), row), pl.BlockSpec((t, GROUP_W), row), _layer_spec(g_pre, layer)] + w_specs
        + [_layer_spec(s, layer) for s in stacks]
        + [_const_spec(sample_q.shape)] + [_const_spec(kv.shape) for kv in sample_kv] + cache_specs,
        out_specs=[pl.BlockSpec((t, d), row), _const_spec((GROUP_W, nb))],
        out_shape=[jax.ShapeDtypeStruct((n, d), F32), jax.ShapeDtypeStruct((GROUP_W, nb), F32)],
        scratch_shapes=[pltpu.VMEM((GROUP_W, nb), F32)],
        compiler_params=pltpu.CompilerParams(dimension_semantics=("arbitrary",), vmem_limit_bytes=VMEM_LIMIT),
        name="prompt_mlp",
    )(x, attn, g_pre, w_in, w_in, *stacks, sample_q, *sample_kv, *cache_views)


def _sample_in_kernel(x_ref, g_ref, w_ref, cq_ref, saq_ref, sbq_ref, ck_ref, sak_ref, sbk_ref,
                      bm_ref, vng_ref, vnb_ref,
                      kv1_ref, kv2_ref, kv3_ref, qt_ref, kvt1_ref, kvt2_ref, kvt3_ref,
                      ga_ref, u_ref, vn_ref, gb_ref, gates_ref):
    hb = _rms(x_ref[...], g_ref[...]).astype(BF16)
    z = jnp.dot(hb, w_ref[...], preferred_element_type=F32)

    def store_tiles(dst_ref, rows):
        dst_ref[...] = rows.T

    qs = []
    for g, (kv_ref, kvt_ref) in enumerate(((kv1_ref, kvt1_ref), (kv2_ref, kvt2_ref), (kv3_ref, kvt3_ref))):
        lo = g * GROUP_W
        qs.append(_rope(z[:, lo:lo + GROUP_W], cq_ref[...], saq_ref[...], sbq_ref[...]))
        k = _rope(z[:, QKV_W + lo:QKV_W + lo + GROUP_W], ck_ref[...], sak_ref[...], sbk_ref[...])
        kv = jnp.concatenate([k, z[:, 2 * QKV_W + lo:2 * QKV_W + lo + GROUP_W]], axis=-1)
        kv_ref[...] = kv
        store_tiles(kvt_ref, kv)
    store_tiles(qt_ref, jnp.concatenate(qs, axis=-1))
    ga, u, vn, gb, gates = _activations(z[:, 3 * QKV_W:], bm_ref[...], vng_ref[...], vnb_ref[...])
    ga_ref[...] = ga
    u_ref[...] = u
    vn_ref[...] = vn
    gb_ref[...] = gb
    gates_ref[...] = gates


def _sample_in_call(layer, x, g_pre, w_in, tabs_q, tabs_k, b_merge, vng, vnb):
    n, d = x.shape
    shapes = ([(n, 2 * GROUP_W)] * 3 + [(QKV_W, n)] + [(2 * GROUP_W, n)] * 3
              + [(n, w) for w in (GROUP_W, GMLP_W, GMLP_W, GMLP_W, 2 * d)])
    tabs = (*tabs_q, *tabs_k)
    return pl.pallas_call(
        _sample_in_kernel,
        grid=(1,),
        in_specs=[_const_spec(x.shape), _layer_spec(g_pre, layer), _layer_spec(w_in, layer)]
        + [_const_spec(t.shape) for t in tabs] + [_layer_spec(s, layer) for s in (b_merge, vng, vnb)],
        out_specs=[_const_spec(s) for s in shapes],
        out_shape=[jax.ShapeDtypeStruct(s, F32) for s in shapes],
        compiler_params=pltpu.CompilerParams(dimension_semantics=("arbitrary",), vmem_limit_bytes=VMEM_LIMIT),
        name="sample_in",
    )(x, g_pre, w_in, *tabs, b_merge, vng, vnb)


def _sample_out_kernel(x_ref, attn_ref, ga_ref, u_ref, vn_ref, gb_ref, gates_ref, ws0_ref, bs0_ref,
                       wpa_ref, wpb_ref, wout_ref, gpost_ref, out_ref):
    mix = vn_ref[...] * ws0_ref[...] + bs0_ref[...]
    attn = attn_ref[...].T
    out_ref[...] = _merge_out(x_ref[...], attn * ga_ref[...], (u_ref[...] * mix) * gb_ref[...],
                              gates_ref[...], wpa_ref[...], wpb_ref[...], wout_ref[...], gpost_ref[...])


def _sample_out_call(layer, x, attn, ga, u, vn, gb, gates, ws0, bs0, wpa, wpb, wout, g_post):
    acts = (x, attn, ga, u, vn, gb, gates)
    stacks = (ws0, bs0, wpa, wpb, wout, g_post)
    return pl.pallas_call(
        _sample_out_kernel,
        grid=(1,),
        in_specs=[_const_spec(a.shape) for a in acts] + [_layer_spec(s, layer) for s in stacks],
        out_specs=_const_spec(x.shape),
        out_shape=jax.ShapeDtypeStruct(x.shape, F32),
        compiler_params=pltpu.CompilerParams(dimension_semantics=("arbitrary",), vmem_limit_bytes=VMEM_LIMIT),
        name="sample_out",
    )(*acts, *stacks)


def kernel(x_prompt, x_sample, cache_kv_w128, cache_kv_w512, cache_kv_w2048, norm_pre, w_in, b_merge, v_norm_g, v_norm_b, w_spatial, b_spatial, w_proj_a, w_proj_b, w_out, norm_post):
    batch, seq, d = x_prompt.shape
    nb, dec_seq, _ = x_sample.shape
    depth = w_in.shape[0]
    assert dec_seq == 1, "the sample group decodes one position per step"
    assert seq % QKV_TILE == 0 and (batch * seq) % MLP_TILE == 0
    caches = (cache_kv_w128, cache_kv_w512, cache_kv_w2048)

    w_in_b = w_in.astype(BF16)
    wpa_b, wpb_b, wout_b = w_proj_a.astype(BF16), w_proj_b.astype(BF16), w_out.astype(BF16)
    ws_stack = w_spatial.reshape(depth, GMLP_GROUPS * BLK, BLK)
    chan_group = jnp.arange(GMLP_W) // GMLP_GROUP_DIM
    bs_full = jnp.swapaxes(b_spatial, 1, 2)[:, :, chan_group]
    ws0 = w_spatial[:, :, 0, 0][:, chan_group][:, None, :]
    bs0 = b_spatial[:, :, 0][:, chan_group][:, None, :]
    row2 = lambda a: a[:, None, :]
    g_pre, g_post, bm, vng, vnb = row2(norm_pre), row2(norm_post), row2(b_merge), row2(v_norm_g), row2(v_norm_b)

    scale = HEAD_DIM ** -0.5
    pos_p = jnp.arange(seq, dtype=jnp.int32)
    pos_s = PAST_LEN + jnp.arange(dec_seq, dtype=jnp.int32)
    tq_p, tk_p = _rope_tables(pos_p, scale), _rope_tables(pos_p, 1.0)
    tq_s, tk_s = _rope_tables(pos_s, scale), _rope_tables(pos_s, 1.0)
    attn_tables = _attn_bias_tables()

    xp = x_prompt
    xs = x_sample.reshape(nb, d)
    kv_p = None
    kv_s = [[] for _ in ATTN_GROUPS]
    v_s = []
    for l in range(depth):
        q1, kv1, q2, kv2, q3, kv3, *kv_p = _qkv_call(l, xp, g_pre, w_in_b, (*tq_p, *tk_p), kv_p)
        attn = _attn_call(q1, q2, q3, kv1, kv2, kv3, attn_tables, batch, seq)
        kn1, kn2, kn3, qt, kt1, kt2, kt3, ga, u, vn, gb, gates = _sample_in_call(l, xs, g_pre, w_in_b, tq_s, tk_s,
                                                                                 bm, vng, vnb)
        xp, attn_s = _mlp_call(l, xp.reshape(batch * seq, d), attn.reshape(batch * seq, GROUP_W), g_pre, w_in_b,
                               bm, vng, vnb, ws_stack, bs_full, wpa_b, wpb_b, wout_b, g_post,
                               qt, (kt1, kt2, kt3), caches)
        xp = xp.reshape(batch, seq, d)
        xs = _sample_out_call(l, xs, attn_s, ga, u, vn, gb, gates, ws0, bs0, wpa_b, wpb_b, wout_b, g_post)
        for g, kn in enumerate((kn1, kn2, kn3)):
            kv_s[g].append(kn.reshape(nb, dec_seq, 2, HEADS_PER_GROUP, HEAD_DIM))
        v_s.append(vn.reshape(nb, dec_seq, GMLP_W))

    new_kv_p = [jnp.transpose(p, (0, 1, 5, 2, 3, 4)) for p in kv_p]
    return (xp, xs.reshape(nb, dec_seq, d), new_kv_p[0], new_kv_p[1], new_kv_p[2],
            jnp.stack(kv_s[0]), jnp.stack(kv_s[1]), jnp.stack(kv_s[2]), jnp.stack(v_s))
```

```python
import functools

import jax
import jax.numpy as jnp
from jax import lax
from jax.experimental import pallas as pl
from jax.experimental.pallas import tpu as pltpu

F32 = jnp.float32
BF16 = jnp.bfloat16

PAST_LEN = 8192
HEAD_DIM = 64
HEADS_PER_GROUP = 4
GROUP_W = HEADS_PER_GROUP * HEAD_DIM
ATTN_GROUPS = ((128, 1), (512, 4), (2048, 16))
N_GROUPS = len(ATTN_GROUPS)
QKV_W = N_GROUPS * GROUP_W
ROT_DIM = HEAD_DIM // 4
ROT_HALF = ROT_DIM // 2
ROPE_THETA = 500000.0
BLK = 128
GMLP_GROUPS = 4
GMLP_GROUP_DIM = 192
GMLP_W = GMLP_GROUPS * GMLP_GROUP_DIM
EPS = 1e-6
LANES = 128
NEG = -1e30

OFF_GA = 0
OFF_U = OFF_GA + GROUP_W
OFF_VB = OFF_U + GMLP_W
OFF_GB = OFF_VB + GMLP_W
OFF_ML = OFF_GB + GMLP_W

QKV_TILE = 512
MLP_TILE = 512
VMEM_LIMIT = 56 * 1024 * 1024


def _rms(x, g):
    return (x * lax.rsqrt(jnp.mean(x * x, axis=-1, keepdims=True) + EPS)) * g


def _layer_norm(x, g, b):
    xc = x - jnp.mean(x, axis=-1, keepdims=True)
    return xc * lax.rsqrt(jnp.mean(xc * xc, axis=-1, keepdims=True) + EPS) * g + b


def _rope(z, c, sa, sb):
    outs = []
    for s in range(GROUP_W // LANES):
        zs = z[:, s * LANES:(s + 1) * LANES]
        outs.append(zs * c + pltpu.roll(zs, LANES - ROT_HALF, 1) * sa + pltpu.roll(zs, ROT_HALF, 1) * sb)
    return jnp.concatenate(outs, axis=-1)


def _rope_tables(pos, scale):
    inv_freq = jnp.power(ROPE_THETA, -jnp.arange(ROT_HALF, dtype=F32) / ROT_HALF)
    ang = pos.astype(F32)[:, None] * inv_freq[None, :]
    cos, sin = jnp.cos(ang), jnp.sin(ang)
    n = pos.shape[0]
    rest1 = jnp.ones((n, HEAD_DIM - ROT_DIM), F32)
    rest0 = jnp.zeros((n, HEAD_DIM - ROT_DIM), F32)
    z8 = jnp.zeros((n, ROT_HALF), F32)
    c = jnp.concatenate([cos, cos, rest1], axis=-1)
    sa = jnp.concatenate([-sin, z8, rest0], axis=-1)
    sb = jnp.concatenate([z8, sin, rest0], axis=-1)
    reps = LANES // HEAD_DIM
    return tuple(jnp.tile(t, (1, reps)) * scale for t in (c, sa, sb))


def _activations(z, b_merge, vng, vnb):
    ga = jax.nn.silu(z[:, OFF_GA:OFF_U])
    u = jax.nn.gelu(z[:, OFF_U:OFF_VB])
    vn = _layer_norm(jax.nn.gelu(z[:, OFF_VB:OFF_GB]), vng, vnb)
    gb = jax.nn.silu(z[:, OFF_GB:OFF_ML])
    gates = jax.nn.sigmoid(z[:, OFF_ML:] + b_merge)
    return ga, u, vn, gb, gates


def _merge_out(x, a_in, b_in, gates, wpa, wpb, wout, gpost):
    d = x.shape[-1]
    ba = jnp.dot(a_in.astype(BF16), wpa, preferred_element_type=F32)
    bb = jnp.dot(b_in.astype(BF16), wpb, preferred_element_type=F32)
    merged = gates[:, :d] * ba + gates[:, d:] * bb
    y = jnp.dot(merged.astype(BF16), wout, preferred_element_type=F32)
    return x + _rms(y, gpost)


def _const_spec(shape):
    nd = len(shape)
    return pl.BlockSpec(shape, lambda *_: (0,) * nd)


def _layer_spec(arr, layer, block_cols=None, col_block=0):
    rows, cols = arr.shape[-2:]
    block_cols = cols if block_cols is None else block_cols
    if arr.ndim == 2:
        return pl.BlockSpec((rows, block_cols), lambda *_: (0, col_block))
    return pl.BlockSpec((None, rows, block_cols), lambda *_: (layer, 0, col_block))


def _qkv_kernel(seq, first_layer, x_ref, g_ref, w_ref, cq_ref, saq_ref, sbq_ref, ck_ref, sak_ref, sbk_ref, *refs):
    if not first_layer:
        refs = refs[N_GROUPS:]
    q1_ref, kv1_ref, q2_ref, kv2_ref, q3_ref, kv3_ref, p1_ref, p2_ref, p3_ref, za_s, zb_s, dei_s = refs
    step = pl.program_id(0)
    t = x_ref.shape[0]
    tiles_per_seq = seq // t
    last_tile = (jnp.maximum(step - 1, 0) % tiles_per_seq) == tiles_per_seq - 1
    q_refs = (q1_ref, q2_ref, q3_ref)
    kv_refs = (kv1_ref, kv2_ref, kv3_ref)
    p_refs = (p1_ref, p2_ref, p3_ref)
    nslab = GROUP_W // LANES

    @pl.when(step == 0)
    def _():
        zb_s[...] = jnp.zeros(zb_s.shape, F32)

    def store_transposed(p_ref, idx, z):
        zt = z.T.reshape(HEADS_PER_GROUP, HEAD_DIM, z.shape[0])
        if first_layer:
            p_ref[0, idx] = zt
            if p_ref.shape[0] > 1:
                p_ref[1:, idx] = jnp.zeros((p_ref.shape[0] - 1,) + zt.shape, F32)
        else:
            p_ref[idx] = zt

    def rope_k(z, g, rows=slice(None)):
        return _rope(z[rows, QKV_W + g * GROUP_W:QKV_W + (g + 1) * GROUP_W], ck_ref[rows, :], sak_ref[rows, :], sbk_ref[rows, :])

    def body(z_new, z):
        hb = _rms(x_ref[...], g_ref[...]).astype(BF16)
        z_new[...] = jnp.dot(hb, w_ref[...], preferred_element_type=F32)
        slots = iter(range(dei_s.shape[0]))

        def split_by_residue(v, dst_ref, lane0, dil):
            slot = next(slots)
            for s in range(nslab):
                dei_s[slot, s] = v[:, s * LANES:(s + 1) * LANES]
            for r in range(dil):
                rows = [dei_s[slot, s, pl.ds(r, t // dil, stride=dil), :] for s in range(nslab)]
                dst_ref[r, :, lane0:lane0 + GROUP_W] = jnp.concatenate(rows, axis=-1).astype(BF16)

        for g, (window, dil) in enumerate(ATTN_GROUPS):
            zq = _rope(z[:, g * GROUP_W:(g + 1) * GROUP_W], cq_ref[...], saq_ref[...], sbq_ref[...])
            zk = rope_k(z, g)
            zv = z[:, 2 * QKV_W + g * GROUP_W:2 * QKV_W + (g + 1) * GROUP_W]
            if dil == 1:
                q_refs[g][...] = zq.astype(BF16)
                kv_refs[g][:, 0:GROUP_W] = zk.astype(BF16)
                kv_refs[g][:, GROUP_W:2 * GROUP_W] = zv.astype(BF16)
            else:
                split_by_residue(zq, q_refs[g], 0, dil)
                split_by_residue(zk, kv_refs[g], 0, dil)
                split_by_residue(zv, kv_refs[g], GROUP_W, dil)
            if window >= seq:
                store_transposed(p_refs[g], 0, zk)
                store_transposed(p_refs[g], 1, zv)

        @pl.when(last_tile)
        def _():
            for g, (window, dil) in enumerate(ATTN_GROUPS):
                if window < seq:
                    tail = slice(t - p_refs[g].shape[-1], t)
                    store_transposed(p_refs[g], 0, rope_k(z, g, tail))
                    store_transposed(p_refs[g], 1, z[tail, 2 * QKV_W + g * GROUP_W:2 * QKV_W + (g + 1) * GROUP_W])

    @pl.when(step % 2 == 0)
    def _():
        body(za_s, zb_s)

    @pl.when(step % 2 == 1)
    def _():
        body(zb_s, za_s)


def _qkv_call(layer, x, g_pre, w_in, tabs, accs):
    batch, seq, d = x.shape
    depth = g_pre.shape[0]
    first_layer = accs is None
    t = QKV_TILE
    nt = seq // t
    n_tiles = batch * nt
    proj = lambda s: jnp.minimum(s, n_tiles - 1)
    fin = lambda s: jnp.maximum(s - 1, 0)
    tab_spec = pl.BlockSpec((t, LANES), lambda s: (fin(s) % nt, 0))
    in_specs = [pl.BlockSpec((None, t, d), lambda s: (proj(s) // nt, proj(s) % nt, 0)), _layer_spec(g_pre, layer),
                _layer_spec(w_in, layer, 3 * QKV_W, 0)]
    out_specs, out_shape = [], []
    for window, dil in ATTN_GROUPS:
        assert t % (dil * 16) == 0
        if dil == 1:
            q_blk, kv_blk = (None, t, GROUP_W), (None, t, 2 * GROUP_W)
            q_shape, kv_shape = (batch, seq, GROUP_W), (batch, seq, 2 * GROUP_W)
            idx = lambda s: (fin(s) // nt, fin(s) % nt, 0)
        else:
            q_blk, kv_blk = (None, dil, t // dil, GROUP_W), (None, dil, t // dil, 2 * GROUP_W)
            q_shape, kv_shape = (batch, dil, seq // dil, GROUP_W), (batch, dil, seq // dil, 2 * GROUP_W)
            idx = lambda s: (fin(s) // nt, 0, fin(s) % nt, 0)
        out_specs += [pl.BlockSpec(q_blk, idx), pl.BlockSpec(kv_blk, idx)]
        out_shape += [jax.ShapeDtypeStruct(q_shape, BF16), jax.ShapeDtypeStruct(kv_shape, BF16)]
    for window, dil in ATTN_GROUPS:
        keep = min(window, seq)
        assert keep == seq or keep <= t, "a partial window must fit in the last tile"
        blk = min(keep, t)
        every_tile = keep == seq
        lead = 0 if first_layer else layer
        p_blk = (depth if first_layer else None, None, 2, HEADS_PER_GROUP, HEAD_DIM, blk)
        p_idx = lambda s, every_tile=every_tile: (lead, fin(s) // nt, 0, 0, 0, fin(s) % nt if every_tile else 0)
        out_specs.append(pl.BlockSpec(p_blk, p_idx))
        out_shape.append(jax.ShapeDtypeStruct((depth, batch, 2, HEADS_PER_GROUP, HEAD_DIM, keep), F32))
    n_in = len(in_specs) + len(tabs)
    acc_specs = [] if first_layer else [pl.BlockSpec(memory_space=pl.ANY)] * N_GROUPS
    aliases = {} if first_layer else {n_in + g: 2 * N_GROUPS + g for g in range(N_GROUPS)}
    n_split = 3 * sum(dil > 1 for _, dil in ATTN_GROUPS)
    return pl.pallas_call(
        functools.partial(_qkv_kernel, seq, first_layer),
        grid=(n_tiles + 1,),
        in_specs=in_specs + [tab_spec] * len(tabs) + acc_specs,
        out_specs=out_specs,
        out_shape=out_shape,
        input_output_aliases=aliases,
        scratch_shapes=[pltpu.VMEM((t, 3 * QKV_W), F32), pltpu.VMEM((t, 3 * QKV_W), F32),
                        pltpu.VMEM((n_split, GROUP_W // LANES, t, LANES), F32)],
        compiler_params=pltpu.CompilerParams(dimension_semantics=("arbitrary",), vmem_limit_bytes=VMEM_LIMIT),
        name="prompt_qkv",
    )(x, g_pre, w_in, *tabs, *(() if first_layer else accs))


def _attn_block(q, k, v, bias):
    nk = k.shape[0]
    hid = lax.shift_right_logical(lax.broadcasted_iota(jnp.int32, (BLK, GROUP_W), 1), HEAD_DIM.bit_length() - 1)
    qf = q.astype(F32)
    qs = jnp.concatenate([jnp.where(hid == h, qf, 0.0) for h in range(HEADS_PER_GROUP)], axis=0).astype(BF16)
    s = lax.dot_general(qs, k, (((1,), (1,)), ((), ())), preferred_element_type=F32)
    s = s.reshape(HEADS_PER_GROUP, BLK, nk) + bias[None]
    m = jnp.max(s, axis=-1, keepdims=True)
    p = jnp.exp(s - m)
    l = jnp.sum(p, axis=-1, keepdims=True)
    pv = jnp.dot(p.reshape(HEADS_PER_GROUP * BLK, nk).astype(BF16), v, preferred_element_type=F32)
    pv = pv.reshape(HEADS_PER_GROUP, BLK, GROUP_W)
    inv = 1.0 / l
    lse = m + jnp.log(l)
    o = jnp.zeros((BLK, GROUP_W), F32)
    ls = jnp.zeros((BLK, GROUP_W), F32)
    for h in range(HEADS_PER_GROUP):
        o = jnp.where(hid == h, pv[h] * inv[h], o)
        ls = jnp.where(hid == h, lse[h], ls)
    return o, ls


def _attn_kernel(q1_ref, q2_ref, q3_ref, kv1_ref, kv2_ref, kv3_ref, band_ref, causal_ref, out_ref,
                 o2_s, l2_s, o3_s, l3_s):
    seq = out_ref.shape[0]
    nslab = GROUP_W // LANES
    d2 = ATTN_GROUPS[1][1]
    d3 = ATTN_GROUPS[2][1]
    nblk = seq // BLK
    blocks_per_res2 = nblk // d2
    kcols, vcols = slice(0, GROUP_W), slice(GROUP_W, 2 * GROUP_W)

    def store(o_s, l_s, rows, o, ls):
        for s in range(nslab):
            o_s[s, rows, :] = o[:, s * LANES:(s + 1) * LANES]
            l_s[s, rows, :] = ls[:, s * LANES:(s + 1) * LANES]

    def band_block(q, kv, n):
        if n == 0:
            return _attn_block(q, kv[0:BLK, kcols], kv[0:BLK, vcols], causal_ref[...])
        keys = slice((n - 1) * BLK, (n + 1) * BLK)
        return _attn_block(q, kv[keys, kcols], kv[keys, vcols], band_ref[...])

    for j in range(nblk):
        r, n = divmod(j, blocks_per_res2)
        o, ls = band_block(q2_ref[r, n * BLK:(n + 1) * BLK, :], kv2_ref.at[r], n)
        store(o2_s, l2_s, pl.ds(n * (BLK * d2) + r, BLK, stride=d2), o, ls)
        o, ls = _attn_block(q3_ref[j], kv3_ref[j, :, kcols], kv3_ref[j, :, vcols], causal_ref[...])
        store(o3_s, l3_s, pl.ds(j, BLK, stride=d3), o, ls)

    for j in range(nblk):
        rows = slice(j * BLK, (j + 1) * BLK)
        o1, l1 = band_block(q1_ref[rows, :], kv1_ref, j)
        for s in range(nslab):
            lanes = slice(s * LANES, (s + 1) * LANES)
            l1s, l2, l3 = l1[:, lanes], l2_s[s, rows, :], l3_s[s, rows, :]
            m = jnp.maximum(jnp.maximum(l1s, l2), l3)
            e1, e2, e3 = jnp.exp(l1s - m), jnp.exp(l2 - m), jnp.exp(l3 - m)
            inv = 1.0 / (e1 + e2 + e3)
            out_ref[rows, lanes] = (e1 * inv) * o1[:, lanes] + (e2 * inv) * o2_s[s, rows, :] + (e3 * inv) * o3_s[s, rows, :]


def _attn_call(q1, q2, q3, kv1, kv2, kv3, tables, batch, seq):
    d2 = ATTN_GROUPS[1][1]
    d3 = ATTN_GROUPS[2][1]
    nblk = seq // BLK
    assert seq // d3 == BLK and nblk % d2 == 0 and nblk == d3
    per_seq = lambda a: pl.BlockSpec((None,) + a.shape[1:], lambda b: (b,) + (0,) * (a.ndim - 1))
    scratch = [pltpu.VMEM((GROUP_W // LANES, seq, LANES), F32)] * 4
    return pl.pallas_call(
        _attn_kernel,
        grid=(batch,),
        in_specs=[per_seq(a) for a in (q1, q2, q3, kv1, kv2, kv3)] + [_const_spec(t.shape) for t in tables],
        out_specs=pl.BlockSpec((None, seq, GROUP_W), lambda b: (b, 0, 0)),
        out_shape=jax.ShapeDtypeStruct((batch, seq, GROUP_W), F32),
        scratch_shapes=scratch,
        compiler_params=pltpu.CompilerParams(dimension_semantics=("arbitrary",), vmem_limit_bytes=VMEM_LIMIT),
        name="prompt_attn",
    )(q1, q2, q3, kv1, kv2, kv3, *tables)


def _attn_bias_tables():
    qi = jnp.arange(BLK)[:, None]
    kj = jnp.arange(2 * BLK)[None, :]
    band = jnp.where((kj >= qi) & (kj <= qi + BLK), 0.0, NEG).astype(F32)
    causal = jnp.where(jnp.arange(BLK)[None, :] <= qi, 0.0, NEG).astype(F32)
    return band, causal


def _spatial_mix(ws, v):
    gd, two = GMLP_GROUP_DIM, 2 * LANES
    starts = [g * gd // LANES * LANES for g in range(GMLP_GROUPS)]
    assert all(s + two >= (g + 1) * gd for g, s in enumerate(starts))
    r = [jnp.dot(ws[g * BLK:(g + 1) * BLK], v[:, s:s + two], preferred_element_type=F32) for g, s in enumerate(starts)]
    lane = lax.broadcasted_iota(jnp.int32, (BLK, LANES), 1)
    slabs = []
    for lo in range(0, GMLP_W, LANES):
        owners = sorted({lo // gd, (lo + LANES - 1) // gd})
        pieces = [r[g][:, lo - starts[g]:lo - starts[g] + LANES] for g in owners]
        slabs.append(pieces[0] if len(owners) == 1 else jnp.where(lane < owners[1] * gd - lo, pieces[0], pieces[1]))
    return jnp.concatenate(slabs, axis=-1)


def _sample_attend(q, kvn, caches):
    column = lambda v, lo: v[lo:lo + GROUP_W].reshape(HEADS_PER_GROUP, HEAD_DIM, 1)
    outs, lses = [], []
    for g, (window, dil) in enumerate(ATTN_GROUPS):
        qg = column(q, g * GROUP_W)
        kc, vc = caches[g]
        kn, vn = column(kvn[g], 0), column(kvn[g], GROUP_W)
        rows = kc.shape[-1]
        sc = jnp.sum(kc * qg, axis=1, keepdims=True)
        r = lax.broadcasted_iota(jnp.int32, (1, 1, rows), 2)
        sc = jnp.where((r & (dil - 1)) == 0, sc, NEG)
        ss = jnp.sum(kn * qg, axis=1, keepdims=True)
        m = jnp.maximum(jnp.max(sc, axis=-1, keepdims=True), ss)
        pc = jnp.exp(sc - m)
        ps = jnp.exp(ss - m)
        l = jnp.sum(pc, axis=-1, keepdims=True) + ps
        o = jnp.sum(vc * pc, axis=-1, keepdims=True) + ps * vn
        outs.append(o / l)
        lses.append(m + jnp.log(l))
    m = jnp.maximum(jnp.maximum(lses[0], lses[1]), lses[2])
    es = [jnp.exp(l - m) for l in lses]
    inv = 1.0 / (es[0] + es[1] + es[2])
    merged = (es[0] * inv) * outs[0] + (es[1] * inv) * outs[1] + (es[2] * inv) * outs[2]
    return merged.reshape(GROUP_W, 1)


def _mlp_kernel(x_ref, attn_ref, gpre_ref, wlo_ref, whi_ref, bm_ref, vng_ref, vnb_ref, ws_ref, bsp_ref,
                wpa_ref, wpb_ref, wout_ref, gpost_ref, qt_ref, kt1_ref, kt2_ref, kt3_ref, c1_ref, c2_ref, c3_ref,
                wsrc_ref, out_ref, sattn_ref, wnext_ref, sacc_s):
    step = pl.program_id(0)
    wnext_ref[...] = wsrc_ref[...].astype(BF16)

    @pl.when(step == 0)
    def _():
        sacc_s[...] = jnp.zeros(sacc_s.shape, F32)

    lane = lax.broadcasted_iota(jnp.int32, (1, sacc_s.shape[-1]), 1)
    for k in range(c1_ref.shape[0]):
        sel = lane == step * c1_ref.shape[0] + k
        pick = lambda ref: jnp.sum(jnp.where(sel, ref[...], 0.0), axis=-1, keepdims=True)
        res = _sample_attend(pick(qt_ref), [pick(r) for r in (kt1_ref, kt2_ref, kt3_ref)],
                             [(c[k, 0], c[k, 1]) for c in (c1_ref, c2_ref, c3_ref)])
        sacc_s[...] = jnp.where(sel, res, sacc_s[...])
    sattn_ref[...] = sacc_s[...]

    x = x_ref[...]
    rows = x.shape[0]
    hb = _rms(x, gpre_ref[...]).astype(BF16)
    z = jnp.concatenate([jnp.dot(hb, wlo_ref[...], preferred_element_type=F32),
                         jnp.dot(hb, whi_ref[...], preferred_element_type=F32)], axis=-1)
    ga, u, vn, gb, gates = _activations(z, bm_ref[...], vng_ref[...], vnb_ref[...])
    wr = lax.broadcasted_iota(jnp.int32, ws_ref.shape, 0) & (BLK - 1)
    wc = lax.broadcasted_iota(jnp.int32, ws_ref.shape, 1)
    ws = jnp.where(wc <= wr, ws_ref[...], 0.0).astype(BF16)
    vnb16 = vn.astype(BF16)
    mix = jnp.concatenate([_spatial_mix(ws, vnb16[c * BLK:(c + 1) * BLK]) + bsp_ref[...]
                           for c in range(rows // BLK)], axis=0)
    out_ref[...] = _merge_out(x, attn_ref[...] * ga, (u * mix) * gb, gates,
                              wpa_ref[...], wpb_ref[...], wout_ref[...], gpost_ref[...])


def _mlp_call(layer, x, attn, g_pre, w_in, b_merge, vng, vnb, ws_stack, bs_full, wpa, wpb, wout, g_post,
              sample_q, sample_kv, caches, w_in_f32):
    n, d = x.shape
    t = MLP_TILE
    steps = n // t
    depth, w_rows, w_cols = w_in_f32.shape
    assert w_rows % (steps * 16) == 0, "each step converts one bf16-tileable row slab of the next weight"
    slab = w_rows // steps
    next_layer = min(layer + 1, depth - 1)
    nb = sample_q.shape[-1]
    assert nb % steps == 0, "every grid step carries the same number of samples"
    spb = nb // steps
    row = lambda i: (i, 0)
    rest_w = w_in.shape[-1] - 3 * QKV_W
    assert rest_w == 2 * 3 * QKV_W, "rest columns are addressed as column blocks 1 and 2 of width 3 * QKV_W"
    w_specs = [_layer_spec(w_in, layer, 3 * QKV_W, 1), _layer_spec(w_in, layer, 3 * QKV_W, 2)]
    stacks = (b_merge, vng, vnb, ws_stack, bs_full, wpa, wpb, wout, g_post)
    cache_specs, cache_views = [], []
    for (window, dil), c in zip(ATTN_GROUPS, caches):
        rows = c.shape[2]
        assert rows == window and window % dil == 0 and dil & (dil - 1) == 0, "cache must hold exactly the window"
        cache_views.append(jnp.transpose(c, (0, 1, 3, 4, 5, 2)))
        cache_specs.append(pl.BlockSpec((None, spb, 2, HEADS_PER_GROUP, HEAD_DIM, rows),
                                        lambda i: (layer, i, 0, 0, 0, 0)))
    return pl.pallas_call(
        _mlp_kernel,
        grid=(steps,),
        in_specs=[pl.BlockSpec((t, d), row), pl.BlockSpec((t, GROUP_W), row), _layer_spec(g_pre, layer)] + w_specs
        + [_layer_spec(s, layer) for s in stacks]
        + [_const_spec(sample_q.shape)] + [_const_spec(kv.shape) for kv in sample_kv] + cache_specs
        + [pl.BlockSpec((None, slab, w_cols), lambda i: (next_layer, i, 0))],
        out_specs=[pl.BlockSpec((t, d), row), _const_spec((GROUP_W, nb)), pl.BlockSpec((slab, w_cols), row)],
        out_shape=[jax.ShapeDtypeStruct((n, d), F32), jax.ShapeDtypeStruct((GROUP_W, nb), F32),
                   jax.ShapeDtypeStruct((w_rows, w_cols), BF16)],
        scratch_shapes=[pltpu.VMEM((GROUP_W, nb), F32)],
        compiler_params=pltpu.CompilerParams(dimension_semantics=("arbitrary",), vmem_limit_bytes=VMEM_LIMIT),
        name="prompt_mlp",
    )(x, attn, g_pre, w_in, w_in, *stacks, sample_q, *sample_kv, *cache_views, w_in_f32)


def _sample_in_kernel(x_ref, g_ref, w_ref, cq_ref, saq_ref, sbq_ref, ck_ref, sak_ref, sbk_ref,
                      bm_ref, vng_ref, vnb_ref,
                      kv1_ref, kv2_ref, kv3_ref, qt_ref, kvt1_ref, kvt2_ref, kvt3_ref,
                      ga_ref, u_ref, vn_ref, gb_ref, gates_ref):
    hb = _rms(x_ref[...], g_ref[...]).astype(BF16)
    z = jnp.dot(hb, w_ref[...], preferred_element_type=F32)

    def store_tiles(dst_ref, rows):
        dst_ref[...] = rows.T

    qs = []
    for g, (kv_ref, kvt_ref) in enumerate(((kv1_ref, kvt1_ref), (kv2_ref, kvt2_ref), (kv3_ref, kvt3_ref))):
        lo = g * GROUP_W
        qs.append(_rope(z[:, lo:lo + GROUP_W], cq_ref[...], saq_ref[...], sbq_ref[...]))
        k = _rope(z[:, QKV_W + lo:QKV_W + lo + GROUP_W], ck_ref[...], sak_ref[...], sbk_ref[...])
        kv = jnp.concatenate([k, z[:, 2 * QKV_W + lo:2 * QKV_W + lo + GROUP_W]], axis=-1)
        kv_ref[...] = kv
        store_tiles(kvt_ref, kv)
    store_tiles(qt_ref, jnp.concatenate(qs, axis=-1))
    ga, u, vn, gb, gates = _activations(z[:, 3 * QKV_W:], bm_ref[...], vng_ref[...], vnb_ref[...])
    ga_ref[...] = ga
    u_ref[...] = u
    vn_ref[...] = vn
    gb_ref[...] = gb
    gates_ref[...] = gates


def _sample_in_call(layer, x, g_pre, w_in, tabs_q, tabs_k, b_merge, vng, vnb):
    n, d = x.shape
    shapes = ([(n, 2 * GROUP_W)] * 3 + [(QKV_W, n)] + [(2 * GROUP_W, n)] * 3
              + [(n, w) for w in (GROUP_W, GMLP_W, GMLP_W, GMLP_W, 2 * d)])
    tabs = (*tabs_q, *tabs_k)
    return pl.pallas_call(
        _sample_in_kernel,
        grid=(1,),
        in_specs=[_const_spec(x.shape), _layer_spec(g_pre, layer), _layer_spec(w_in, layer)]
        + [_const_spec(t.shape) for t in tabs] + [_layer_spec(s, layer) for s in (b_merge, vng, vnb)],
        out_specs=[_const_spec(s) for s in shapes],
        out_shape=[jax.ShapeDtypeStruct(s, F32) for s in shapes],
        compiler_params=pltpu.CompilerParams(dimension_semantics=("arbitrary",), vmem_limit_bytes=VMEM_LIMIT),
        name="sample_in",
    )(x, g_pre, w_in, *tabs, b_merge, vng, vnb)


def _sample_out_kernel(x_ref, attn_ref, ga_ref, u_ref, vn_ref, gb_ref, gates_ref, ws0_ref, bs0_ref,
                       wpa_ref, wpb_ref, wout_ref, gpost_ref, out_ref):
    mix = vn_ref[...] * ws0_ref[...] + bs0_ref[...]
    attn = attn_ref[...].T
    out_ref[...] = _merge_out(x_ref[...], attn * ga_ref[...], (u_ref[...] * mix) * gb_ref[...],
                              gates_ref[...], wpa_ref[...], wpb_ref[...], wout_ref[...], gpost_ref[...])


def _sample_out_call(layer, x, attn, ga, u, vn, gb, gates, ws0, bs0, wpa, wpb, wout, g_post):
    acts = (x, attn, ga, u, vn, gb, gates)
    stacks = (ws0, bs0, wpa, wpb, wout, g_post)
    return pl.pallas_call(
        _sample_out_kernel,
        grid=(1,),
        in_specs=[_const_spec(a.shape) for a in acts] + [_layer_spec(s, layer) for s in stacks],
        out_specs=_const_spec(x.shape),
        out_shape=jax.ShapeDtypeStruct(x.shape, F32),
        compiler_params=pltpu.CompilerParams(dimension_semantics=("arbitrary",), vmem_limit_bytes=VMEM_LIMIT),
        name="sample_out",
    )(*acts, *stacks)


def kernel(x_prompt, x_sample, cache_kv_w128, cache_kv_w512, cache_kv_w2048, norm_pre, w_in, b_merge, v_norm_g, v_norm_b, w_spatial, b_spatial, w_proj_a, w_proj_b, w_out, norm_post):
    batch, seq, d = x_prompt.shape
    nb, dec_seq, _ = x_sample.shape
    depth = w_in.shape[0]
    assert dec_seq == 1, "the sample group decodes one position per step"
    assert seq % QKV_TILE == 0 and (batch * seq) % MLP_TILE == 0
    caches = (cache_kv_w128, cache_kv_w512, cache_kv_w2048)

    w_in_l = w_in[0].astype(BF16)
    wpa_b, wpb_b, wout_b = w_proj_a.astype(BF16), w_proj_b.astype(BF16), w_out.astype(BF16)
    ws_stack = w_spatial.reshape(depth, GMLP_GROUPS * BLK, BLK)
    chan_group = jnp.arange(GMLP_W) // GMLP_GROUP_DIM
    bs_full = jnp.swapaxes(b_spatial, 1, 2)[:, :, chan_group]
    ws0 = w_spatial[:, :, 0, 0][:, chan_group][:, None, :]
    bs0 = b_spatial[:, :, 0][:, chan_group][:, None, :]
    row2 = lambda a: a[:, None, :]
    g_pre, g_post, bm, vng, vnb = row2(norm_pre), row2(norm_post), row2(b_merge), row2(v_norm_g), row2(v_norm_b)

    scale = HEAD_DIM ** -0.5
    pos_p = jnp.arange(seq, dtype=jnp.int32)
    pos_s = PAST_LEN + jnp.arange(dec_seq, dtype=jnp.int32)
    tq_p, tk_p = _rope_tables(pos_p, scale), _rope_tables(pos_p, 1.0)
    tq_s, tk_s = _rope_tables(pos_s, scale), _rope_tables(pos_s, 1.0)
    attn_tables = _attn_bias_tables()

    xp = x_prompt
    xs = x_sample.reshape(nb, d)
    kv_p = None
    kv_s = [[] for _ in ATTN_GROUPS]
    v_s = []
    for l in range(depth):
        q1, kv1, q2, kv2, q3, kv3, *kv_p = _qkv_call(l, xp, g_pre, w_in_l, (*tq_p, *tk_p), kv_p)
        attn = _attn_call(q1, q2, q3, kv1, kv2, kv3, attn_tables, batch, seq)
        kn1, kn2, kn3, qt, kt1, kt2, kt3, ga, u, vn, gb, gates = _sample_in_call(l, xs, g_pre, w_in_l, tq_s, tk_s,
                                                                                 bm, vng, vnb)
        xp, attn_s, w_in_l = _mlp_call(l, xp.reshape(batch * seq, d), attn.reshape(batch * seq, GROUP_W), g_pre,
                                       w_in_l, bm, vng, vnb, ws_stack, bs_full, wpa_b, wpb_b, wout_b, g_post,
                                       qt, (kt1, kt2, kt3), caches, w_in)
        xp = xp.reshape(batch, seq, d)
        xs = _sample_out_call(l, xs, attn_s, ga, u, vn, gb, gates, ws0, bs0, wpa_b, wpb_b, wout_b, g_post)
        for g, kn in enumerate((kn1, kn2, kn3)):
            kv_s[g].append(kn.reshape(nb, dec_seq, 2, HEADS_PER_GROUP, HEAD_DIM))
        v_s.append(vn.reshape(nb, dec_seq, GMLP_W))

    new_kv_p = [jnp.transpose(p, (0, 1, 5, 2, 3, 4)) for p in kv_p]
    return (xp, xs.reshape(nb, dec_seq, d), new_kv_p[0], new_kv_p[1], new_kv_p[2],
            jnp.stack(kv_s[0]), jnp.stack(kv_s[1]), jnp.stack(kv_s[2]), jnp.stack(v_s))
```

```python
import functools

import jax
import jax.numpy as jnp
from jax import lax
from jax.experimental import pallas as pl
from jax.experimental.pallas import tpu as pltpu

F32 = jnp.float32
BF16 = jnp.bfloat16

PAST_LEN = 8192
HEAD_DIM = 64
HEADS_PER_GROUP = 4
GROUP_W = HEADS_PER_GROUP * HEAD_DIM
ATTN_GROUPS = ((128, 1), (512, 4), (2048, 16))
N_GROUPS = len(ATTN_GROUPS)
QKV_W = N_GROUPS * GROUP_W
ROT_DIM = HEAD_DIM // 4
ROT_HALF = ROT_DIM // 2
ROPE_THETA = 500000.0
BLK = 128
GMLP_GROUPS = 4
GMLP_GROUP_DIM = 192
GMLP_W = GMLP_GROUPS * GMLP_GROUP_DIM
EPS = 1e-6
LANES = 128
NEG = -1e30

OFF_GA = 0
OFF_U = OFF_GA + GROUP_W
OFF_VB = OFF_U + GMLP_W
OFF_GB = OFF_VB + GMLP_W
OFF_ML = OFF_GB + GMLP_W

QKV_TILE = 512
MLP_TILE = 512
VMEM_LIMIT = 56 * 1024 * 1024


def _rms(x, g):
    return (x * lax.rsqrt(jnp.mean(x * x, axis=-1, keepdims=True) + EPS)) * g


def _layer_norm(x, g, b):
    xc = x - jnp.mean(x, axis=-1, keepdims=True)
    return xc * lax.rsqrt(jnp.mean(xc * xc, axis=-1, keepdims=True) + EPS) * g + b


def _rope(z, c, sa, sb):
    outs = []
    for s in range(GROUP_W // LANES):
        zs = z[:, s * LANES:(s + 1) * LANES]
        outs.append(zs * c + pltpu.roll(zs, LANES - ROT_HALF, 1) * sa + pltpu.roll(zs, ROT_HALF, 1) * sb)
    return jnp.concatenate(outs, axis=-1)


def _rope_tables(pos, scale):
    inv_freq = jnp.power(ROPE_THETA, -jnp.arange(ROT_HALF, dtype=F32) / ROT_HALF)
    ang = pos.astype(F32)[:, None] * inv_freq[None, :]
    cos, sin = jnp.cos(ang), jnp.sin(ang)
    n = pos.shape[0]
    rest1 = jnp.ones((n, HEAD_DIM - ROT_DIM), F32)
    rest0 = jnp.zeros((n, HEAD_DIM - ROT_DIM), F32)
    z8 = jnp.zeros((n, ROT_HALF), F32)
    c = jnp.concatenate([cos, cos, rest1], axis=-1)
    sa = jnp.concatenate([-sin, z8, rest0], axis=-1)
    sb = jnp.concatenate([z8, sin, rest0], axis=-1)
    reps = LANES // HEAD_DIM
    return tuple(jnp.tile(t, (1, reps)) * scale for t in (c, sa, sb))


def _activations(z, b_merge, vng, vnb):
    ga = jax.nn.silu(z[:, OFF_GA:OFF_U])
    u = jax.nn.gelu(z[:, OFF_U:OFF_VB])
    vn = _layer_norm(jax.nn.gelu(z[:, OFF_VB:OFF_GB]), vng, vnb)
    gb = jax.nn.silu(z[:, OFF_GB:OFF_ML])
    gates = jax.nn.sigmoid(z[:, OFF_ML:] + b_merge)
    return ga, u, vn, gb, gates


def _merge_out(x, a_in, b_in, gates, wpa, wpb, wout, gpost):
    d = x.shape[-1]
    ba = jnp.dot(a_in.astype(BF16), wpa, preferred_element_type=F32)
    bb = jnp.dot(b_in.astype(BF16), wpb, preferred_element_type=F32)
    merged = gates[:, :d] * ba + gates[:, d:] * bb
    y = jnp.dot(merged.astype(BF16), wout, preferred_element_type=F32)
    return x + _rms(y, gpost)


def _const_spec(shape):
    nd = len(shape)
    return pl.BlockSpec(shape, lambda *_: (0,) * nd)


def _layer_spec(arr, layer, block_cols=None, col_block=0):
    rows, cols = arr.shape[-2:]
    block_cols = cols if block_cols is None else block_cols
    if arr.ndim == 2:
        return pl.BlockSpec((rows, block_cols), lambda *_: (0, col_block))
    return pl.BlockSpec((None, rows, block_cols), lambda *_: (layer, 0, col_block))


def _qkv_kernel(seq, first_layer, x_ref, g_ref, w_ref, cq_ref, saq_ref, sbq_ref, ck_ref, sak_ref, sbk_ref, *refs):
    if not first_layer:
        refs = refs[N_GROUPS:]
    q1_ref, kv1_ref, q2_ref, kv2_ref, q3_ref, kv3_ref, p1_ref, p2_ref, p3_ref, za_s, zb_s, dei_s = refs
    step = pl.program_id(0)
    t = x_ref.shape[0]
    tiles_per_seq = seq // t
    last_tile = (jnp.maximum(step - 1, 0) % tiles_per_seq) == tiles_per_seq - 1
    q_refs = (q1_ref, q2_ref, q3_ref)
    kv_refs = (kv1_ref, kv2_ref, kv3_ref)
    p_refs = (p1_ref, p2_ref, p3_ref)
    nslab = GROUP_W // LANES

    @pl.when(step == 0)
    def _():
        zb_s[...] = jnp.zeros(zb_s.shape, F32)

    def store_transposed(p_ref, idx, z):
        zt = z.T.reshape(HEADS_PER_GROUP, HEAD_DIM, z.shape[0])
        if first_layer:
            p_ref[0, idx] = zt
            if p_ref.shape[0] > 1:
                p_ref[1:, idx] = jnp.zeros((p_ref.shape[0] - 1,) + zt.shape, F32)
        else:
            p_ref[idx] = zt

    def rope_k(z, g, rows=slice(None)):
        return _rope(z[rows, QKV_W + g * GROUP_W:QKV_W + (g + 1) * GROUP_W], ck_ref[rows, :], sak_ref[rows, :], sbk_ref[rows, :])

    def body(z_new, z):
        hb = _rms(x_ref[...], g_ref[...]).astype(BF16)
        z_new[...] = jnp.dot(hb, w_ref[...], preferred_element_type=F32)
        slots = iter(range(dei_s.shape[0]))

        def split_by_residue(v, dst_ref, lane0, dil):
            slot = next(slots)
            for s in range(nslab):
                dei_s[slot, s] = v[:, s * LANES:(s + 1) * LANES]
            for r in range(dil):
                rows = [dei_s[slot, s, pl.ds(r, t // dil, stride=dil), :] for s in range(nslab)]
                dst_ref[r, :, lane0:lane0 + GROUP_W] = jnp.concatenate(rows, axis=-1).astype(BF16)

        for g, (window, dil) in enumerate(ATTN_GROUPS):
            zq = _rope(z[:, g * GROUP_W:(g + 1) * GROUP_W], cq_ref[...], saq_ref[...], sbq_ref[...])
            zk = rope_k(z, g)
            zv = z[:, 2 * QKV_W + g * GROUP_W:2 * QKV_W + (g + 1) * GROUP_W]
            if dil == 1:
                q_refs[g][...] = zq.astype(BF16)
                kv_refs[g][:, 0:GROUP_W] = zk.astype(BF16)
                kv_refs[g][:, GROUP_W:2 * GROUP_W] = zv.astype(BF16)
            else:
                split_by_residue(zq, q_refs[g], 0, dil)
                split_by_residue(zk, kv_refs[g], 0, dil)
                split_by_residue(zv, kv_refs[g], GROUP_W, dil)
            if window >= seq:
                store_transposed(p_refs[g], 0, zk)
                store_transposed(p_refs[g], 1, zv)

        @pl.when(last_tile)
        def _():
            for g, (window, dil) in enumerate(ATTN_GROUPS):
                if window < seq:
                    tail = slice(t - p_refs[g].shape[-1], t)
                    store_transposed(p_refs[g], 0, rope_k(z, g, tail))
                    store_transposed(p_refs[g], 1, z[tail, 2 * QKV_W + g * GROUP_W:2 * QKV_W + (g + 1) * GROUP_W])

    @pl.when(step % 2 == 0)
    def _():
        body(za_s, zb_s)

    @pl.when(step % 2 == 1)
    def _():
        body(zb_s, za_s)


def _qkv_call(layer, x, g_pre, w_in, tabs, accs):
    batch, seq, d = x.shape
    depth = g_pre.shape[0]
    first_layer = accs is None
    t = QKV_TILE
    nt = seq // t
    n_tiles = batch * nt
    proj = lambda s: jnp.minimum(s, n_tiles - 1)
    fin = lambda s: jnp.maximum(s - 1, 0)
    tab_spec = pl.BlockSpec((t, LANES), lambda s: (fin(s) % nt, 0))
    in_specs = [pl.BlockSpec((None, t, d), lambda s: (proj(s) // nt, proj(s) % nt, 0)), _layer_spec(g_pre, layer),
                _layer_spec(w_in, layer, 3 * QKV_W, 0)]
    out_specs, out_shape = [], []
    for window, dil in ATTN_GROUPS:
        assert t % (dil * 16) == 0
        if dil == 1:
            q_blk, kv_blk = (None, t, GROUP_W), (None, t, 2 * GROUP_W)
            q_shape, kv_shape = (batch, seq, GROUP_W), (batch, seq, 2 * GROUP_W)
            idx = lambda s: (fin(s) // nt, fin(s) % nt, 0)
        else:
            q_blk, kv_blk = (None, dil, t // dil, GROUP_W), (None, dil, t // dil, 2 * GROUP_W)
            q_shape, kv_shape = (batch, dil, seq // dil, GROUP_W), (batch, dil, seq // dil, 2 * GROUP_W)
            idx = lambda s: (fin(s) // nt, 0, fin(s) % nt, 0)
        out_specs += [pl.BlockSpec(q_blk, idx), pl.BlockSpec(kv_blk, idx)]
        out_shape += [jax.ShapeDtypeStruct(q_shape, BF16), jax.ShapeDtypeStruct(kv_shape, BF16)]
    for window, dil in ATTN_GROUPS:
        keep = min(window, seq)
        assert keep == seq or keep <= t, "a partial window must fit in the last tile"
        blk = min(keep, t)
        every_tile = keep == seq
        lead = 0 if first_layer else layer
        p_blk = (depth if first_layer else None, None, 2, HEADS_PER_GROUP, HEAD_DIM, blk)
        p_idx = lambda s, every_tile=every_tile: (lead, fin(s) // nt, 0, 0, 0, fin(s) % nt if every_tile else 0)
        out_specs.append(pl.BlockSpec(p_blk, p_idx))
        out_shape.append(jax.ShapeDtypeStruct((depth, batch, 2, HEADS_PER_GROUP, HEAD_DIM, keep), F32))
    n_in = len(in_specs) + len(tabs)
    acc_specs = [] if first_layer else [pl.BlockSpec(memory_space=pl.ANY)] * N_GROUPS
    aliases = {} if first_layer else {n_in + g: 2 * N_GROUPS + g for g in range(N_GROUPS)}
    n_split = 3 * sum(dil > 1 for _, dil in ATTN_GROUPS)
    return pl.pallas_call(
        functools.partial(_qkv_kernel, seq, first_layer),
        grid=(n_tiles + 1,),
        in_specs=in_specs + [tab_spec] * len(tabs) + acc_specs,
        out_specs=out_specs,
        out_shape=out_shape,
        input_output_aliases=aliases,
        scratch_shapes=[pltpu.VMEM((t, 3 * QKV_W), F32), pltpu.VMEM((t, 3 * QKV_W), F32),
                        pltpu.VMEM((n_split, GROUP_W // LANES, t, LANES), F32)],
        compiler_params=pltpu.CompilerParams(dimension_semantics=("arbitrary",), vmem_limit_bytes=VMEM_LIMIT),
        name="prompt_qkv",
    )(x, g_pre, w_in, *tabs, *(() if first_layer else accs))


def _attn_block(q, k, v, bias):
    nk = k.shape[0]
    hid = lax.shift_right_logical(lax.broadcasted_iota(jnp.int32, (BLK, GROUP_W), 1), HEAD_DIM.bit_length() - 1)
    qf = q.astype(F32)
    qs = jnp.concatenate([jnp.where(hid == h, qf, 0.0) for h in range(HEADS_PER_GROUP)], axis=0).astype(BF16)
    s = lax.dot_general(qs, k, (((1,), (1,)), ((), ())), preferred_element_type=F32)
    s = s.reshape(HEADS_PER_GROUP, BLK, nk) + bias[None]
    m = jnp.max(s, axis=-1, keepdims=True)
    p = jnp.exp(s - m)
    l = jnp.sum(p, axis=-1, keepdims=True)
    pv = jnp.dot(p.reshape(HEADS_PER_GROUP * BLK, nk).astype(BF16), v, preferred_element_type=F32)
    pv = pv.reshape(HEADS_PER_GROUP, BLK, GROUP_W)
    inv = 1.0 / l
    lse = m + jnp.log(l)
    o = jnp.zeros((BLK, GROUP_W), F32)
    ls = jnp.zeros((BLK, GROUP_W), F32)
    for h in range(HEADS_PER_GROUP):
        o = jnp.where(hid == h, pv[h] * inv[h], o)
        ls = jnp.where(hid == h, lse[h], ls)
    return o, ls


def _attn_kernel(n_weights, q1_ref, q2_ref, q3_ref, kv1_ref, kv2_ref, kv3_ref, band_ref, causal_ref, *refs):
    w_src, out_ref, w_dst = refs[:n_weights], refs[n_weights], refs[n_weights + 1:2 * n_weights + 1]
    o2_s, l2_s, o3_s, l3_s = refs[2 * n_weights + 1:]
    for src, dst in zip(w_src, w_dst):
        dst[...] = src[...].astype(BF16)
    seq = out_ref.shape[0]
    nslab = GROUP_W // LANES
    d2 = ATTN_GROUPS[1][1]
    d3 = ATTN_GROUPS[2][1]
    nblk = seq // BLK
    blocks_per_res2 = nblk // d2
    kcols, vcols = slice(0, GROUP_W), slice(GROUP_W, 2 * GROUP_W)

    def store(o_s, l_s, rows, o, ls):
        for s in range(nslab):
            o_s[s, rows, :] = o[:, s * LANES:(s + 1) * LANES]
            l_s[s, rows, :] = ls[:, s * LANES:(s + 1) * LANES]

    def band_block(q, kv, n):
        if n == 0:
            return _attn_block(q, kv[0:BLK, kcols], kv[0:BLK, vcols], causal_ref[...])
        keys = slice((n - 1) * BLK, (n + 1) * BLK)
        return _attn_block(q, kv[keys, kcols], kv[keys, vcols], band_ref[...])

    for j in range(nblk):
        r, n = divmod(j, blocks_per_res2)
        o, ls = band_block(q2_ref[r, n * BLK:(n + 1) * BLK, :], kv2_ref.at[r], n)
        store(o2_s, l2_s, pl.ds(n * (BLK * d2) + r, BLK, stride=d2), o, ls)
        o, ls = _attn_block(q3_ref[j], kv3_ref[j, :, kcols], kv3_ref[j, :, vcols], causal_ref[...])
        store(o3_s, l3_s, pl.ds(j, BLK, stride=d3), o, ls)

    for j in range(nblk):
        rows = slice(j * BLK, (j + 1) * BLK)
        o1, l1 = band_block(q1_ref[rows, :], kv1_ref, j)
        for s in range(nslab):
            lanes = slice(s * LANES, (s + 1) * LANES)
            l1s, l2, l3 = l1[:, lanes], l2_s[s, rows, :], l3_s[s, rows, :]
            m = jnp.maximum(jnp.maximum(l1s, l2), l3)
            e1, e2, e3 = jnp.exp(l1s - m), jnp.exp(l2 - m), jnp.exp(l3 - m)
            inv = 1.0 / (e1 + e2 + e3)
            out_ref[rows, lanes] = (e1 * inv) * o1[:, lanes] + (e2 * inv) * o2_s[s, rows, :] + (e3 * inv) * o3_s[s, rows, :]


def _attn_call(layer, q1, q2, q3, kv1, kv2, kv3, tables, batch, seq, proj_weights):
    d2 = ATTN_GROUPS[1][1]
    d3 = ATTN_GROUPS[2][1]
    nblk = seq // BLK
    assert seq // d3 == BLK and nblk % d2 == 0 and nblk == d3
    per_seq = lambda a: pl.BlockSpec((None,) + a.shape[1:], lambda b: (b,) + (0,) * (a.ndim - 1))
    scratch = [pltpu.VMEM((GROUP_W // LANES, seq, LANES), F32)] * 4
    w_in_specs, w_out_specs, w_out_shapes = [], [], []
    for w in proj_weights:
        rows, cols = w.shape[1:]
        assert rows % (batch * 16) == 0, "one bf16-tileable row slab per grid step"
        slab = rows // batch
        w_in_specs.append(pl.BlockSpec((None, slab, cols), lambda b: (layer, b, 0)))
        w_out_specs.append(pl.BlockSpec((slab, cols), lambda b: (b, 0)))
        w_out_shapes.append(jax.ShapeDtypeStruct((rows, cols), BF16))
    return pl.pallas_call(
        functools.partial(_attn_kernel, len(proj_weights)),
        grid=(batch,),
        in_specs=[per_seq(a) for a in (q1, q2, q3, kv1, kv2, kv3)] + [_const_spec(t.shape) for t in tables] + w_in_specs,
        out_specs=[pl.BlockSpec((None, seq, GROUP_W), lambda b: (b, 0, 0))] + w_out_specs,
        out_shape=[jax.ShapeDtypeStruct((batch, seq, GROUP_W), F32)] + w_out_shapes,
        scratch_shapes=scratch,
        compiler_params=pltpu.CompilerParams(dimension_semantics=("arbitrary",), vmem_limit_bytes=VMEM_LIMIT),
        name="prompt_attn",
    )(q1, q2, q3, kv1, kv2, kv3, *tables, *proj_weights)


def _attn_bias_tables():
    qi = jnp.arange(BLK)[:, None]
    kj = jnp.arange(2 * BLK)[None, :]
    band = jnp.where((kj >= qi) & (kj <= qi + BLK), 0.0, NEG).astype(F32)
    causal = jnp.where(jnp.arange(BLK)[None, :] <= qi, 0.0, NEG).astype(F32)
    return band, causal


def _spatial_mix(ws, v):
    gd, two = GMLP_GROUP_DIM, 2 * LANES
    starts = [g * gd // LANES * LANES for g in range(GMLP_GROUPS)]
    assert all(s + two >= (g + 1) * gd for g, s in enumerate(starts))
    r = [jnp.dot(ws[g * BLK:(g + 1) * BLK], v[:, s:s + two], preferred_element_type=F32) for g, s in enumerate(starts)]
    lane = lax.broadcasted_iota(jnp.int32, (BLK, LANES), 1)
    slabs = []
    for lo in range(0, GMLP_W, LANES):
        owners = sorted({lo // gd, (lo + LANES - 1) // gd})
        pieces = [r[g][:, lo - starts[g]:lo - starts[g] + LANES] for g in owners]
        slabs.append(pieces[0] if len(owners) == 1 else jnp.where(lane < owners[1] * gd - lo, pieces[0], pieces[1]))
    return jnp.concatenate(slabs, axis=-1)


def _sample_attend(q, kvn, caches):
    column = lambda v, lo: v[lo:lo + GROUP_W].reshape(HEADS_PER_GROUP, HEAD_DIM, 1)
    outs, lses = [], []
    for g, (window, dil) in enumerate(ATTN_GROUPS):
        qg = column(q, g * GROUP_W)
        kc, vc = caches[g]
        kn, vn = column(kvn[g], 0), column(kvn[g], GROUP_W)
        rows = kc.shape[-1]
        sc = jnp.sum(kc * qg, axis=1, keepdims=True)
        r = lax.broadcasted_iota(jnp.int32, (1, 1, rows), 2)
        sc = jnp.where((r & (dil - 1)) == 0, sc, NEG)
        ss = jnp.sum(kn * qg, axis=1, keepdims=True)
        m = jnp.maximum(jnp.max(sc, axis=-1, keepdims=True), ss)
        pc = jnp.exp(sc - m)
        ps = jnp.exp(ss - m)
        l = jnp.sum(pc, axis=-1, keepdims=True) + ps
        o = jnp.sum(vc * pc, axis=-1, keepdims=True) + ps * vn
        outs.append(o / l)
        lses.append(m + jnp.log(l))
    m = jnp.maximum(jnp.maximum(lses[0], lses[1]), lses[2])
    es = [jnp.exp(l - m) for l in lses]
    inv = 1.0 / (es[0] + es[1] + es[2])
    merged = (es[0] * inv) * outs[0] + (es[1] * inv) * outs[1] + (es[2] * inv) * outs[2]
    return merged.reshape(GROUP_W, 1)


def _mlp_kernel(x_ref, attn_ref, gpre_ref, wlo_ref, whi_ref, bm_ref, vng_ref, vnb_ref, ws_ref, bsp_ref,
                wpa_ref, wpb_ref, wout_ref, gpost_ref, qt_ref, kt1_ref, kt2_ref, kt3_ref, c1_ref, c2_ref, c3_ref,
                wsrc_ref, out_ref, sattn_ref, wnext_ref, sacc_s):
    step = pl.program_id(0)
    wnext_ref[...] = wsrc_ref[...].astype(BF16)

    @pl.when(step == 0)
    def _():
        sacc_s[...] = jnp.zeros(sacc_s.shape, F32)

    lane = lax.broadcasted_iota(jnp.int32, (1, sacc_s.shape[-1]), 1)
    for k in range(c1_ref.shape[0]):
        sel = lane == step * c1_ref.shape[0] + k
        pick = lambda ref: jnp.sum(jnp.where(sel, ref[...], 0.0), axis=-1, keepdims=True)
        res = _sample_attend(pick(qt_ref), [pick(r) for r in (kt1_ref, kt2_ref, kt3_ref)],
                             [(c[k, 0], c[k, 1]) for c in (c1_ref, c2_ref, c3_ref)])
        sacc_s[...] = jnp.where(sel, res, sacc_s[...])
    sattn_ref[...] = sacc_s[...]

    x = x_ref[...]
    rows = x.shape[0]
    hb = _rms(x, gpre_ref[...]).astype(BF16)
    z = jnp.concatenate([jnp.dot(hb, wlo_ref[...], preferred_element_type=F32),
                         jnp.dot(hb, whi_ref[...], preferred_element_type=F32)], axis=-1)
    ga, u, vn, gb, gates = _activations(z, bm_ref[...], vng_ref[...], vnb_ref[...])
    wr = lax.broadcasted_iota(jnp.int32, ws_ref.shape, 0) & (BLK - 1)
    wc = lax.broadcasted_iota(jnp.int32, ws_ref.shape, 1)
    ws = jnp.where(wc <= wr, ws_ref[...], 0.0).astype(BF16)
    vnb16 = vn.astype(BF16)
    mix = jnp.concatenate([_spatial_mix(ws, vnb16[c * BLK:(c + 1) * BLK]) + bsp_ref[...]
                           for c in range(rows // BLK)], axis=0)
    out_ref[...] = _merge_out(x, attn_ref[...] * ga, (u * mix) * gb, gates,
                              wpa_ref[...], wpb_ref[...], wout_ref[...], gpost_ref[...])


def _mlp_call(layer, x, attn, g_pre, w_in, b_merge, vng, vnb, ws_stack, bs_full, wpa, wpb, wout, g_post,
              sample_q, sample_kv, caches, w_in_f32):
    n, d = x.shape
    t = MLP_TILE
    steps = n // t
    depth, w_rows, w_cols = w_in_f32.shape
    assert w_rows % (steps * 16) == 0, "each step converts one bf16-tileable row slab of the next weight"
    slab = w_rows // steps
    next_layer = min(layer + 1, depth - 1)
    nb = sample_q.shape[-1]
    assert nb % steps == 0, "every grid step carries the same number of samples"
    spb = nb // steps
    row = lambda i: (i, 0)
    rest_w = w_in.shape[-1] - 3 * QKV_W
    assert rest_w == 2 * 3 * QKV_W, "rest columns are addressed as column blocks 1 and 2 of width 3 * QKV_W"
    w_specs = [_layer_spec(w_in, layer, 3 * QKV_W, 1), _layer_spec(w_in, layer, 3 * QKV_W, 2)]
    stacks = (b_merge, vng, vnb, ws_stack, bs_full, wpa, wpb, wout, g_post)
    cache_specs, cache_views = [], []
    for (window, dil), c in zip(ATTN_GROUPS, caches):
        rows = c.shape[2]
        assert rows == window and window % dil == 0 and dil & (dil - 1) == 0, "cache must hold exactly the window"
        cache_views.append(jnp.transpose(c, (0, 1, 3, 4, 5, 2)))
        cache_specs.append(pl.BlockSpec((None, spb, 2, HEADS_PER_GROUP, HEAD_DIM, rows),
                                        lambda i: (layer, i, 0, 0, 0, 0)))
    return pl.pallas_call(
        _mlp_kernel,
        grid=(steps,),
        in_specs=[pl.BlockSpec((t, d), row), pl.BlockSpec((t, GROUP_W), row), _layer_spec(g_pre, layer)] + w_specs
        + [_layer_spec(s, layer) for s in stacks]
        + [_const_spec(sample_q.shape)] + [_const_spec(kv.shape) for kv in sample_kv] + cache_specs
        + [pl.BlockSpec((None, slab, w_cols), lambda i: (next_layer, i, 0))],
        out_specs=[pl.BlockSpec((t, d), row), _const_spec((GROUP_W, nb)), pl.BlockSpec((slab, w_cols), row)],
        out_shape=[jax.ShapeDtypeStruct((n, d), F32), jax.ShapeDtypeStruct((GROUP_W, nb), F32),
                   jax.ShapeDtypeStruct((w_rows, w_cols), BF16)],
        scratch_shapes=[pltpu.VMEM((GROUP_W, nb), F32)],
        compiler_params=pltpu.CompilerParams(dimension_semantics=("arbitrary",), vmem_limit_bytes=VMEM_LIMIT),
        name="prompt_mlp",
    )(x, attn, g_pre, w_in, w_in, *stacks, sample_q, *sample_kv, *cache_views, w_in_f32)


def _sample_in_kernel(x_ref, g_ref, w_ref, cq_ref, saq_ref, sbq_ref, ck_ref, sak_ref, sbk_ref,
                      bm_ref, vng_ref, vnb_ref,
                      kv1_ref, kv2_ref, kv3_ref, qt_ref, kvt1_ref, kvt2_ref, kvt3_ref,
                      ga_ref, u_ref, vn_ref, gb_ref, gates_ref):
    hb = _rms(x_ref[...], g_ref[...]).astype(BF16)
    z = jnp.dot(hb, w_ref[...], preferred_element_type=F32)

    def store_tiles(dst_ref, rows):
        dst_ref[...] = rows.T

    qs = []
    for g, (kv_ref, kvt_ref) in enumerate(((kv1_ref, kvt1_ref), (kv2_ref, kvt2_ref), (kv3_ref, kvt3_ref))):
        lo = g * GROUP_W
        qs.append(_rope(z[:, lo:lo + GROUP_W], cq_ref[...], saq_ref[...], sbq_ref[...]))
        k = _rope(z[:, QKV_W + lo:QKV_W + lo + GROUP_W], ck_ref[...], sak_ref[...], sbk_ref[...])
        kv = jnp.concatenate([k, z[:, 2 * QKV_W + lo:2 * QKV_W + lo + GROUP_W]], axis=-1)
        kv_ref[...] = kv
        store_tiles(kvt_ref, kv)
    store_tiles(qt_ref, jnp.concatenate(qs, axis=-1))
    ga, u, vn, gb, gates = _activations(z[:, 3 * QKV_W:], bm_ref[...], vng_ref[...], vnb_ref[...])
    ga_ref[...] = ga
    u_ref[...] = u
    vn_ref[...] = vn
    gb_ref[...] = gb
    gates_ref[...] = gates


def _sample_in_call(layer, x, g_pre, w_in, tabs_q, tabs_k, b_merge, vng, vnb):
    n, d = x.shape
    shapes = ([(n, 2 * GROUP_W)] * 3 + [(QKV_W, n)] + [(2 * GROUP_W, n)] * 3
              + [(n, w) for w in (GROUP_W, GMLP_W, GMLP_W, GMLP_W, 2 * d)])
    tabs = (*tabs_q, *tabs_k)
    return pl.pallas_call(
        _sample_in_kernel,
        grid=(1,),
        in_specs=[_const_spec(x.shape), _layer_spec(g_pre, layer), _layer_spec(w_in, layer)]
        + [_const_spec(t.shape) for t in tabs] + [_layer_spec(s, layer) for s in (b_merge, vng, vnb)],
        out_specs=[_const_spec(s) for s in shapes],
        out_shape=[jax.ShapeDtypeStruct(s, F32) for s in shapes],
        compiler_params=pltpu.CompilerParams(dimension_semantics=("arbitrary",), vmem_limit_bytes=VMEM_LIMIT),
        name="sample_in",
    )(x, g_pre, w_in, *tabs, b_merge, vng, vnb)


def _sample_out_kernel(x_ref, attn_ref, ga_ref, u_ref, vn_ref, gb_ref, gates_ref, ws0_ref, bs0_ref,
                       wpa_ref, wpb_ref, wout_ref, gpost_ref, out_ref):
    mix = vn_ref[...] * ws0_ref[...] + bs0_ref[...]
    attn = attn_ref[...].T
    out_ref[...] = _merge_out(x_ref[...], attn * ga_ref[...], (u_ref[...] * mix) * gb_ref[...],
                              gates_ref[...], wpa_ref[...], wpb_ref[...], wout_ref[...], gpost_ref[...])


def _sample_out_call(layer, x, attn, ga, u, vn, gb, gates, ws0, bs0, wpa, wpb, wout, g_post):
    acts = (x, attn, ga, u, vn, gb, gates)
    stacks = (ws0, bs0, wpa, wpb, wout, g_post)
    return pl.pallas_call(
        _sample_out_kernel,
        grid=(1,),
        in_specs=[_const_spec(a.shape) for a in acts] + [_layer_spec(s, layer) for s in stacks],
        out_specs=_const_spec(x.shape),
        out_shape=jax.ShapeDtypeStruct(x.shape, F32),
        compiler_params=pltpu.CompilerParams(dimension_semantics=("arbitrary",), vmem_limit_bytes=VMEM_LIMIT),
        name="sample_out",
    )(*acts, *stacks)


def kernel(x_prompt, x_sample, cache_kv_w128, cache_kv_w512, cache_kv_w2048, norm_pre, w_in, b_merge, v_norm_g, v_norm_b, w_spatial, b_spatial, w_proj_a, w_proj_b, w_out, norm_post):
    batch, seq, d = x_prompt.shape
    nb, dec_seq, _ = x_sample.shape
    depth = w_in.shape[0]
    assert dec_seq == 1, "the sample group decodes one position per step"
    assert seq % QKV_TILE == 0 and (batch * seq) % MLP_TILE == 0
    caches = (cache_kv_w128, cache_kv_w512, cache_kv_w2048)

    w_in_l = w_in[0].astype(BF16)
    ws_stack = w_spatial.reshape(depth, GMLP_GROUPS * BLK, BLK)
    chan_group = jnp.arange(GMLP_W) // GMLP_GROUP_DIM
    bs_full = jnp.swapaxes(b_spatial, 1, 2)[:, :, chan_group]
    ws0 = w_spatial[:, :, 0, 0][:, chan_group][:, None, :]
    bs0 = b_spatial[:, :, 0][:, chan_group][:, None, :]
    row2 = lambda a: a[:, None, :]
    g_pre, g_post, bm, vng, vnb = row2(norm_pre), row2(norm_post), row2(b_merge), row2(v_norm_g), row2(v_norm_b)

    scale = HEAD_DIM ** -0.5
    pos_p = jnp.arange(seq, dtype=jnp.int32)
    pos_s = PAST_LEN + jnp.arange(dec_seq, dtype=jnp.int32)
    tq_p, tk_p = _rope_tables(pos_p, scale), _rope_tables(pos_p, 1.0)
    tq_s, tk_s = _rope_tables(pos_s, scale), _rope_tables(pos_s, 1.0)
    attn_tables = _attn_bias_tables()

    xp = x_prompt
    xs = x_sample.reshape(nb, d)
    kv_p = None
    kv_s = [[] for _ in ATTN_GROUPS]
    v_s = []
    for l in range(depth):
        q1, kv1, q2, kv2, q3, kv3, *kv_p = _qkv_call(l, xp, g_pre, w_in_l, (*tq_p, *tk_p), kv_p)
        attn, wpa_b, wpb_b, wout_b = _attn_call(l, q1, q2, q3, kv1, kv2, kv3, attn_tables, batch, seq,
                                                (w_proj_a, w_proj_b, w_out))
        kn1, kn2, kn3, qt, kt1, kt2, kt3, ga, u, vn, gb, gates = _sample_in_call(l, xs, g_pre, w_in_l, tq_s, tk_s,
                                                                                 bm, vng, vnb)
        xp, attn_s, w_in_l = _mlp_call(l, xp.reshape(batch * seq, d), attn.reshape(batch * seq, GROUP_W), g_pre,
                                       w_in_l, bm, vng, vnb, ws_stack, bs_full, wpa_b, wpb_b, wout_b, g_post,
                                       qt, (kt1, kt2, kt3), caches, w_in)
        xp = xp.reshape(batch, seq, d)
        xs = _sample_out_call(l, xs, attn_s, ga, u, vn, gb, gates, ws0, bs0, wpa_b, wpb_b, wout_b, g_post)
        for g, kn in enumerate((kn1, kn2, kn3)):
            kv_s[g].append(kn.reshape(nb, dec_seq, 2, HEADS_PER_GROUP, HEAD_DIM))
        v_s.append(vn.reshape(nb, dec_seq, GMLP_W))

    new_kv_p = [jnp.transpose(p, (0, 1, 5, 2, 3, 4)) for p in kv_p]
    return (xp, xs.reshape(nb, dec_seq, d), new_kv_p[0], new_kv_p[1], new_kv_p[2],
            jnp.stack(kv_s[0]), jnp.stack(kv_s[1]), jnp.stack(kv_s[2]), jnp.stack(v_s))
```

```python
import functools

import jax
import jax.numpy as jnp
from jax import lax
from jax.experimental import pallas as pl
from jax.experimental.pallas import tpu as pltpu

F32 = jnp.float32
BF16 = jnp.bfloat16

PAST_LEN = 8192
HEAD_DIM = 64
HEADS_PER_GROUP = 4
GROUP_W = HEADS_PER_GROUP * HEAD_DIM
ATTN_GROUPS = ((128, 1), (512, 4), (2048, 16))
N_GROUPS = len(ATTN_GROUPS)
QKV_W = N_GROUPS * GROUP_W
ROT_DIM = HEAD_DIM // 4
ROT_HALF = ROT_DIM // 2
ROPE_THETA = 500000.0
BLK = 128
GMLP_GROUPS = 4
GMLP_GROUP_DIM = 192
GMLP_W = GMLP_GROUPS * GMLP_GROUP_DIM
EPS = 1e-6
LANES = 128
NEG = -1e30

OFF_GA = 0
OFF_U = OFF_GA + GROUP_W
OFF_VB = OFF_U + GMLP_W
OFF_GB = OFF_VB + GMLP_W
OFF_ML = OFF_GB + GMLP_W

QKV_TILE = 512
MLP_TILE = 512
VMEM_LIMIT = 56 * 1024 * 1024


def _rms(x, g):
    return (x * lax.rsqrt(jnp.mean(x * x, axis=-1, keepdims=True) + EPS)) * g


def _layer_norm(x, g, b):
    xc = x - jnp.mean(x, axis=-1, keepdims=True)
    return xc * lax.rsqrt(jnp.mean(xc * xc, axis=-1, keepdims=True) + EPS) * g + b


def _rope(z, tab):
    c, sa, sb = tab[:, 0:LANES], tab[:, LANES:2 * LANES], tab[:, 2 * LANES:3 * LANES]
    outs = []
    for s in range(GROUP_W // LANES):
        zs = z[:, s * LANES:(s + 1) * LANES]
        outs.append(zs * c + pltpu.roll(zs, LANES - ROT_HALF, 1) * sa + pltpu.roll(zs, ROT_HALF, 1) * sb)
    return jnp.concatenate(outs, axis=-1)


def _rope_tables(pos, scale):
    inv_freq = jnp.power(ROPE_THETA, -jnp.arange(ROT_HALF, dtype=F32) / ROT_HALF)
    ang = pos.astype(F32)[:, None] * inv_freq[None, :]
    cos, sin = jnp.cos(ang), jnp.sin(ang)
    n = pos.shape[0]
    rest1 = jnp.ones((n, HEAD_DIM - ROT_DIM), F32)
    rest0 = jnp.zeros((n, HEAD_DIM - ROT_DIM), F32)
    z8 = jnp.zeros((n, ROT_HALF), F32)
    c = jnp.concatenate([cos, cos, rest1], axis=-1)
    sa = jnp.concatenate([-sin, z8, rest0], axis=-1)
    sb = jnp.concatenate([z8, sin, rest0], axis=-1)
    reps = LANES // HEAD_DIM
    return jnp.concatenate([jnp.tile(t, (1, reps)) * scale for t in (c, sa, sb)], axis=-1)


def _activations(z, b_merge, vng, vnb):
    ga = jax.nn.silu(z[:, OFF_GA:OFF_U])
    u = jax.nn.gelu(z[:, OFF_U:OFF_VB])
    vn = _layer_norm(jax.nn.gelu(z[:, OFF_VB:OFF_GB]), vng, vnb)
    gb = jax.nn.silu(z[:, OFF_GB:OFF_ML])
    gates = jax.nn.sigmoid(z[:, OFF_ML:] + b_merge)
    return ga, u, vn, gb, gates


def _merge_out(x, a_in, b_in, gates, wpa, wpb, wout, gpost):
    d = x.shape[-1]
    ba = jnp.dot(a_in.astype(BF16), wpa, preferred_element_type=F32)
    bb = jnp.dot(b_in.astype(BF16), wpb, preferred_element_type=F32)
    merged = gates[:, :d] * ba + gates[:, d:] * bb
    y = jnp.dot(merged.astype(BF16), wout, preferred_element_type=F32)
    return x + _rms(y, gpost)


def _const_spec(shape):
    nd = len(shape)
    return pl.BlockSpec(shape, lambda *_: (0,) * nd)


def _layer_spec(arr, layer, block_cols=None, col_block=0):
    rows, cols = arr.shape[-2:]
    block_cols = cols if block_cols is None else block_cols
    if arr.ndim == 2:
        return pl.BlockSpec((rows, block_cols), lambda *_: (0, col_block))
    return pl.BlockSpec((None, rows, block_cols), lambda *_: (layer, 0, col_block))


def _qkv_kernel(seq, first_layer, x_ref, g_ref, w_ref, tq_ref, tk_ref, *refs):
    if not first_layer:
        refs = refs[N_GROUPS:]
    q1_ref, kv1_ref, q2_ref, kv2_ref, q3_ref, kv3_ref, p1_ref, p2_ref, p3_ref, za_s, zb_s, dei_s = refs
    step = pl.program_id(0)
    t = x_ref.shape[0]
    tiles_per_seq = seq // t
    last_tile = (jnp.maximum(step - 1, 0) % tiles_per_seq) == tiles_per_seq - 1
    q_refs = (q1_ref, q2_ref, q3_ref)
    kv_refs = (kv1_ref, kv2_ref, kv3_ref)
    p_refs = (p1_ref, p2_ref, p3_ref)
    nslab = GROUP_W // LANES

    @pl.when(step == 0)
    def _():
        zb_s[...] = jnp.zeros(zb_s.shape, F32)

    def store_transposed(p_ref, idx, z):
        zt = z.T.reshape(HEADS_PER_GROUP, HEAD_DIM, z.shape[0])
        if first_layer:
            p_ref[0, idx] = zt
            if p_ref.shape[0] > 1:
                p_ref[1:, idx] = jnp.zeros((p_ref.shape[0] - 1,) + zt.shape, F32)
        else:
            p_ref[idx] = zt

    def rope_k(z, g, rows=slice(None)):
        return _rope(z[rows, QKV_W + g * GROUP_W:QKV_W + (g + 1) * GROUP_W], tk_ref[rows, :])

    def body(z_new, z):
        hb = _rms(x_ref[...], g_ref[...]).astype(BF16)
        z_new[...] = jnp.dot(hb, w_ref[...], preferred_element_type=F32)
        slots = iter(range(dei_s.shape[0]))

        def split_by_residue(v, dst_ref, lane0, dil):
            slot = next(slots)
            for s in range(nslab):
                dei_s[slot, s] = v[:, s * LANES:(s + 1) * LANES]
            for r in range(dil):
                rows = [dei_s[slot, s, pl.ds(r, t // dil, stride=dil), :] for s in range(nslab)]
                dst_ref[r, :, lane0:lane0 + GROUP_W] = jnp.concatenate(rows, axis=-1).astype(BF16)

        for g, (window, dil) in enumerate(ATTN_GROUPS):
            zq = _rope(z[:, g * GROUP_W:(g + 1) * GROUP_W], tq_ref[...])
            zk = rope_k(z, g)
            zv = z[:, 2 * QKV_W + g * GROUP_W:2 * QKV_W + (g + 1) * GROUP_W]
            if dil == 1:
                q_refs[g][...] = zq.astype(BF16)
                kv_refs[g][:, 0:GROUP_W] = zk.astype(BF16)
                kv_refs[g][:, GROUP_W:2 * GROUP_W] = zv.astype(BF16)
            else:
                split_by_residue(zq, q_refs[g], 0, dil)
                split_by_residue(zk, kv_refs[g], 0, dil)
                split_by_residue(zv, kv_refs[g], GROUP_W, dil)
            if window >= seq:
                store_transposed(p_refs[g], 0, zk)
                store_transposed(p_refs[g], 1, zv)

        @pl.when(last_tile)
        def _():
            for g, (window, dil) in enumerate(ATTN_GROUPS):
                if window < seq:
                    tail = slice(t - p_refs[g].shape[-1], t)
                    store_transposed(p_refs[g], 0, rope_k(z, g, tail))
                    store_transposed(p_refs[g], 1, z[tail, 2 * QKV_W + g * GROUP_W:2 * QKV_W + (g + 1) * GROUP_W])

    @pl.when(step % 2 == 0)
    def _():
        body(za_s, zb_s)

    @pl.when(step % 2 == 1)
    def _():
        body(zb_s, za_s)


def _qkv_call(layer, x, g_pre, w_in, tabs, accs):
    batch, seq, d = x.shape
    depth = g_pre.shape[0]
    first_layer = accs is None
    t = QKV_TILE
    nt = seq // t
    n_tiles = batch * nt
    proj = lambda s: jnp.minimum(s, n_tiles - 1)
    fin = lambda s: jnp.maximum(s - 1, 0)
    tab_spec = pl.BlockSpec((t, tabs[0].shape[-1]), lambda s: (fin(s) % nt, 0))
    in_specs = [pl.BlockSpec((None, t, d), lambda s: (proj(s) // nt, proj(s) % nt, 0)), _layer_spec(g_pre, layer),
                _layer_spec(w_in, layer, 3 * QKV_W, 0)]
    out_specs, out_shape = [], []
    for window, dil in ATTN_GROUPS:
        assert t % (dil * 16) == 0
        if dil == 1:
            q_blk, kv_blk = (None, t, GROUP_W), (None, t, 2 * GROUP_W)
            q_shape, kv_shape = (batch, seq, GROUP_W), (batch, seq, 2 * GROUP_W)
            idx = lambda s: (fin(s) // nt, fin(s) % nt, 0)
        else:
            q_blk, kv_blk = (None, dil, t // dil, GROUP_W), (None, dil, t // dil, 2 * GROUP_W)
            q_shape, kv_shape = (batch, dil, seq // dil, GROUP_W), (batch, dil, seq // dil, 2 * GROUP_W)
            idx = lambda s: (fin(s) // nt, 0, fin(s) % nt, 0)
        out_specs += [pl.BlockSpec(q_blk, idx), pl.BlockSpec(kv_blk, idx)]
        out_shape += [jax.ShapeDtypeStruct(q_shape, BF16), jax.ShapeDtypeStruct(kv_shape, BF16)]
    for window, dil in ATTN_GROUPS:
        keep = min(window, seq)
        assert keep == seq or keep <= t, "a partial window must fit in the last tile"
        blk = min(keep, t)
        every_tile = keep == seq
        lead = 0 if first_layer else layer
        p_blk = (depth if first_layer else None, None, 2, HEADS_PER_GROUP, HEAD_DIM, blk)
        p_idx = lambda s, every_tile=every_tile: (lead, fin(s) // nt, 0, 0, 0, fin(s) % nt if every_tile else 0)
        out_specs.append(pl.BlockSpec(p_blk, p_idx))
        out_shape.append(jax.ShapeDtypeStruct((depth, batch, 2, HEADS_PER_GROUP, HEAD_DIM, keep), F32))
    n_in = len(in_specs) + len(tabs)
    acc_specs = [] if first_layer else [pl.BlockSpec(memory_space=pl.ANY)] * N_GROUPS
    aliases = {} if first_layer else {n_in + g: 2 * N_GROUPS + g for g in range(N_GROUPS)}
    n_split = 3 * sum(dil > 1 for _, dil in ATTN_GROUPS)
    return pl.pallas_call(
        functools.partial(_qkv_kernel, seq, first_layer),
        grid=(n_tiles + 1,),
        in_specs=in_specs + [tab_spec] * len(tabs) + acc_specs,
        out_specs=out_specs,
        out_shape=out_shape,
        input_output_aliases=aliases,
        scratch_shapes=[pltpu.VMEM((t, 3 * QKV_W), F32), pltpu.VMEM((t, 3 * QKV_W), F32),
                        pltpu.VMEM((n_split, GROUP_W // LANES, t, LANES), F32)],
        compiler_params=pltpu.CompilerParams(dimension_semantics=("arbitrary",), vmem_limit_bytes=VMEM_LIMIT),
        name="prompt_qkv",
    )(x, g_pre, w_in, *tabs, *(() if first_layer else accs))


def _attn_block(q, k, v, bias):
    nk = k.shape[0]
    hid = lax.shift_right_logical(lax.broadcasted_iota(jnp.int32, (BLK, GROUP_W), 1), HEAD_DIM.bit_length() - 1)
    qf = q.astype(F32)
    qs = jnp.concatenate([jnp.where(hid == h, qf, 0.0) for h in range(HEADS_PER_GROUP)], axis=0).astype(BF16)
    s = lax.dot_general(qs, k, (((1,), (1,)), ((), ())), preferred_element_type=F32)
    s = s.reshape(HEADS_PER_GROUP, BLK, nk) + bias[None]
    m = jnp.max(s, axis=-1, keepdims=True)
    p = jnp.exp(s - m)
    l = jnp.sum(p, axis=-1, keepdims=True)
    pv = jnp.dot(p.reshape(HEADS_PER_GROUP * BLK, nk).astype(BF16), v, preferred_element_type=F32)
    pv = pv.reshape(HEADS_PER_GROUP, BLK, GROUP_W)
    inv = 1.0 / l
    lse = m + jnp.log(l)
    o = jnp.zeros((BLK, GROUP_W), F32)
    ls = jnp.zeros((BLK, GROUP_W), F32)
    for h in range(HEADS_PER_GROUP):
        o = jnp.where(hid == h, pv[h] * inv[h], o)
        ls = jnp.where(hid == h, lse[h], ls)
    return o, ls


def _attn_kernel(q1_ref, q2_ref, q3_ref, kv1_ref, kv2_ref, kv3_ref, band_ref, causal_ref, out_ref,
                 o2_s, l2_s, o3_s, l3_s):
    seq = out_ref.shape[0]
    nslab = GROUP_W // LANES
    d2 = ATTN_GROUPS[1][1]
    d3 = ATTN_GROUPS[2][1]
    nblk = seq // BLK
    blocks_per_res2 = nblk // d2
    kcols, vcols = slice(0, GROUP_W), slice(GROUP_W, 2 * GROUP_W)

    def store(o_s, l_s, rows, o, ls):
        for s in range(nslab):
            o_s[s, rows, :] = o[:, s * LANES:(s + 1) * LANES]
            l_s[s, rows, :] = ls[:, s * LANES:(s + 1) * LANES]

    def band_block(q, kv, n):
        if n == 0:
            return _attn_block(q, kv[0:BLK, kcols], kv[0:BLK, vcols], causal_ref[...])
        keys = slice((n - 1) * BLK, (n + 1) * BLK)
        return _attn_block(q, kv[keys, kcols], kv[keys, vcols], band_ref[...])

    for j in range(nblk):
        r, n = divmod(j, blocks_per_res2)
        o, ls = band_block(q2_ref[r, n * BLK:(n + 1) * BLK, :], kv2_ref.at[r], n)
        store(o2_s, l2_s, pl.ds(n * (BLK * d2) + r, BLK, stride=d2), o, ls)
        o, ls = _attn_block(q3_ref[j], kv3_ref[j, :, kcols], kv3_ref[j, :, vcols], causal_ref[...])
        store(o3_s, l3_s, pl.ds(j, BLK, stride=d3), o, ls)

    for j in range(nblk):
        rows = slice(j * BLK, (j + 1) * BLK)
        o1, l1 = band_block(q1_ref[rows, :], kv1_ref, j)
        for s in range(nslab):
            lanes = slice(s * LANES, (s + 1) * LANES)
            l1s, l2, l3 = l1[:, lanes], l2_s[s, rows, :], l3_s[s, rows, :]
            m = jnp.maximum(jnp.maximum(l1s, l2), l3)
            e1, e2, e3 = jnp.exp(l1s - m), jnp.exp(l2 - m), jnp.exp(l3 - m)
            inv = 1.0 / (e1 + e2 + e3)
            out_ref[rows, lanes] = (e1 * inv) * o1[:, lanes] + (e2 * inv) * o2_s[s, rows, :] + (e3 * inv) * o3_s[s, rows, :]


def _attn_call(q1, q2, q3, kv1, kv2, kv3, tables, batch, seq):
    d2 = ATTN_GROUPS[1][1]
    d3 = ATTN_GROUPS[2][1]
    nblk = seq // BLK
    assert seq // d3 == BLK and nblk % d2 == 0 and nblk == d3
    per_seq = lambda a: pl.BlockSpec((None,) + a.shape[1:], lambda b: (b,) + (0,) * (a.ndim - 1))
    scratch = [pltpu.VMEM((GROUP_W // LANES, seq, LANES), F32)] * 4
    return pl.pallas_call(
        _attn_kernel,
        grid=(batch,),
        in_specs=[per_seq(a) for a in (q1, q2, q3, kv1, kv2, kv3)] + [_const_spec(t.shape) for t in tables],
        out_specs=pl.BlockSpec((None, seq, GROUP_W), lambda b: (b, 0, 0)),
        out_shape=jax.ShapeDtypeStruct((batch, seq, GROUP_W), F32),
        scratch_shapes=scratch,
        compiler_params=pltpu.CompilerParams(dimension_semantics=("arbitrary",), vmem_limit_bytes=VMEM_LIMIT),
        name="prompt_attn",
    )(q1, q2, q3, kv1, kv2, kv3, *tables)


def _attn_bias_tables():
    qi = jnp.arange(BLK)[:, None]
    kj = jnp.arange(2 * BLK)[None, :]
    band = jnp.where((kj >= qi) & (kj <= qi + BLK), 0.0, NEG).astype(F32)
    causal = jnp.where(jnp.arange(BLK)[None, :] <= qi, 0.0, NEG).astype(F32)
    return band, causal


def _spatial_mix(ws, v):
    gd, two = GMLP_GROUP_DIM, 2 * LANES
    starts = [g * gd // LANES * LANES for g in range(GMLP_GROUPS)]
    assert all(s + two >= (g + 1) * gd for g, s in enumerate(starts))
    r = [jnp.dot(ws[g * BLK:(g + 1) * BLK], v[:, s:s + two], preferred_element_type=F32) for g, s in enumerate(starts)]
    lane = lax.broadcasted_iota(jnp.int32, (BLK, LANES), 1)
    slabs = []
    for lo in range(0, GMLP_W, LANES):
        owners = sorted({lo // gd, (lo + LANES - 1) // gd})
        pieces = [r[g][:, lo - starts[g]:lo - starts[g] + LANES] for g in owners]
        slabs.append(pieces[0] if len(owners) == 1 else jnp.where(lane < owners[1] * gd - lo, pieces[0], pieces[1]))
    return jnp.concatenate(slabs, axis=-1)


def _sample_attend(q, kvn, caches):
    column = lambda v, lo: v[lo:lo + GROUP_W].reshape(HEADS_PER_GROUP, HEAD_DIM, 1)
    outs, lses = [], []
    for g, (window, dil) in enumerate(ATTN_GROUPS):
        qg = column(q, g * GROUP_W)
        kc, vc = caches[g]
        kn, vn = column(kvn[g], 0), column(kvn[g], GROUP_W)
        rows = kc.shape[-1]
        sc = jnp.sum(kc * qg, axis=1, keepdims=True)
        r = lax.broadcasted_iota(jnp.int32, (1, 1, rows), 2)
        sc = jnp.where((r & (dil - 1)) == 0, sc, NEG)
        ss = jnp.sum(kn * qg, axis=1, keepdims=True)
        m = jnp.maximum(jnp.max(sc, axis=-1, keepdims=True), ss)
        pc = jnp.exp(sc - m)
        ps = jnp.exp(ss - m)
        l = jnp.sum(pc, axis=-1, keepdims=True) + ps
        o = jnp.sum(vc * pc, axis=-1, keepdims=True) + ps * vn
        outs.append(o / l)
        lses.append(m + jnp.log(l))
    m = jnp.maximum(jnp.maximum(lses[0], lses[1]), lses[2])
    es = [jnp.exp(l - m) for l in lses]
    inv = 1.0 / (es[0] + es[1] + es[2])
    merged = (es[0] * inv) * outs[0] + (es[1] * inv) * outs[1] + (es[2] * inv) * outs[2]
    return merged.reshape(GROUP_W, 1)


def _mlp_kernel(x_ref, attn_ref, gpre_ref, wlo_ref, whi_ref, bm_ref, vng_ref, vnb_ref, ws_ref, bsp_ref,
                wpa_ref, wpb_ref, wout_ref, gpost_ref, qt_ref, kt1_ref, kt2_ref, kt3_ref, c1_ref, c2_ref, c3_ref,
                wsrc_ref, out_ref, sattn_ref, wnext_ref, sacc_s):
    step = pl.program_id(0)
    wnext_ref[...] = wsrc_ref[...].astype(BF16)

    @pl.when(step == 0)
    def _():
        sacc_s[...] = jnp.zeros(sacc_s.shape, F32)

    lane = lax.broadcasted_iota(jnp.int32, (1, sacc_s.shape[-1]), 1)
    for k in range(c1_ref.shape[0]):
        sel = lane == step * c1_ref.shape[0] + k
        pick = lambda ref: jnp.sum(jnp.where(sel, ref[...], 0.0), axis=-1, keepdims=True)
        res = _sample_attend(pick(qt_ref), [pick(r) for r in (kt1_ref, kt2_ref, kt3_ref)],
                             [(c[k, 0], c[k, 1]) for c in (c1_ref, c2_ref, c3_ref)])
        sacc_s[...] = jnp.where(sel, res, sacc_s[...])
    sattn_ref[...] = sacc_s[...]

    x = x_ref[...]
    rows = x.shape[0]
    hb = _rms(x, gpre_ref[...]).astype(BF16)
    z = jnp.concatenate([jnp.dot(hb, wlo_ref[...], preferred_element_type=F32),
                         jnp.dot(hb, whi_ref[...], preferred_element_type=F32)], axis=-1)
    ga, u, vn, gb, gates = _activations(z, bm_ref[...], vng_ref[...], vnb_ref[...])
    wr = lax.broadcasted_iota(jnp.int32, ws_ref.shape, 0) & (BLK - 1)
    wc = lax.broadcasted_iota(jnp.int32, ws_ref.shape, 1)
    ws = jnp.where(wc <= wr, ws_ref[...], 0.0).astype(BF16)
    vnb16 = vn.astype(BF16)
    mix = jnp.concatenate([_spatial_mix(ws, vnb16[c * BLK:(c + 1) * BLK]) + bsp_ref[...]
                           for c in range(rows // BLK)], axis=0)
    out_ref[...] = _merge_out(x, attn_ref[...] * ga, (u * mix) * gb, gates,
                              wpa_ref[...], wpb_ref[...], wout_ref[...], gpost_ref[...])


def _mlp_call(layer, x, attn, g_pre, w_in, b_merge, vng, vnb, ws_stack, bs_full, wpa, wpb, wout, g_post,
              sample_q, sample_kv, caches, w_in_f32):
    n, d = x.shape
    t = MLP_TILE
    steps = n // t
    depth, w_rows, w_cols = w_in_f32.shape
    assert w_rows % (steps * 16) == 0, "each step converts one bf16-tileable row slab of the next weight"
    slab = w_rows // steps
    next_layer = min(layer + 1, depth - 1)
    nb = sample_q.shape[-1]
    assert nb % steps == 0, "every grid step carries the same number of samples"
    spb = nb // steps
    row = lambda i: (i, 0)
    rest_w = w_in.shape[-1] - 3 * QKV_W
    assert rest_w == 2 * 3 * QKV_W, "rest columns are addressed as column blocks 1 and 2 of width 3 * QKV_W"
    w_specs = [_layer_spec(w_in, layer, 3 * QKV_W, 1), _layer_spec(w_in, layer, 3 * QKV_W, 2)]
    stacks = (b_merge, vng, vnb, ws_stack, bs_full, wpa, wpb, wout, g_post)
    cache_specs, cache_views = [], []
    for (window, dil), c in zip(ATTN_GROUPS, caches):
        rows = c.shape[2]
        assert rows == window and window % dil == 0 and dil & (dil - 1) == 0, "cache must hold exactly the window"
        cache_views.append(jnp.transpose(c, (0, 1, 3, 4, 5, 2)))
        cache_specs.append(pl.BlockSpec((None, spb, 2, HEADS_PER_GROUP, HEAD_DIM, rows),
                                        lambda i: (layer, i, 0, 0, 0, 0)))
    return pl.pallas_call(
        _mlp_kernel,
        grid=(steps,),
        in_specs=[pl.BlockSpec((t, d), row), pl.BlockSpec((t, GROUP_W), row), _layer_spec(g_pre, layer)] + w_specs
        + [_layer_spec(s, layer) for s in stacks]
        + [_const_spec(sample_q.shape)] + [_const_spec(kv.shape) for kv in sample_kv] + cache_specs
        + [pl.BlockSpec((None, slab, w_cols), lambda i: (next_layer, i, 0))],
        out_specs=[pl.BlockSpec((t, d), row), _const_spec((GROUP_W, nb)), pl.BlockSpec((slab, w_cols), row)],
        out_shape=[jax.ShapeDtypeStruct((n, d), F32), jax.ShapeDtypeStruct((GROUP_W, nb), F32),
                   jax.ShapeDtypeStruct((w_rows, w_cols), BF16)],
        scratch_shapes=[pltpu.VMEM((GROUP_W, nb), F32)],
        compiler_params=pltpu.CompilerParams(dimension_semantics=("arbitrary",), vmem_limit_bytes=VMEM_LIMIT),
        name="prompt_mlp",
    )(x, attn, g_pre, w_in, w_in, *stacks, sample_q, *sample_kv, *cache_views, w_in_f32)


def _sample_in_kernel(x_ref, g_ref, w_ref, tq_ref, tk_ref,
                      bm_ref, vng_ref, vnb_ref,
                      kv1_ref, kv2_ref, kv3_ref, qt_ref, kvt1_ref, kvt2_ref, kvt3_ref,
                      ga_ref, u_ref, vn_ref, gb_ref, gates_ref):
    hb = _rms(x_ref[...], g_ref[...]).astype(BF16)
    z = jnp.dot(hb, w_ref[...], preferred_element_type=F32)

    def store_tiles(dst_ref, rows):
        dst_ref[...] = rows.T

    qs = []
    for g, (kv_ref, kvt_ref) in enumerate(((kv1_ref, kvt1_ref), (kv2_ref, kvt2_ref), (kv3_ref, kvt3_ref))):
        lo = g * GROUP_W
        qs.append(_rope(z[:, lo:lo + GROUP_W], tq_ref[...]))
        k = _rope(z[:, QKV_W + lo:QKV_W + lo + GROUP_W], tk_ref[...])
        kv = jnp.concatenate([k, z[:, 2 * QKV_W + lo:2 * QKV_W + lo + GROUP_W]], axis=-1)
        kv_ref[...] = kv
        store_tiles(kvt_ref, kv)
    store_tiles(qt_ref, jnp.concatenate(qs, axis=-1))
    ga, u, vn, gb, gates = _activations(z[:, 3 * QKV_W:], bm_ref[...], vng_ref[...], vnb_ref[...])
    ga_ref[...] = ga
    u_ref[...] = u
    vn_ref[...] = vn
    gb_ref[...] = gb
    gates_ref[...] = gates


def _sample_in_call(layer, x, g_pre, w_in, tabs_q, tabs_k, b_merge, vng, vnb):
    n, d = x.shape
    shapes = ([(n, 2 * GROUP_W)] * 3 + [(QKV_W, n)] + [(2 * GROUP_W, n)] * 3
              + [(n, w) for w in (GROUP_W, GMLP_W, GMLP_W, GMLP_W, 2 * d)])
    tabs = (tabs_q, tabs_k)
    return pl.pallas_call(
        _sample_in_kernel,
        grid=(1,),
        in_specs=[_const_spec(x.shape), _layer_spec(g_pre, layer), _layer_spec(w_in, layer)]
        + [_const_spec(t.shape) for t in tabs] + [_layer_spec(s, layer) for s in (b_merge, vng, vnb)],
        out_specs=[_const_spec(s) for s in shapes],
        out_shape=[jax.ShapeDtypeStruct(s, F32) for s in shapes],
        compiler_params=pltpu.CompilerParams(dimension_semantics=("arbitrary",), vmem_limit_bytes=VMEM_LIMIT),
        name="sample_in",
    )(x, g_pre, w_in, *tabs, b_merge, vng, vnb)


def _sample_out_kernel(x_ref, attn_ref, ga_ref, u_ref, vn_ref, gb_ref, gates_ref, ws0_ref, bs0_ref,
                       wpa_ref, wpb_ref, wout_ref, gpost_ref, out_ref):
    mix = vn_ref[...] * ws0_ref[...] + bs0_ref[...]
    attn = attn_ref[...].T
    out_ref[...] = _merge_out(x_ref[...], attn * ga_ref[...], (u_ref[...] * mix) * gb_ref[...],
                              gates_ref[...], wpa_ref[...], wpb_ref[...], wout_ref[...], gpost_ref[...])


def _sample_out_call(layer, x, attn, ga, u, vn, gb, gates, ws0, bs0, wpa, wpb, wout, g_post):
    acts = (x, attn, ga, u, vn, gb, gates)
    stacks = (ws0, bs0, wpa, wpb, wout, g_post)
    return pl.pallas_call(
        _sample_out_kernel,
        grid=(1,),
        in_specs=[_const_spec(a.shape) for a in acts] + [_layer_spec(s, layer) for s in stacks],
        out_specs=_const_spec(x.shape),
        out_shape=jax.ShapeDtypeStruct(x.shape, F32),
        compiler_params=pltpu.CompilerParams(dimension_semantics=("arbitrary",), vmem_limit_bytes=VMEM_LIMIT),
        name="sample_out",
    )(*acts, *stacks)


def kernel(x_prompt, x_sample, cache_kv_w128, cache_kv_w512, cache_kv_w2048, norm_pre, w_in, b_merge, v_norm_g, v_norm_b, w_spatial, b_spatial, w_proj_a, w_proj_b, w_out, norm_post):
    batch, seq, d = x_prompt.shape
    nb, dec_seq, _ = x_sample.shape
    depth = w_in.shape[0]
    assert dec_seq == 1, "the sample group decodes one position per step"
    assert seq % QKV_TILE == 0 and (batch * seq) % MLP_TILE == 0
    caches = (cache_kv_w128, cache_kv_w512, cache_kv_w2048)

    w_in_l = w_in[0].astype(BF16)
    wpa_b, wpb_b, wout_b = w_proj_a.astype(BF16), w_proj_b.astype(BF16), w_out.astype(BF16)
    ws_stack = w_spatial.reshape(depth, GMLP_GROUPS * BLK, BLK)
    chan_group = jnp.arange(GMLP_W) // GMLP_GROUP_DIM
    bs_full = jnp.swapaxes(b_spatial, 1, 2)[:, :, chan_group]
    ws0 = w_spatial[:, :, 0, 0][:, chan_group][:, None, :]
    bs0 = b_spatial[:, :, 0][:, chan_group][:, None, :]
    row2 = lambda a: a[:, None, :]
    g_pre, g_post, bm, vng, vnb = row2(norm_pre), row2(norm_post), row2(b_merge), row2(v_norm_g), row2(v_norm_b)

    scale = HEAD_DIM ** -0.5
    pos_p = jnp.arange(seq, dtype=jnp.int32)
    pos_s = PAST_LEN + jnp.arange(dec_seq, dtype=jnp.int32)
    tq_p, tk_p = _rope_tables(pos_p, scale), _rope_tables(pos_p, 1.0)
    tq_s, tk_s = _rope_tables(pos_s, scale), _rope_tables(pos_s, 1.0)
    attn_tables = _attn_bias_tables()

    xp = x_prompt
    xs = x_sample.reshape(nb, d)
    kv_p = None
    kv_s = [[] for _ in ATTN_GROUPS]
    v_s = []
    for l in range(depth):
        q1, kv1, q2, kv2, q3, kv3, *kv_p = _qkv_call(l, xp, g_pre, w_in_l, (tq_p, tk_p), kv_p)
        attn = _attn_call(q1, q2, q3, kv1, kv2, kv3, attn_tables, batch, seq)
        kn1, kn2, kn3, qt, kt1, kt2, kt3, ga, u, vn, gb, gates = _sample_in_call(l, xs, g_pre, w_in_l, tq_s, tk_s,
                                                                                 bm, vng, vnb)
        xp, attn_s, w_in_l = _mlp_call(l, xp.reshape(batch * seq, d), attn.reshape(batch * seq, GROUP_W), g_pre,
                                       w_in_l, bm, vng, vnb, ws_stack, bs_full, wpa_b, wpb_b, wout_b, g_post,
                                       qt, (kt1, kt2, kt3), caches, w_in)
        xp = xp.reshape(batch, seq, d)
        xs = _sample_out_call(l, xs, attn_s, ga, u, vn, gb, gates, ws0, bs0, wpa_b, wpb_b, wout_b, g_post)
        for g, kn in enumerate((kn1, kn2, kn3)):
            kv_s[g].append(kn.reshape(nb, dec_seq, 2, HEADS_PER_GROUP, HEAD_DIM))
        v_s.append(vn.reshape(nb, dec_seq, GMLP_W))

    new_kv_p = [jnp.transpose(p, (0, 1, 5, 2, 3, 4)) for p in kv_p]
    return (xp, xs.reshape(nb, dec_seq, d), new_kv_p[0], new_kv_p[1], new_kv_p[2],
            jnp.stack(kv_s[0]), jnp.stack(kv_s[1]), jnp.stack(kv_s[2]), jnp.stack(v_s))
```

```python
import functools

import jax
import jax.numpy as jnp
from jax import lax
from jax.experimental import pallas as pl
from jax.experimental.pallas import tpu as pltpu

F32 = jnp.float32
BF16 = jnp.bfloat16

PAST_LEN = 8192
HEAD_DIM = 64
HEADS_PER_GROUP = 4
GROUP_W = HEADS_PER_GROUP * HEAD_DIM
ATTN_GROUPS = ((128, 1), (512, 4), (2048, 16))
N_GROUPS = len(ATTN_GROUPS)
QKV_W = N_GROUPS * GROUP_W
ROT_DIM = HEAD_DIM // 4
ROT_HALF = ROT_DIM // 2
ROPE_THETA = 500000.0
BLK = 128
GMLP_GROUPS = 4
GMLP_GROUP_DIM = 192
GMLP_W = GMLP_GROUPS * GMLP_GROUP_DIM
EPS = 1e-6
LANES = 128
NEG = -1e30

OFF_GA = 0
OFF_U = OFF_GA + GROUP_W
OFF_VB = OFF_U + GMLP_W
OFF_GB = OFF_VB + GMLP_W
OFF_ML = OFF_GB + GMLP_W

QKV_TILE = 512
MLP_TILE = 512
VMEM_LIMIT = 56 * 1024 * 1024


def _rms(x, g):
    return (x * lax.rsqrt(jnp.mean(x * x, axis=-1, keepdims=True) + EPS)) * g


def _layer_norm(x, g, b):
    xc = x - jnp.mean(x, axis=-1, keepdims=True)
    return xc * lax.rsqrt(jnp.mean(xc * xc, axis=-1, keepdims=True) + EPS) * g + b


def _rope(z, tab):
    c, sa, sb = tab[:, 0:LANES], tab[:, LANES:2 * LANES], tab[:, 2 * LANES:3 * LANES]
    outs = []
    for s in range(GROUP_W // LANES):
        zs = z[:, s * LANES:(s + 1) * LANES]
        outs.append(zs * c + pltpu.roll(zs, LANES - ROT_HALF, 1) * sa + pltpu.roll(zs, ROT_HALF, 1) * sb)
    return jnp.concatenate(outs, axis=-1)


def _rope_tables(pos, scale):
    inv_freq = jnp.power(ROPE_THETA, -jnp.arange(ROT_HALF, dtype=F32) / ROT_HALF)
    ang = pos.astype(F32)[:, None] * inv_freq[None, :]
    cos, sin = jnp.cos(ang), jnp.sin(ang)
    n = pos.shape[0]
    rest1 = jnp.ones((n, HEAD_DIM - ROT_DIM), F32)
    rest0 = jnp.zeros((n, HEAD_DIM - ROT_DIM), F32)
    z8 = jnp.zeros((n, ROT_HALF), F32)
    c = jnp.concatenate([cos, cos, rest1], axis=-1)
    sa = jnp.concatenate([-sin, z8, rest0], axis=-1)
    sb = jnp.concatenate([z8, sin, rest0], axis=-1)
    reps = LANES // HEAD_DIM
    return jnp.concatenate([jnp.tile(t, (1, reps)) * scale for t in (c, sa, sb)], axis=-1)


def _activations(z, b_merge, vng, vnb):
    ga = jax.nn.silu(z[:, OFF_GA:OFF_U])
    u = jax.nn.gelu(z[:, OFF_U:OFF_VB])
    vn = _layer_norm(jax.nn.gelu(z[:, OFF_VB:OFF_GB]), vng, vnb)
    gb = jax.nn.silu(z[:, OFF_GB:OFF_ML])
    gates = jax.nn.sigmoid(z[:, OFF_ML:] + b_merge)
    return ga, u, vn, gb, gates


def _merge_out(x, a_in, b_in, gates, wpa, wpb, wout, gpost):
    d = x.shape[-1]
    ba = jnp.dot(a_in.astype(BF16), wpa, preferred_element_type=F32)
    bb = jnp.dot(b_in.astype(BF16), wpb, preferred_element_type=F32)
    merged = gates[:, :d] * ba + gates[:, d:] * bb
    y = jnp.dot(merged.astype(BF16), wout, preferred_element_type=F32)
    return x + _rms(y, gpost)


def _const_spec(shape):
    nd = len(shape)
    return pl.BlockSpec(shape, lambda *_: (0,) * nd)


def _layer_spec(arr, layer, block_cols=None, col_block=0):
    rows, cols = arr.shape[-2:]
    block_cols = cols if block_cols is None else block_cols
    if arr.ndim == 2:
        return pl.BlockSpec((rows, block_cols), lambda *_: (0, col_block))
    return pl.BlockSpec((None, rows, block_cols), lambda *_: (layer, 0, col_block))


def _qkv_kernel(seq, first_layer, x_ref, g_ref, w_ref, tq_ref, tk_ref, *refs):
    if not first_layer:
        refs = refs[N_GROUPS:]
    q1_ref, kv1_ref, q2_ref, kv2_ref, q3_ref, kv3_ref, p1_ref, p2_ref, p3_ref, za_s, zb_s, dei_s = refs
    step = pl.program_id(0)
    t = x_ref.shape[0]
    q_refs = (q1_ref, q2_ref, q3_ref)
    kv_refs = (kv1_ref, kv2_ref, kv3_ref)
    p_refs = (p1_ref, p2_ref, p3_ref)
    nslab = GROUP_W // LANES

    @pl.when(step == 0)
    def _():
        zb_s[...] = jnp.zeros(zb_s.shape, F32)

    def store_transposed(p_ref, idx, z):
        zt = z.T.reshape(HEADS_PER_GROUP, HEAD_DIM, z.shape[0])
        if first_layer:
            p_ref[0, idx] = zt
            if p_ref.shape[0] > 1:
                p_ref[1:, idx] = jnp.zeros((p_ref.shape[0] - 1,) + zt.shape, F32)
        else:
            p_ref[idx] = zt

    def body(z_new, z):
        hb = _rms(x_ref[...], g_ref[...]).astype(BF16)
        z_new[...] = jnp.dot(hb, w_ref[...], preferred_element_type=F32)
        slots = iter(range(dei_s.shape[0]))

        def split_by_residue(v, dst_ref, lane0, dil):
            slot = next(slots)
            for s in range(nslab):
                dei_s[slot, s] = v[:, s * LANES:(s + 1) * LANES]
            for r in range(dil):
                rows = [dei_s[slot, s, pl.ds(r, t // dil, stride=dil), :] for s in range(nslab)]
                dst_ref[r, :, lane0:lane0 + GROUP_W] = jnp.concatenate(rows, axis=-1).astype(BF16)

        for g, (window, dil) in enumerate(ATTN_GROUPS):
            zq = _rope(z[:, g * GROUP_W:(g + 1) * GROUP_W], tq_ref[...])
            zk = _rope(z[:, QKV_W + g * GROUP_W:QKV_W + (g + 1) * GROUP_W], tk_ref[...])
            zv = z[:, 2 * QKV_W + g * GROUP_W:2 * QKV_W + (g + 1) * GROUP_W]
            if dil == 1:
                q_refs[g][...] = zq.astype(BF16)
                kv_refs[g][:, 0:GROUP_W] = zk.astype(BF16)
                kv_refs[g][:, GROUP_W:2 * GROUP_W] = zv.astype(BF16)
            else:
                split_by_residue(zq, q_refs[g], 0, dil)
                split_by_residue(zk, kv_refs[g], 0, dil)
                split_by_residue(zv, kv_refs[g], GROUP_W, dil)
            keep = p_refs[g].shape[-1]
            store_transposed(p_refs[g], 0, zk[t - keep:])
            store_transposed(p_refs[g], 1, zv[t - keep:])

    @pl.when(step % 2 == 0)
    def _():
        body(za_s, zb_s)

    @pl.when(step % 2 == 1)
    def _():
        body(zb_s, za_s)


def _qkv_call(layer, x, g_pre, w_in, tabs, accs):
    batch, seq, d = x.shape
    depth = g_pre.shape[0]
    first_layer = accs is None
    t = QKV_TILE
    nt = seq // t
    n_tiles = batch * nt
    proj = lambda s: jnp.minimum(s, n_tiles - 1)
    fin = lambda s: jnp.maximum(s - 1, 0)
    tab_spec = pl.BlockSpec((t, tabs[0].shape[-1]), lambda s: (fin(s) % nt, 0))
    in_specs = [pl.BlockSpec((None, t, d), lambda s: (proj(s) // nt, proj(s) % nt, 0)), _layer_spec(g_pre, layer),
                _layer_spec(w_in, layer, 3 * QKV_W, 0)]
    out_specs, out_shape = [], []
    for window, dil in ATTN_GROUPS:
        assert t % (dil * 16) == 0
        if dil == 1:
            q_blk, kv_blk = (None, t, GROUP_W), (None, t, 2 * GROUP_W)
            q_shape, kv_shape = (batch, seq, GROUP_W), (batch, seq, 2 * GROUP_W)
            idx = lambda s: (fin(s) // nt, fin(s) % nt, 0)
        else:
            q_blk, kv_blk = (None, dil, t // dil, GROUP_W), (None, dil, t // dil, 2 * GROUP_W)
            q_shape, kv_shape = (batch, dil, seq // dil, GROUP_W), (batch, dil, seq // dil, 2 * GROUP_W)
            idx = lambda s: (fin(s) // nt, 0, fin(s) % nt, 0)
        out_specs += [pl.BlockSpec(q_blk, idx), pl.BlockSpec(kv_blk, idx)]
        out_shape += [jax.ShapeDtypeStruct(q_shape, BF16), jax.ShapeDtypeStruct(kv_shape, BF16)]
    for window, dil in ATTN_GROUPS:
        keep = min(window, seq)
        assert keep == seq or keep <= t, "a partial window must fit in the last tile"
        blk = min(keep, t)
        every_tile = keep == seq
        lead = 0 if first_layer else layer
        p_blk = (depth if first_layer else None, None, 2, HEADS_PER_GROUP, HEAD_DIM, blk)
        p_idx = lambda s, every_tile=every_tile: (lead, fin(s) // nt, 0, 0, 0, fin(s) % nt if every_tile else 0)
        out_specs.append(pl.BlockSpec(p_blk, p_idx))
        out_shape.append(jax.ShapeDtypeStruct((depth, batch, 2, HEADS_PER_GROUP, HEAD_DIM, keep), F32))
    n_in = len(in_specs) + len(tabs)
    acc_specs = [] if first_layer else [pl.BlockSpec(memory_space=pl.ANY)] * N_GROUPS
    aliases = {} if first_layer else {n_in + g: 2 * N_GROUPS + g for g in range(N_GROUPS)}
    n_split = 3 * sum(dil > 1 for _, dil in ATTN_GROUPS)
    return pl.pallas_call(
        functools.partial(_qkv_kernel, seq, first_layer),
        grid=(n_tiles + 1,),
        in_specs=in_specs + [tab_spec] * len(tabs) + acc_specs,
        out_specs=out_specs,
        out_shape=out_shape,
        input_output_aliases=aliases,
        scratch_shapes=[pltpu.VMEM((t, 3 * QKV_W), F32), pltpu.VMEM((t, 3 * QKV_W), F32),
                        pltpu.VMEM((n_split, GROUP_W // LANES, t, LANES), F32)],
        compiler_params=pltpu.CompilerParams(dimension_semantics=("arbitrary",), vmem_limit_bytes=VMEM_LIMIT),
        name="prompt_qkv",
    )(x, g_pre, w_in, *tabs, *(() if first_layer else accs))


def _attn_block(q, k, v, bias):
    nk = k.shape[0]
    hid = lax.shift_right_logical(lax.broadcasted_iota(jnp.int32, (BLK, GROUP_W), 1), HEAD_DIM.bit_length() - 1)
    qf = q.astype(F32)
    qs = jnp.concatenate([jnp.where(hid == h, qf, 0.0) for h in range(HEADS_PER_GROUP)], axis=0).astype(BF16)
    s = lax.dot_general(qs, k, (((1,), (1,)), ((), ())), preferred_element_type=F32)
    s = s.reshape(HEADS_PER_GROUP, BLK, nk) + bias[None]
    m = jnp.max(s, axis=-1, keepdims=True)
    p = jnp.exp(s - m)
    l = jnp.sum(p, axis=-1, keepdims=True)
    pv = jnp.dot(p.reshape(HEADS_PER_GROUP * BLK, nk).astype(BF16), v, preferred_element_type=F32)
    pv = pv.reshape(HEADS_PER_GROUP, BLK, GROUP_W)
    inv = 1.0 / l
    lse = m + jnp.log(l)
    o = jnp.zeros((BLK, GROUP_W), F32)
    ls = jnp.zeros((BLK, GROUP_W), F32)
    for h in range(HEADS_PER_GROUP):
        o = jnp.where(hid == h, pv[h] * inv[h], o)
        ls = jnp.where(hid == h, lse[h], ls)
    return o, ls


def _attn_kernel(q1_ref, q2_ref, q3_ref, kv1_ref, kv2_ref, kv3_ref, band_ref, causal_ref, out_ref,
                 o2_s, l2_s, o3_s, l3_s):
    seq = out_ref.shape[0]
    nslab = GROUP_W // LANES
    d2 = ATTN_GROUPS[1][1]
    d3 = ATTN_GROUPS[2][1]
    nblk = seq // BLK
    blocks_per_res2 = nblk // d2
    kcols, vcols = slice(0, GROUP_W), slice(GROUP_W, 2 * GROUP_W)

    def store(o_s, l_s, rows, o, ls):
        for s in range(nslab):
            o_s[s, rows, :] = o[:, s * LANES:(s + 1) * LANES]
            l_s[s, rows, :] = ls[:, s * LANES:(s + 1) * LANES]

    def band_block(q, kv, n):
        if n == 0:
            return _attn_block(q, kv[0:BLK, kcols], kv[0:BLK, vcols], causal_ref[...])
        keys = slice((n - 1) * BLK, (n + 1) * BLK)
        return _attn_block(q, kv[keys, kcols], kv[keys, vcols], band_ref[...])

    for j in range(nblk):
        r, n = divmod(j, blocks_per_res2)
        o, ls = band_block(q2_ref[r, n * BLK:(n + 1) * BLK, :], kv2_ref.at[r], n)
        store(o2_s, l2_s, pl.ds(n * (BLK * d2) + r, BLK, stride=d2), o, ls)
        o, ls = _attn_block(q3_ref[j], kv3_ref[j, :, kcols], kv3_ref[j, :, vcols], causal_ref[...])
        store(o3_s, l3_s, pl.ds(j, BLK, stride=d3), o, ls)

    for j in range(nblk):
        rows = slice(j * BLK, (j + 1) * BLK)
        o1, l1 = band_block(q1_ref[rows, :], kv1_ref, j)
        for s in range(nslab):
            lanes = slice(s * LANES, (s + 1) * LANES)
            l1s, l2, l3 = l1[:, lanes], l2_s[s, rows, :], l3_s[s, rows, :]
            m = jnp.maximum(jnp.maximum(l1s, l2), l3)
            e1, e2, e3 = jnp.exp(l1s - m), jnp.exp(l2 - m), jnp.exp(l3 - m)
            inv = 1.0 / (e1 + e2 + e3)
            out_ref[rows, lanes] = (e1 * inv) * o1[:, lanes] + (e2 * inv) * o2_s[s, rows, :] + (e3 * inv) * o3_s[s, rows, :]


def _attn_call(q1, q2, q3, kv1, kv2, kv3, tables, batch, seq):
    d2 = ATTN_GROUPS[1][1]
    d3 = ATTN_GROUPS[2][1]
    nblk = seq // BLK
    assert seq // d3 == BLK and nblk % d2 == 0 and nblk == d3
    per_seq = lambda a: pl.BlockSpec((None,) + a.shape[1:], lambda b: (b,) + (0,) * (a.ndim - 1))
    scratch = [pltpu.VMEM((GROUP_W // LANES, seq, LANES), F32)] * 4
    return pl.pallas_call(
        _attn_kernel,
        grid=(batch,),
        in_specs=[per_seq(a) for a in (q1, q2, q3, kv1, kv2, kv3)] + [_const_spec(t.shape) for t in tables],
        out_specs=pl.BlockSpec((None, seq, GROUP_W), lambda b: (b, 0, 0)),
        out_shape=jax.ShapeDtypeStruct((batch, seq, GROUP_W), F32),
        scratch_shapes=scratch,
        compiler_params=pltpu.CompilerParams(dimension_semantics=("arbitrary",), vmem_limit_bytes=VMEM_LIMIT),
        name="prompt_attn",
    )(q1, q2, q3, kv1, kv2, kv3, *tables)


def _attn_bias_tables():
    qi = jnp.arange(BLK)[:, None]
    kj = jnp.arange(2 * BLK)[None, :]
    band = jnp.where((kj >= qi) & (kj <= qi + BLK), 0.0, NEG).astype(F32)
    causal = jnp.where(jnp.arange(BLK)[None, :] <= qi, 0.0, NEG).astype(F32)
    return band, causal


def _spatial_mix(ws, v):
    gd, two = GMLP_GROUP_DIM, 2 * LANES
    starts = [g * gd // LANES * LANES for g in range(GMLP_GROUPS)]
    assert all(s + two >= (g + 1) * gd for g, s in enumerate(starts))
    r = [jnp.dot(ws[g * BLK:(g + 1) * BLK], v[:, s:s + two], preferred_element_type=F32) for g, s in enumerate(starts)]
    lane = lax.broadcasted_iota(jnp.int32, (BLK, LANES), 1)
    slabs = []
    for lo in range(0, GMLP_W, LANES):
        owners = sorted({lo // gd, (lo + LANES - 1) // gd})
        pieces = [r[g][:, lo - starts[g]:lo - starts[g] + LANES] for g in owners]
        slabs.append(pieces[0] if len(owners) == 1 else jnp.where(lane < owners[1] * gd - lo, pieces[0], pieces[1]))
    return jnp.concatenate(slabs, axis=-1)


def _sample_attend(q, kvn, caches):
    column = lambda v, lo: v[lo:lo + GROUP_W].reshape(HEADS_PER_GROUP, HEAD_DIM, 1)
    outs, lses = [], []
    for g, (window, dil) in enumerate(ATTN_GROUPS):
        qg = column(q, g * GROUP_W)
        kc, vc = caches[g]
        kn, vn = column(kvn[g], 0), column(kvn[g], GROUP_W)
        rows = kc.shape[-1]
        sc = jnp.sum(kc * qg, axis=1, keepdims=True)
        r = lax.broadcasted_iota(jnp.int32, (1, 1, rows), 2)
        sc = jnp.where((r & (dil - 1)) == 0, sc, NEG)
        ss = jnp.sum(kn * qg, axis=1, keepdims=True)
        m = jnp.maximum(jnp.max(sc, axis=-1, keepdims=True), ss)
        pc = jnp.exp(sc - m)
        ps = jnp.exp(ss - m)
        l = jnp.sum(pc, axis=-1, keepdims=True) + ps
        o = jnp.sum(vc * pc, axis=-1, keepdims=True) + ps * vn
        outs.append(o / l)
        lses.append(m + jnp.log(l))
    m = jnp.maximum(jnp.maximum(lses[0], lses[1]), lses[2])
    es = [jnp.exp(l - m) for l in lses]
    inv = 1.0 / (es[0] + es[1] + es[2])
    merged = (es[0] * inv) * outs[0] + (es[1] * inv) * outs[1] + (es[2] * inv) * outs[2]
    return merged.reshape(GROUP_W, 1)


def _mlp_kernel(x_ref, attn_ref, gpre_ref, wlo_ref, whi_ref, bm_ref, vng_ref, vnb_ref, ws_ref, bsp_ref,
                wpa_ref, wpb_ref, wout_ref, gpost_ref, qt_ref, kt1_ref, kt2_ref, kt3_ref, c1_ref, c2_ref, c3_ref,
                wsrc_ref, out_ref, sattn_ref, wnext_ref, sacc_s):
    step = pl.program_id(0)
    wnext_ref[...] = wsrc_ref[...].astype(BF16)

    @pl.when(step == 0)
    def _():
        sacc_s[...] = jnp.zeros(sacc_s.shape, F32)

    lane = lax.broadcasted_iota(jnp.int32, (1, sacc_s.shape[-1]), 1)
    for k in range(c1_ref.shape[0]):
        sel = lane == step * c1_ref.shape[0] + k
        pick = lambda ref: jnp.sum(jnp.where(sel, ref[...], 0.0), axis=-1, keepdims=True)
        res = _sample_attend(pick(qt_ref), [pick(r) for r in (kt1_ref, kt2_ref, kt3_ref)],
                             [(c[k, 0], c[k, 1]) for c in (c1_ref, c2_ref, c3_ref)])
        sacc_s[...] = jnp.where(sel, res, sacc_s[...])
    sattn_ref[...] = sacc_s[...]

    x = x_ref[...]
    rows = x.shape[0]
    hb = _rms(x, gpre_ref[...]).astype(BF16)
    z = jnp.concatenate([jnp.dot(hb, wlo_ref[...], preferred_element_type=F32),
                         jnp.dot(hb, whi_ref[...], preferred_element_type=F32)], axis=-1)
    ga, u, vn, gb, gates = _activations(z, bm_ref[...], vng_ref[...], vnb_ref[...])
    wr = lax.broadcasted_iota(jnp.int32, ws_ref.shape, 0) & (BLK - 1)
    wc = lax.broadcasted_iota(jnp.int32, ws_ref.shape, 1)
    ws = jnp.where(wc <= wr, ws_ref[...], 0.0).astype(BF16)
    vnb16 = vn.astype(BF16)
    mix = jnp.concatenate([_spatial_mix(ws, vnb16[c * BLK:(c + 1) * BLK]) + bsp_ref[...]
                           for c in range(rows // BLK)], axis=0)
    out_ref[...] = _merge_out(x, attn_ref[...] * ga, (u * mix) * gb, gates,
                              wpa_ref[...], wpb_ref[...], wout_ref[...], gpost_ref[...])


def _mlp_call(layer, x, attn, g_pre, w_in, b_merge, vng, vnb, ws_stack, bs_full, wpa, wpb, wout, g_post,
              sample_q, sample_kv, caches, w_in_f32):
    n, d = x.shape
    t = MLP_TILE
    steps = n // t
    depth, w_rows, w_cols = w_in_f32.shape
    assert w_rows % (steps * 16) == 0, "each step converts one bf16-tileable row slab of the next weight"
    slab = w_rows // steps
    next_layer = min(layer + 1, depth - 1)
    nb = sample_q.shape[-1]
    assert nb % steps == 0, "every grid step carries the same number of samples"
    spb = nb // steps
    row = lambda i: (i, 0)
    rest_w = w_in.shape[-1] - 3 * QKV_W
    assert rest_w == 2 * 3 * QKV_W, "rest columns are addressed as column blocks 1 and 2 of width 3 * QKV_W"
    w_specs = [_layer_spec(w_in, layer, 3 * QKV_W, 1), _layer_spec(w_in, layer, 3 * QKV_W, 2)]
    stacks = (b_merge, vng, vnb, ws_stack, bs_full, wpa, wpb, wout, g_post)
    cache_specs, cache_views = [], []
    for (window, dil), c in zip(ATTN_GROUPS, caches):
        rows = c.shape[2]
        assert rows == window and window % dil == 0 and dil & (dil - 1) == 0, "cache must hold exactly the window"
        cache_views.append(jnp.transpose(c, (0, 1, 3, 4, 5, 2)))
        cache_specs.append(pl.BlockSpec((None, spb, 2, HEADS_PER_GROUP, HEAD_DIM, rows),
                                        lambda i: (layer, i, 0, 0, 0, 0)))
    return pl.pallas_call(
        _mlp_kernel,
        grid=(steps,),
        in_specs=[pl.BlockSpec((t, d), row), pl.BlockSpec((t, GROUP_W), row), _layer_spec(g_pre, layer)] + w_specs
        + [_layer_spec(s, layer) for s in stacks]
        + [_const_spec(sample_q.shape)] + [_const_spec(kv.shape) for kv in sample_kv] + cache_specs
        + [pl.BlockSpec((None, slab, w_cols), lambda i: (next_layer, i, 0))],
        out_specs=[pl.BlockSpec((t, d), row), _const_spec((GROUP_W, nb)), pl.BlockSpec((slab, w_cols), row)],
        out_shape=[jax.ShapeDtypeStruct((n, d), F32), jax.ShapeDtypeStruct((GROUP_W, nb), F32),
                   jax.ShapeDtypeStruct((w_rows, w_cols), BF16)],
        scratch_shapes=[pltpu.VMEM((GROUP_W, nb), F32)],
        compiler_params=pltpu.CompilerParams(dimension_semantics=("arbitrary",), vmem_limit_bytes=VMEM_LIMIT),
        name="prompt_mlp",
    )(x, attn, g_pre, w_in, w_in, *stacks, sample_q, *sample_kv, *cache_views, w_in_f32)


def _sample_in_kernel(x_ref, g_ref, w_ref, tq_ref, tk_ref,
                      bm_ref, vng_ref, vnb_ref,
                      kv1_ref, kv2_ref, kv3_ref, qt_ref, kvt1_ref, kvt2_ref, kvt3_ref,
                      ga_ref, u_ref, vn_ref, gb_ref, gates_ref):
    hb = _rms(x_ref[...], g_ref[...]).astype(BF16)
    z = jnp.dot(hb, w_ref[...], preferred_element_type=F32)

    def store_tiles(dst_ref, rows):
        dst_ref[...] = rows.T

    qs = []
    for g, (kv_ref, kvt_ref) in enumerate(((kv1_ref, kvt1_ref), (kv2_ref, kvt2_ref), (kv3_ref, kvt3_ref))):
        lo = g * GROUP_W
        qs.append(_rope(z[:, lo:lo + GROUP_W], tq_ref[...]))
        k = _rope(z[:, QKV_W + lo:QKV_W + lo + GROUP_W], tk_ref[...])
        kv = jnp.concatenate([k, z[:, 2 * QKV_W + lo:2 * QKV_W + lo + GROUP_W]], axis=-1)
        kv_ref[...] = kv
        store_tiles(kvt_ref, kv)
    store_tiles(qt_ref, jnp.concatenate(qs, axis=-1))
    ga, u, vn, gb, gates = _activations(z[:, 3 * QKV_W:], bm_ref[...], vng_ref[...], vnb_ref[...])
    ga_ref[...] = ga
    u_ref[...] = u
    vn_ref[...] = vn
    gb_ref[...] = gb
    gates_ref[...] = gates


def _sample_in_call(layer, x, g_pre, w_in, tabs_q, tabs_k, b_merge, vng, vnb):
    n, d = x.shape
    shapes = ([(n, 2 * GROUP_W)] * 3 + [(QKV_W, n)] + [(2 * GROUP_W, n)] * 3
              + [(n, w) for w in (GROUP_W, GMLP_W, GMLP_W, GMLP_W, 2 * d)])
    tabs = (tabs_q, tabs_k)
    return pl.pallas_call(
        _sample_in_kernel,
        grid=(1,),
        in_specs=[_const_spec(x.shape), _layer_spec(g_pre, layer), _layer_spec(w_in, layer)]
        + [_const_spec(t.shape) for t in tabs] + [_layer_spec(s, layer) for s in (b_merge, vng, vnb)],
        out_specs=[_const_spec(s) for s in shapes],
        out_shape=[jax.ShapeDtypeStruct(s, F32) for s in shapes],
        compiler_params=pltpu.CompilerParams(dimension_semantics=("arbitrary",), vmem_limit_bytes=VMEM_LIMIT),
        name="sample_in",
    )(x, g_pre, w_in, *tabs, b_merge, vng, vnb)


def _sample_out_kernel(x_ref, attn_ref, ga_ref, u_ref, vn_ref, gb_ref, gates_ref, ws0_ref, bs0_ref,
                       wpa_ref, wpb_ref, wout_ref, gpost_ref, out_ref):
    mix = vn_ref[...] * ws0_ref[...] + bs0_ref[...]
    attn = attn_ref[...].T
    out_ref[...] = _merge_out(x_ref[...], attn * ga_ref[...], (u_ref[...] * mix) * gb_ref[...],
                              gates_ref[...], wpa_ref[...], wpb_ref[...], wout_ref[...], gpost_ref[...])


def _sample_out_call(layer, x, attn, ga, u, vn, gb, gates, ws0, bs0, wpa, wpb, wout, g_post):
    acts = (x, attn, ga, u, vn, gb, gates)
    stacks = (ws0, bs0, wpa, wpb, wout, g_post)
    return pl.pallas_call(
        _sample_out_kernel,
        grid=(1,),
        in_specs=[_const_spec(a.shape) for a in acts] + [_layer_spec(s, layer) for s in stacks],
        out_specs=_const_spec(x.shape),
        out_shape=jax.ShapeDtypeStruct(x.shape, F32),
        compiler_params=pltpu.CompilerParams(dimension_semantics=("arbitrary",), vmem_limit_bytes=VMEM_LIMIT),
        name="sample_out",
    )(*acts, *stacks)


def kernel(x_prompt, x_sample, cache_kv_w128, cache_kv_w512, cache_kv_w2048, norm_pre, w_in, b_merge, v_norm_g, v_norm_b, w_spatial, b_spatial, w_proj_a, w_proj_b, w_out, norm_post):
    batch, seq, d = x_prompt.shape
    nb, dec_seq, _ = x_sample.shape
    depth = w_in.shape[0]
    assert dec_seq == 1, "the sample group decodes one position per step"
    assert seq % QKV_TILE == 0 and (batch * seq) % MLP_TILE == 0
    caches = (cache_kv_w128, cache_kv_w512, cache_kv_w2048)

    w_in_l = w_in[0].astype(BF16)
    wpa_b, wpb_b, wout_b = w_proj_a.astype(BF16), w_proj_b.astype(BF16), w_out.astype(BF16)
    ws_stack = w_spatial.reshape(depth, GMLP_GROUPS * BLK, BLK)
    chan_group = jnp.arange(GMLP_W) // GMLP_GROUP_DIM
    bs_full = jnp.swapaxes(b_spatial, 1, 2)[:, :, chan_group]
    ws0 = w_spatial[:, :, 0, 0][:, chan_group][:, None, :]
    bs0 = b_spatial[:, :, 0][:, chan_group][:, None, :]
    row2 = lambda a: a[:, None, :]
    g_pre, g_post, bm, vng, vnb = row2(norm_pre), row2(norm_post), row2(b_merge), row2(v_norm_g), row2(v_norm_b)

    scale = HEAD_DIM ** -0.5
    pos_p = jnp.arange(seq, dtype=jnp.int32)
    pos_s = PAST_LEN + jnp.arange(dec_seq, dtype=jnp.int32)
    tq_p, tk_p = _rope_tables(pos_p, scale), _rope_tables(pos_p, 1.0)
    tq_s, tk_s = _rope_tables(pos_s, scale), _rope_tables(pos_s, 1.0)
    attn_tables = _attn_bias_tables()

    xp = x_prompt
    xs = x_sample.reshape(nb, d)
    kv_p = None
    kv_s = [[] for _ in ATTN_GROUPS]
    v_s = []
    for l in range(depth):
        q1, kv1, q2, kv2, q3, kv3, *kv_p = _qkv_call(l, xp, g_pre, w_in_l, (tq_p, tk_p), kv_p)
        attn = _attn_call(q1, q2, q3, kv1, kv2, kv3, attn_tables, batch, seq)
        kn1, kn2, kn3, qt, kt1, kt2, kt3, ga, u, vn, gb, gates = _sample_in_call(l, xs, g_pre, w_in_l, tq_s, tk_s,
                                                                                 bm, vng, vnb)
        xp, attn_s, w_in_l = _mlp_call(l, xp.reshape(batch * seq, d), attn.reshape(batch * seq, GROUP_W), g_pre,
                                       w_in_l, bm, vng, vnb, ws_stack, bs_full, wpa_b, wpb_b, wout_b, g_post,
                                       qt, (kt1, kt2, kt3), caches, w_in)
        xp = xp.reshape(batch, seq, d)
        xs = _sample_out_call(l, xs, attn_s, ga, u, vn, gb, gates, ws0, bs0, wpa_b, wpb_b, wout_b, g_post)
        for g, kn in enumerate((kn1, kn2, kn3)):
            kv_s[g].append(kn.reshape(nb, dec_seq, 2, HEADS_PER_GROUP, HEAD_DIM))
        v_s.append(vn.reshape(nb, dec_seq, GMLP_W))

    new_kv_p = [jnp.transpose(p, (0, 1, 5, 2, 3, 4)) for p in kv_p]
    return (xp, xs.reshape(nb, dec_seq, d), new_kv_p[0], new_kv_p[1], new_kv_p[2],
            jnp.stack(kv_s[0]), jnp.stack(kv_s[1]), jnp.stack(kv_s[2]), jnp.stack(v_s))
```
